```python
import jax
import jax.numpy as jnp
from jax import lax
import numpy as np

D_MODEL = 2048
BATCH = 8
SEQ = 2048
DEPTH = 1

GRID_W = 64
CTX_LEN = 256
MIX_GROUP = D_MODEL // 2
HEADS = 8
HEAD_DIM = MIX_GROUP // HEADS
N_PARTS = 8
P_QN, P_KN, P_VN, P_QH, P_FF, P_FB, P_IH, P_GH = range(N_PARTS)
WIN_R = 8
WIN_C = 16
Q_COLS = 16
K_COLS = 32
CHUNK = 64
ROPE_THETA = 10000.0
N_EXPERTS = 32
TOP_K = 4
MOE_FF = D_MODEL
SWIGLU_LIMIT = 7.0
SWIGLU_ALPHA = 1.702
MOE_BLOCK = 128
EPS = 1e-6

kernel_name = 'hybrid_natten_hgrn2_moe_block'


def rms_norm(x, gain):
    xf = x.astype(jnp.float32)
    y = xf * lax.rsqrt(jnp.mean(xf * xf, axis=-1, keepdims=True) + EPS)
    return (y * gain.astype(jnp.float32)).astype(x.dtype)


def modulate(h, shift, scale):
    return h * (1.0 + scale) + shift


def flip(a):
    return jnp.flip(a, axis=1)


def in_proj(h, w_in, parts):
    w = w_in.reshape(w_in.shape[0], N_PARTS, MIX_GROUP)[:, np.asarray(parts)]
    p = jnp.einsum('bld,dpe->pble', h, w)
    b, l = h.shape[0], h.shape[1]
    return {name: p[i].reshape(b, l, HEADS, HEAD_DIM) for i, name in enumerate(parts)}


def axial_rope_angles(l):
    t = jnp.arange(l)
    n_freq = HEAD_DIM // 4
    inv_freq = ROPE_THETA ** (-jnp.arange(n_freq, dtype=jnp.float32) / n_freq)
    ang_row = (t // GRID_W).astype(jnp.float32)[:, None] * inv_freq
    ang_col = (t % GRID_W).astype(jnp.float32)[:, None] * inv_freq
    return ang_row, ang_col


def rotate(x, ang):
    half = x.shape[-1] // 2
    cos = jnp.cos(ang)[:, None, :]
    sin = jnp.sin(ang)[:, None, :]
    x1, x2 = x[..., :half], x[..., half:]
    return jnp.concatenate([x1 * cos - x2 * sin, x2 * cos + x1 * sin], axis=-1)


def axial_rope(x, ang_row, ang_col):
    half = x.shape[-1] // 2
    return jnp.concatenate([rotate(x[..., :half], ang_row), rotate(x[..., half:], ang_col)], axis=-1)


def neighborhood_attention(q, k, v, k_ctx, v_ctx, rpb):
    b, l, h, dh = q.shape
    rows = l // GRID_W
    kr = min(WIN_R, rows)
    ncb = GRID_W // Q_COLS
    r = np.arange(rows)
    key_rows = np.clip(r - kr // 2, 0, rows - kr)[:, None] + np.arange(kr)
    kc0 = np.clip(np.arange(ncb) * Q_COLS - WIN_C // 2, 0, GRID_W - K_COLS)
    key_cols = kc0[:, None] + np.arange(K_COLS)
    nk = kr * K_COLS
    key_idx = (key_rows[:, None, :, None] * GRID_W + key_cols[None, :, None, :]).reshape(rows, ncb, nk)
    kg = k[:, key_idx]
    vg = v[:, key_idx]
    qb = q.reshape(b, rows, ncb, Q_COLS, h, dh)
    scale = dh ** -0.5
    s_win = jnp.einsum('brjqhd,brjkhd->bhrjqk', qb, kg).astype(jnp.float32) * scale
    q_cols = np.arange(GRID_W).reshape(ncb, Q_COLS)
    col_start = np.clip(q_cols - WIN_C // 2, 0, GRID_W - WIN_C)
    kcol = np.tile(key_cols, (1, kr))
    krow = np.repeat(key_rows, K_COLS, axis=1)
    in_win = (kcol[:, None, :] >= col_start[:, :, None]) & (kcol[:, None, :] < col_start[:, :, None] + WIN_C)
    d_row = krow - r[:, None] + WIN_R - 1
    d_col = np.clip(kcol[:, None, :] - q_cols[:, :, None] + WIN_C - 1, 0, 2 * WIN_C - 2)
    bias = rpb[:, d_row[:, None, None, :], d_col[None, :, :, :]].astype(jnp.float32)
    s_win = jnp.where(in_win[None, None, None], s_win + bias[None], -jnp.inf)
    s_ctx = jnp.einsum('brjqhd,bchd->bhrjqc', qb, k_ctx).astype(jnp.float32) * scale
    p = jax.nn.softmax(jnp.concatenate([s_win, s_ctx], axis=-1), axis=-1).astype(v.dtype)
    o = (jnp.einsum('bhrjqk,brjkhd->brjqhd', p[..., :nk], vg)
         + jnp.einsum('bhrjqc,bchd->brjqhd', p[..., nk:], v_ctx))
    return o.reshape(b, l, h, dh)


def context_attention(q, k, v):
    s = jnp.einsum('bqhd,bkhd->bhqk', q, k).astype(jnp.float32) * (q.shape[-1] ** -0.5)
    p = jax.nn.softmax(s, axis=-1).astype(v.dtype)
    return jnp.einsum('bhqk,bkhd->bqhd', p, v)


def forget_gate(f_raw, lb):
    f_raw = f_raw.astype(jnp.float32)
    log_f = jnp.log(lb + (1.0 - lb) * jax.nn.sigmoid(f_raw))
    key = (1.0 - lb) * jax.nn.sigmoid(-f_raw)
    return log_f, key


def gla_chunk_scan(q, k, v, g, s0):
    b, l, h, dk = q.shape
    dv = v.shape[-1]
    n = l // CHUNK
    mask = jnp.tril(jnp.ones((CHUNK, CHUNK), dtype=bool))[None, :, :, None, None]

    def to_chunks(a):
        return jnp.moveaxis(a.reshape(b, n, CHUNK, h, a.shape[-1]), 1, 0)

    def step(s, inp):
        qc, kc, vc, gc = inp
        cum = jnp.cumsum(gc, axis=1)
        cum_last = cum[:, -1]
        o_inter = jnp.einsum('bthk,bhkv->bthv', qc * jnp.exp(cum), s)
        decay = jnp.exp(jnp.where(mask, cum[:, :, None] - cum[:, None, :], -jnp.inf))
        att = jnp.sum(qc[:, :, None] * kc[:, None] * decay, axis=-1)
        o_intra = jnp.einsum('btsh,bshv->bthv', att, vc)
        k_dec = kc * jnp.exp(cum_last[:, None] - cum)
        s_new = jnp.exp(cum_last)[..., None] * s + jnp.einsum('bshk,bshv->bhkv', k_dec, vc)
        return s_new, o_inter + o_intra

    s_final, o = lax.scan(step, s0, (to_chunks(q), to_chunks(k), to_chunks(v), to_chunks(g)))
    return jnp.moveaxis(o, 0, 1).reshape(b, l, h, dv), s_final


def gla_final_state(k, v, g):
    cum = jnp.cumsum(g, axis=1)
    return jnp.einsum('blhk,blhv->bhkv', k * jnp.exp(cum[:, -1:] - cum), v)


def hgrn2_output(o, gate, gain, dtype):
    return (rms_norm(o, gain) * jax.nn.silu(gate.astype(jnp.float32))).astype(dtype)


def hybrid_mixer(hx, hc, w_in, lb, hg_gain, rpb, w_out, ang_row, ang_col, with_ctx_out):
    b, l, _ = hx.shape
    px = in_proj(hx, w_in, tuple(range(N_PARTS)))
    ctx_parts = tuple(range(N_PARTS)) if with_ctx_out else (P_KN, P_VN, P_FF, P_FB, P_IH)
    pc = in_proj(hc, w_in, ctx_parts)
    o_na = neighborhood_attention(px[P_QN], px[P_KN], px[P_VN], pc[P_KN], pc[P_VN], rpb)
    lb_f, lb_b = lb.reshape(2, HEADS, HEAD_DIM)
    gf_c, kf_c = forget_gate(pc[P_FF], lb_f)
    gb_c, kb_c = forget_gate(pc[P_FB], lb_b)
    v_c = pc[P_IH].astype(jnp.float32)
    if with_ctx_out:
        q_c = pc[P_QH].astype(jnp.float32)
        zero = jnp.zeros((b, HEADS, HEAD_DIM, HEAD_DIM), jnp.float32)
        oc_f, s_f = gla_chunk_scan(q_c, kf_c, v_c, gf_c, zero)
        oc_b, s_b = gla_chunk_scan(flip(q_c), flip(kb_c), flip(v_c), flip(gb_c), zero)
        oc_hg = oc_f + flip(oc_b)
    else:
        s_f = gla_final_state(kf_c, v_c, gf_c)
        s_b = gla_final_state(flip(kb_c), flip(v_c), flip(gb_c))
    gf, kf = forget_gate(px[P_FF], lb_f)
    gb, kb = forget_gate(px[P_FB], lb_b)
    q_x = axial_rope(px[P_QH].astype(jnp.float32), ang_row, ang_col)
    kf = axial_rope(kf, ang_row, ang_col)
    kb = axial_rope(kb, ang_row, ang_col)
    v_x = px[P_IH].astype(jnp.float32)
    o_f, _ = gla_chunk_scan(q_x, kf, v_x, gf, s_f)
    o_b, _ = gla_chunk_scan(flip(q_x), flip(kb), flip(v_x), flip(gb), s_b)
    o_hg = hgrn2_output(o_f + flip(o_b), px[P_GH], hg_gain, hx.dtype)
    y = jnp.concatenate([o_na.reshape(b, l, MIX_GROUP), o_hg.reshape(b, l, MIX_GROUP)], axis=-1) @ w_out
    if not with_ctx_out:
        return y, None
    lc = hc.shape[1]
    oc_na = context_attention(pc[P_QN], pc[P_KN], pc[P_VN])
    oc_hg = hgrn2_output(oc_hg, pc[P_GH], hg_gain, hc.dtype)
    yc = jnp.concatenate([oc_na.reshape(b, lc, MIX_GROUP), oc_hg.reshape(b, lc, MIX_GROUP)], axis=-1) @ w_out
    return y, yc


def moe_ffn(h, w_router, b_router, w_gu, b_gu, w_down, b_down):
    t, d = h.shape
    logits = (h @ w_router + b_router).astype(jnp.float32)
    top_val, top_idx = lax.top_k(logits, TOP_K)
    gate = jax.nn.softmax(top_val, axis=-1)
    a = t * TOP_K
    exp_flat = top_idx.reshape(a)
    order = jnp.argsort(exp_flat)
    e_sorted = exp_flat[order]
    tok_sorted = (order // TOP_K).astype(jnp.int32)
    w_sorted = gate.reshape(a)[order].astype(h.dtype)
    counts = jnp.bincount(exp_flat, length=N_EXPERTS)
    padded = (counts + MOE_BLOCK - 1) // MOE_BLOCK * MOE_BLOCK
    start = jnp.cumsum(counts) - counts
    pad_end = jnp.cumsum(padded)
    pad_start = pad_end - padded
    dest = pad_start[e_sorted] + (jnp.arange(a) - start[e_sorted])
    n_blocks = -(-(a + N_EXPERTS * (MOE_BLOCK - 1)) // MOE_BLOCK)
    cap = n_blocks * MOE_BLOCK
    slot_tok = jnp.full((cap,), t, jnp.int32).at[dest].set(tok_sorted)
    slot_w = jnp.zeros((cap,), h.dtype).at[dest].set(w_sorted)
    block_exp = jnp.minimum(jnp.searchsorted(pad_end, jnp.arange(n_blocks) * MOE_BLOCK, side='right'), N_EXPERTS - 1)
    h_pad = jnp.concatenate([h, jnp.zeros((1, d), h.dtype)], axis=0)
    xb = h_pad[slot_tok].reshape(n_blocks, MOE_BLOCK, d)

    def expert_block(args):
        xe, e = args
        gu = xe @ w_gu[e] + b_gu[e]
        x_glu = jnp.minimum(gu[:, :MOE_FF], SWIGLU_LIMIT)
        x_lin = jnp.clip(gu[:, MOE_FF:], -SWIGLU_LIMIT, SWIGLU_LIMIT)
        y = x_glu * jax.nn.sigmoid(SWIGLU_ALPHA * x_glu) * (x_lin + 1.0)
        return y @ w_down[e] + b_down[e]

    yb = lax.map(expert_block, (xb, block_exp)).reshape(cap, d)
    out = jnp.zeros((t + 1, d), h.dtype).at[slot_tok].add(yb * slot_w[:, None])
    return out[:t]


def setup_inputs(seed: int = 0) -> dict:
    key = jax.random.key(seed)
    ks = jax.random.split(key, 20)
    d = D_MODEL

    def nrm(k, shape, s):
        return jax.random.normal(k, shape, jnp.float32) * s

    return {
        'x': nrm(ks[0], (BATCH, SEQ, d), 1.0),
        'c': nrm(ks[1], (BATCH, d), 1.0),
        'ctx': nrm(ks[2], (BATCH, CTX_LEN, d), 1.0),
        'c_ctx': nrm(ks[3], (d,), 1.0),
        'w_ada': nrm(ks[4], (DEPTH, d, 6 * d), 0.5 * d ** -0.5),
        'b_ada': nrm(ks[5], (DEPTH, 6 * d), 0.02),
        'norm_mix': 1.0 + nrm(ks[6], (DEPTH, d), 0.02),
        'norm_ffn': 1.0 + nrm(ks[7], (DEPTH, d), 0.02),
        'w_in': nrm(ks[8], (DEPTH, d, N_PARTS * MIX_GROUP), d ** -0.5),
        'lb_table': nrm(ks[9], (DEPTH + 1, 2 * MIX_GROUP), 0.5),
        'hg_norm': 1.0 + nrm(ks[10], (DEPTH, HEAD_DIM), 0.02),
        'rpb': nrm(ks[11], (DEPTH, HEADS, 2 * WIN_R - 1, 2 * WIN_C - 1), 0.02),
        'w_out': nrm(ks[12], (DEPTH, 2 * MIX_GROUP, d), (2 * MIX_GROUP) ** -0.5),
        'w_router': nrm(ks[13], (DEPTH, d, N_EXPERTS), d ** -0.5),
        'b_router': nrm(ks[14], (DEPTH, N_EXPERTS), 0.01),
        'w_gu': nrm(ks[15], (DEPTH, N_EXPERTS, d, 2 * MOE_FF), d ** -0.5),
        'b_gu': nrm(ks[16], (DEPTH, N_EXPERTS, 2 * MOE_FF), 0.01),
        'w_down': nrm(ks[17], (DEPTH, N_EXPERTS, MOE_FF, d), MOE_FF ** -0.5),
        'b_down': nrm(ks[18], (DEPTH, N_EXPERTS, d), 0.01),
        'norm_final': 1.0 + nrm(ks[19], (d,), 0.02),
    }


def reference(x, c, ctx, c_ctx, w_ada, b_ada, norm_mix, norm_ffn, w_in, lb_table, hg_norm, rpb,
              w_out, w_router, b_router, w_gu, b_gu, w_down, b_down, norm_final):
    b, l, d = x.shape
    lc = ctx.shape[1]
    ang_row, ang_col = axial_rope_angles(l)
    lower_bounds = jnp.cumsum(jax.nn.softmax(lb_table.astype(jnp.float32), axis=0), axis=0)
    cond = jax.nn.silu(c)
    cond_ctx = jax.nn.silu(c_ctx)
    h_ctx = ctx
    for layer in range(DEPTH):
        last = layer == DEPTH - 1
        mod = cond @ w_ada[layer] + b_ada[layer]
        sh1, sc1, g1, sh2, sc2, g2 = jnp.split(mod[:, None, :], 6, axis=-1)
        n_ctx_mod = 2 if last else 6
        mod_ctx = jnp.split(cond_ctx @ w_ada[layer][:, :n_ctx_mod * d] + b_ada[layer][:n_ctx_mod * d], n_ctx_mod)
        hx = modulate(rms_norm(x, norm_mix[layer]), sh1, sc1)
        hc = modulate(rms_norm(h_ctx, norm_mix[layer]), mod_ctx[0], mod_ctx[1])
        y, yc = hybrid_mixer(hx, hc, w_in[layer], lower_bounds[layer], hg_norm[layer], rpb[layer],
                             w_out[layer], ang_row, ang_col, not last)
        x = x + g1 * y
        hx = modulate(rms_norm(x, norm_ffn[layer]), sh2, sc2)
        moe_params = (w_router[layer], b_router[layer], w_gu[layer], b_gu[layer], w_down[layer], b_down[layer])
        if last:
            x = x + g2 * moe_ffn(hx.reshape(b * l, d), *moe_params).reshape(b, l, d)
        else:
            h_ctx = h_ctx + mod_ctx[2] * yc
            hc = modulate(rms_norm(h_ctx, norm_ffn[layer]), mod_ctx[3], mod_ctx[4])
            tokens = jnp.concatenate([hc.reshape(b * lc, d), hx.reshape(b * l, d)], axis=0)
            out = moe_ffn(tokens, *moe_params)
            h_ctx = h_ctx + mod_ctx[5] * out[:b * lc].reshape(b, lc, d)
            x = x + g2 * out[b * lc:].reshape(b, l, d)
    return rms_norm(x, norm_final)
```

```python
import functools

import numpy as np
import jax
import jax.numpy as jnp
from jax import lax
from jax.experimental import pallas as pl
from jax.experimental.pallas import tpu as pltpu

f32 = jnp.float32
bf16 = jnp.bfloat16
u32 = jnp.uint32

GRID_W = 64
HEADS = 8
HEAD_DIM = 128
WIN_R = 8
WIN_C = 16
ROPE_THETA = 10000.0
N_EXPERTS = 32
TOP_K = 4
SWIGLU_LIMIT = 7.0
SWIGLU_ALPHA = 1.702
EPS = 1e-6
P_QN, P_KN, P_VN, P_QH, P_FF, P_FB, P_IH, P_GH = range(8)

CHUNK = 64
SUB = 16
MOE_TM = 256
NEG_BIG = -1e30
HI_MASK = 0xFFFF0000

_HIGHEST = lax.Precision.HIGHEST


def _cparams(sem, vmem_mb=None):
    kw = dict(dimension_semantics=sem)
    if vmem_mb is not None:
        kw["vmem_limit_bytes"] = vmem_mb * 1024 * 1024
    return pltpu.CompilerParams(**kw)


def _ada_kernel(c_ref, w_ref, b_ref, o_ref):
    c = c_ref[...]
    cond = c * jax.nn.sigmoid(c)
    o_ref[...] = jnp.dot(cond, w_ref[...], precision=_HIGHEST, preferred_element_type=f32) + b_ref[...]


def _ada(c16, w_ada, b_ada):
    d, n = w_ada.shape
    tn = min(1024, n)
    return pl.pallas_call(
        _ada_kernel,
        out_shape=jax.ShapeDtypeStruct((c16.shape[0], n), f32),
        grid=(n // tn,),
        in_specs=[pl.BlockSpec((c16.shape[0], d), lambda j: (0, 0)),
                  pl.BlockSpec((d, tn), lambda j: (0, j)),
                  pl.BlockSpec((1, tn), lambda j: (0, j))],
        out_specs=pl.BlockSpec((c16.shape[0], tn), lambda j: (0, j)),
        compiler_params=_cparams(("parallel",), 40),
        name="ada",
    )(c16, w_ada, b_ada.reshape(1, n))


def _inproj_kernel(x_ref, shift_ref, scale_ref, gain_ref, w_ref, o_ref, h_scr):
    @pl.when(pl.program_id(2) == 0)
    def _():
        x = x_ref[...]
        ms = jnp.mean(x * x, axis=-1, keepdims=True)
        y = x * lax.rsqrt(ms + EPS) * gain_ref[...]
        h_scr[...] = (y * (1.0 + scale_ref[...]) + shift_ref[...]).astype(bf16)

    acc = jnp.dot(h_scr[...], w_ref[...], preferred_element_type=f32)
    for hh in range(HEADS):
        o_ref[hh] = acc[:, hh * HEAD_DIM:(hh + 1) * HEAD_DIM].astype(o_ref.dtype)


def _inproj(x, shift, scale, gain, w_bf, part_lo, n_parts, out_dtype, tm):
    b, l, d = x.shape
    pw = HEADS * HEAD_DIM
    tm = min(tm, l)
    return pl.pallas_call(
        _inproj_kernel,
        out_shape=jax.ShapeDtypeStruct((b, n_parts * HEADS, l, HEAD_DIM), out_dtype),
        grid=(b, l // tm, n_parts),
        in_specs=[pl.BlockSpec((None, tm, d), lambda bi, mi, ni: (bi, mi, 0)),
                  pl.BlockSpec((None, 1, d), lambda bi, mi, ni: (bi, 0, 0)),
                  pl.BlockSpec((None, 1, d), lambda bi, mi, ni: (bi, 0, 0)),
                  pl.BlockSpec((1, d), lambda bi, mi, ni: (0, 0)),
                  pl.BlockSpec((d, pw), lambda bi, mi, ni: (0, part_lo + ni))],
        out_specs=pl.BlockSpec((None, HEADS, tm, HEAD_DIM), lambda bi, mi, ni: (bi, ni, mi, 0)),
        scratch_shapes=[pltpu.VMEM((tm, d), bf16)],
        compiler_params=_cparams(("parallel", "parallel", "arbitrary"), 48),
        name="inproj",
    )(x, shift, scale, gain, w_bf)


def _natten_bias_table(rpb, rows):
    kr = min(WIN_R, rows)
    q = np.arange(GRID_W)
    col_start = np.clip(q - WIN_C // 2, 0, GRID_W - WIN_C)
    kc = np.arange(GRID_W)
    in_win = (kc[None, :] >= col_start[:, None]) & (kc[None, :] < col_start[:, None] + WIN_C)
    d_col = np.clip(kc[None, :] - q[:, None] + WIN_C - 1, 0, 2 * WIN_C - 2)
    n_d0 = 2 * WIN_R - 1 - (kr - 1)
    d_row = np.arange(n_d0)[:, None] + np.arange(kr)[None, :]
    t = rpb.astype(f32)[:, d_row[:, :, None, None], d_col[None, None, :, :]]
    t = jnp.where(in_win[None, None, None], t, -jnp.inf)
    t = jnp.transpose(t, (0, 1, 3, 2, 4))
    return t.reshape(rpb.shape[0], n_d0, GRID_W, kr * GRID_W)


def _natten_kernel(q_ref, k_ref, v_ref, kc_ref, vc_ref, bias_ref, o_ref, *, rows, kr):
    scale = HEAD_DIM ** -0.5
    kc = kc_ref[...]
    vc = vc_ref[...]
    nt = (((1,), (1,)), ((), ()))

    def body(r, carry):
        kr0 = jnp.clip(r - kr // 2, 0, rows - kr)
        d0 = kr0 - r + WIN_R - 1
        q = q_ref[pl.ds(pl.multiple_of(r * GRID_W, GRID_W), GRID_W), :]
        k0 = pl.multiple_of(kr0 * GRID_W, GRID_W)
        kw = k_ref[pl.ds(k0, kr * GRID_W), :]
        vw = v_ref[pl.ds(k0, kr * GRID_W), :]
        s_w = lax.dot_general(q, kw, nt, preferred_element_type=f32) * scale + bias_ref[d0]
        s_c = lax.dot_general(q, kc, nt, preferred_element_type=f32) * scale
        m = jnp.maximum(jnp.max(s_w, axis=-1, keepdims=True), jnp.max(s_c, axis=-1, keepdims=True))
        p_w = jnp.exp(s_w - m)
        p_c = jnp.exp(s_c - m)
        denom = jnp.sum(p_w, axis=-1, keepdims=True) + jnp.sum(p_c, axis=-1, keepdims=True)
        o = (jnp.dot(p_w.astype(bf16), vw, preferred_element_type=f32)
             + jnp.dot(p_c.astype(bf16), vc, preferred_element_type=f32))
        o_ref[pl.ds(pl.multiple_of(r * GRID_W, GRID_W), GRID_W), :] = (o / denom).astype(o_ref.dtype)
        return carry

    lax.fori_loop(0, rows, body, 0)


def _natten(px_att, pc_att, bias_tab):
    b, _, l, _ = px_att.shape
    lc = pc_att.shape[2]
    rows = l // GRID_W
    kr = min(WIN_R, rows)
    n_d0 = bias_tab.shape[1]
    blk = lambda off: pl.BlockSpec((None, None, l, HEAD_DIM), lambda h, bi: (bi, off + h, 0, 0))
    cblk = lambda off: pl.BlockSpec((None, None, lc, HEAD_DIM), lambda h, bi: (bi, off + h, 0, 0))
    return pl.pallas_call(
        functools.partial(_natten_kernel, rows=rows, kr=kr),
        out_shape=jax.ShapeDtypeStruct((b, l, HEADS * HEAD_DIM), bf16),
        grid=(HEADS, b),
        in_specs=[blk(0), blk(HEADS), blk(2 * HEADS), cblk(0), cblk(HEADS),
                  pl.BlockSpec((None, n_d0, GRID_W, kr * GRID_W), lambda h, bi: (h, 0, 0, 0))],
        out_specs=pl.BlockSpec((None, l, HEAD_DIM), lambda h, bi: (bi, 0, h)),
        compiler_params=_cparams(("parallel", "parallel")),
        name="natten",
    )(px_att, px_att, px_att, pc_att, pc_att, bias_tab)


def _hgrn_consts():
    t = np.arange(CHUNK)
    bt, bs = t[:, None] // SUB, t[None, :] // SUB
    tri, masks = [], []
    for sgn in (1, -1):
        before = (t[None, :] <= t[:, None]) if sgn == 1 else (t[None, :] >= t[:, None])
        tri.append(before.astype(np.float32))
        dist = (bt - bs) * sgn
        masks.append(np.stack([dist == 1, dist == 2, dist == 3, (dist == 0) & before]).astype(np.float32))
    return np.stack(tri), np.stack(masks)


def _rope_tables(l):
    t = jnp.arange(l)
    n_freq = HEAD_DIM // 4
    inv_freq = ROPE_THETA ** (-jnp.arange(n_freq, dtype=f32) / n_freq)
    ang_row = (t // GRID_W).astype(f32)[:, None] * inv_freq
    ang_col = (t % GRID_W).astype(f32)[:, None] * inv_freq
    cr, sr, cc, sc = jnp.cos(ang_row), jnp.sin(ang_row), jnp.cos(ang_col), jnp.sin(ang_col)
    z = jnp.zeros_like(sr)
    cos_t = jnp.concatenate([cr, cr, cc, cc], axis=-1)
    sin_up = jnp.concatenate([-sr, z, -sc, z], axis=-1)
    sin_dn = jnp.concatenate([z, sr, z, sc], axis=-1)
    return cos_t, sin_up, sin_dn


def _split3_dot(tri_bf, g):
    g1 = g.astype(bf16)
    r1 = g - g1.astype(f32)
    g2 = r1.astype(bf16)
    g3 = (r1 - g2.astype(f32)).astype(bf16)
    dot = lambda a: jnp.dot(tri_bf, a, preferred_element_type=f32)
    return dot(g1) + dot(g2) + dot(g3)


def _gate(f_raw, lb):
    log_f = jnp.log(lb + (1.0 - lb) * jax.nn.sigmoid(f_raw))
    key = (1.0 - lb) * jax.nn.sigmoid(-f_raw)
    return log_f, key


def _chunk_refs(cum, backward):
    nb = CHUNK // SUB
    if backward:
        ends = [cum[i * SUB:i * SUB + 1, :] for i in range(nb)]
        order = list(range(nb - 1, -1, -1))
    else:
        ends = [cum[i * SUB + SUB - 1:i * SUB + SUB, :] for i in range(nb)]
        order = list(range(nb))
    zero = jnp.zeros_like(ends[0])
    b_rows, g2_rows, g3_rows = [None] * nb, [None] * nb, [None] * nb
    for pos, i in enumerate(order):
        b_i = zero if pos == 0 else ends[order[pos - 1]]
        b_rows[i] = b_i
        g2_rows[i] = b_i - ends[order[pos - 2]] if pos >= 2 else zero
        g3_rows[i] = b_i - ends[order[pos - 3]] if pos >= 3 else zero
    expand = lambda rws: jnp.concatenate([jnp.broadcast_to(r, (SUB, HEAD_DIM)) for r in rws], axis=0)
    total = ends[order[-1]]
    return expand(b_rows), expand(ends), expand(g2_rows), expand(g3_rows), total


def _rope(x, cos_t, sin_up, sin_dn):
    return x * cos_t + pltpu.roll(x, 96, 1) * sin_up + pltpu.roll(x, 32, 1) * sin_dn


def _scan_step(st, q, f_raw, v, lb, rope, tri_bf, masks, backward):
    g, key = _gate(f_raw, lb)
    qr = _rope(q, *rope)
    kr = _rope(key, *rope)
    cum = _split3_dot(tri_bf, g)
    b, e, gap2, gap3, total = _chunk_refs(cum, backward)
    q_t = qr * jnp.exp(cum - b)
    k_hat = kr * jnp.exp(e - cum)
    k_til = kr * jnp.exp(b - cum)
    lhs = jnp.concatenate([q_t, q_t * jnp.exp(gap2), q_t * jnp.exp(gap3)], axis=0).astype(bf16)
    rhs = jnp.concatenate([k_hat, k_til], axis=0).astype(bf16)
    p = lax.dot_general(lhs, rhs, (((1,), (1,)), ((), ())), preferred_element_type=f32)
    att = (jnp.where(masks[0] > 0, p[0:CHUNK, 0:CHUNK], 0.0)
           + jnp.where(masks[1] > 0, p[CHUNK:2 * CHUNK, 0:CHUNK], 0.0)
           + jnp.where(masks[2] > 0, p[2 * CHUNK:3 * CHUNK, 0:CHUNK], 0.0)
           + jnp.where(masks[3] > 0, p[0:CHUNK, CHUNK:2 * CHUNK], 0.0))
    v_bf = v.astype(bf16)
    o = jnp.dot(att.astype(bf16), v_bf, preferred_element_type=f32)
    q_in = (q_t * jnp.exp(b)).astype(bf16)
    o = o + lax.dot_general(q_in, st.astype(bf16), (((1,), (1,)), ((), ())), preferred_element_type=f32)
    k_dec = (k_hat * jnp.exp(total - e)).astype(bf16)
    upd = lax.dot_general(v_bf, k_dec, (((0,), (0,)), ((), ())), preferred_element_type=f32)
    return st * jnp.exp(total) + upd, o


def _hgrn_kernel(q_ref, ff_ref, fb_ref, v_ref, gate_ref, cff_ref, cfb_ref, cv_ref, lb_ref, gain_ref,
                 cos_ref, sup_ref, sdn_ref, tri_ref, mask_ref, o_ref, of_scr, ob_scr, *, n_chunks, n_cchunks):
    lb_f = lb_ref[0]
    lb_b = lb_ref[1]
    tri_f = tri_ref[0].astype(bf16)
    tri_b = tri_ref[1].astype(bf16)
    zero = jnp.zeros((HEAD_DIM, HEAD_DIM), f32)

    def ctx_body(c, carry):
        st_f, st_b = carry
        rf = pl.ds(pl.multiple_of(c * CHUNK, CHUNK), CHUNK)
        rb = pl.ds(pl.multiple_of((n_cchunks - 1 - c) * CHUNK, CHUNK), CHUNK)
        st_f = _state_step(st_f, cv_ref[rf, :], cff_ref[rf, :], lb_f, tri_f, False)
        st_b = _state_step(st_b, cv_ref[rb, :], cfb_ref[rb, :], lb_b, tri_b, True)
        return st_f, st_b

    st_f, st_b = lax.fori_loop(0, n_cchunks, ctx_body, (zero, zero))

    def body(c, carry):
        st_f, st_b = carry
        rf = pl.ds(pl.multiple_of(c * CHUNK, CHUNK), CHUNK)
        rb = pl.ds(pl.multiple_of((n_chunks - 1 - c) * CHUNK, CHUNK), CHUNK)
        rope_f = (cos_ref[rf, :], sup_ref[rf, :], sdn_ref[rf, :])
        rope_b = (cos_ref[rb, :], sup_ref[rb, :], sdn_ref[rb, :])
        st_f, o_f = _scan_step(st_f, q_ref[rf, :], ff_ref[rf, :], v_ref[rf, :], lb_f, rope_f, tri_f,
                               [mask_ref[0, i] for i in range(4)], False)
        st_b, o_b = _scan_step(st_b, q_ref[rb, :], fb_ref[rb, :], v_ref[rb, :], lb_b, rope_b, tri_b,
                               [mask_ref[1, i] for i in range(4)], True)
        of_scr[rf, :] = o_f
        ob_scr[rb, :] = o_b
        return st_f, st_b

    lax.fori_loop(0, n_chunks, body, (st_f, st_b))

    o = of_scr[...] + ob_scr[...]
    y = o * lax.rsqrt(jnp.mean(o * o, axis=-1, keepdims=True) + EPS) * gain_ref[...]
    gate = gate_ref[...]
    o_ref[...] = (y * (gate * jax.nn.sigmoid(gate))).astype(o_ref.dtype)


def _state_step(st, v, f_raw, lb, tri_bf, backward):
    g, key = _gate(f_raw, lb)
    cum = _split3_dot(tri_bf, g)
    total = cum[0:1, :] if backward else cum[CHUNK - 1:CHUNK, :]
    k_dec = (key * jnp.exp(total - cum)).astype(bf16)
    upd = lax.dot_general(v.astype(bf16), k_dec, (((0,), (0,)), ((), ())), preferred_element_type=f32)
    return st * jnp.exp(total) + upd


def _hgrn(px_hg, pc_hg, lb2, hg_gain):
    b, _, l, _ = px_hg.shape
    lc = pc_hg.shape[2]
    cos_t, sin_up, sin_dn = _rope_tables(l)
    tri, masks = _hgrn_consts()
    blk = lambda off: pl.BlockSpec((None, None, l, HEAD_DIM), lambda h, bi: (bi, off + h, 0, 0))
    cblk = lambda off: pl.BlockSpec((None, None, lc, HEAD_DIM), lambda h, bi: (bi, off + h, 0, 0))
    full = lambda shp: pl.BlockSpec(shp, lambda h, bi: (0,) * len(shp))
    return pl.pallas_call(
        functools.partial(_hgrn_kernel, n_chunks=l // CHUNK, n_cchunks=lc // CHUNK),
        out_shape=jax.ShapeDtypeStruct((b, l, HEADS * HEAD_DIM), bf16),
        grid=(HEADS, b),
        in_specs=[blk(0), blk(HEADS), blk(2 * HEADS), blk(3 * HEADS), blk(4 * HEADS),
                  cblk(0), cblk(HEADS), cblk(2 * HEADS),
                  pl.BlockSpec((2, None, 1, HEAD_DIM), lambda h, bi: (0, h, 0, 0)),
                  full((1, HEAD_DIM)),
                  full((l, HEAD_DIM)), full((l, HEAD_DIM)), full((l, HEAD_DIM)),
                  full((2, CHUNK, CHUNK)), full((2, 4, CHUNK, CHUNK))],
        out_specs=pl.BlockSpec((None, l, HEAD_DIM), lambda h, bi: (bi, 0, h)),
        scratch_shapes=[pltpu.VMEM((l, HEAD_DIM), f32), pltpu.VMEM((l, HEAD_DIM), f32)],
        compiler_params=_cparams(("parallel", "parallel")),
        name="hgrn",
    )(px_hg, px_hg, px_hg, px_hg, px_hg, pc_hg, pc_hg, pc_hg,
      lb2.reshape(2, HEADS, 1, HEAD_DIM), hg_gain.reshape(1, HEAD_DIM),
      cos_t, sin_up, sin_dn, jnp.asarray(tri), jnp.asarray(masks))


def _pack_halves(h):
    k = h.shape[-1] // 2
    lo = lax.bitcast_convert_type(h[:, :k].astype(bf16).astype(f32), u32)
    hi = lax.bitcast_convert_type(h[:, k:].astype(bf16).astype(f32), u32)
    return (lo >> 16) | (hi & u32(HI_MASK))


def _unpack_halves(u):
    lo = lax.bitcast_convert_type(u << 16, f32)
    hi = lax.bitcast_convert_type(u & u32(HI_MASK), f32)
    return lo, hi


def _outproj_kernel(ana_ref, ahg_ref, x_ref, w0_ref, w1_ref, g1_ref, sh_ref, sc_ref, gain_ref, wr_ref, br_ref,
                    x1_ref, hp_ref, route_ref):
    y = (jnp.dot(ana_ref[...], w0_ref[...], preferred_element_type=f32)
         + jnp.dot(ahg_ref[...], w1_ref[...], preferred_element_type=f32))
    x1 = x_ref[...] + g1_ref[...] * y
    x1_ref[...] = x1
    ms = jnp.mean(x1 * x1, axis=-1, keepdims=True)
    h = x1 * lax.rsqrt(ms + EPS) * gain_ref[...]
    h = h * (1.0 + sc_ref[...]) + sh_ref[...]
    hp_ref[...] = _pack_halves(h)
    logits = jnp.dot(h, wr_ref[...], precision=_HIGHEST, preferred_element_type=f32) + br_ref[...]
    lane = lax.broadcasted_iota(jnp.int32, logits.shape, 1).astype(f32)
    cur = logits
    vals, idxs = [], []
    for _ in range(TOP_K):
        m = jnp.max(cur, axis=-1, keepdims=True)
        i = jnp.min(jnp.where(cur == m, lane, float(logits.shape[-1])), axis=-1, keepdims=True)
        vals.append(m)
        idxs.append(i)
        cur = jnp.where(lane == i, -jnp.inf, cur)
    es = [jnp.exp(v - vals[0]) for v in vals]
    denom = es[0] + es[1] + es[2] + es[3]
    route = jnp.zeros(logits.shape, f32)
    for k in range(TOP_K):
        route = jnp.where(lane == k, idxs[k], route)
        route = jnp.where(lane == TOP_K + k, es[k] / denom, route)
    route_ref[...] = route


def _outproj(a_na, a_hg, x, w_out_bf, g1, sh2, sc2, gain, w_router, b_router, tm):
    b, l, d = x.shape
    hw = a_na.shape[-1]
    tm = min(tm, l)
    n_e = w_router.shape[-1]
    wr = jnp.zeros((d, 128), f32).at[:, :n_e].set(w_router)
    br = jnp.full((1, 128), NEG_BIG, f32).at[0, :n_e].set(b_router)
    row = lambda last: pl.BlockSpec((None, tm, last), lambda bi, mi: (bi, mi, 0))
    vec = pl.BlockSpec((None, 1, d), lambda bi, mi: (bi, 0, 0))
    return pl.pallas_call(
        _outproj_kernel,
        out_shape=(jax.ShapeDtypeStruct((b, l, d), f32),
                   jax.ShapeDtypeStruct((b, l, d // 2), u32),
                   jax.ShapeDtypeStruct((b, l, 128), f32)),
        grid=(b, l // tm),
        in_specs=[row(hw), row(hw), row(d),
                  pl.BlockSpec((hw, d), lambda bi, mi: (0, 0)),
                  pl.BlockSpec((hw, d), lambda bi, mi: (1, 0)),
                  vec, vec, vec,
                  pl.BlockSpec((1, d), lambda bi, mi: (0, 0)),
                  pl.BlockSpec((d, 128), lambda bi, mi: (0, 0)),
                  pl.BlockSpec((1, 128), lambda bi, mi: (0, 0))],
        out_specs=(row(d), row(d // 2), row(128)),
        compiler_params=_cparams(("parallel", "parallel"), 48),
        name="outproj",
    )(a_na, a_hg, x, w_out_bf, w_out_bf, g1, sh2, sc2, gain, wr, br)


DISPATCH_ROWS = 1024


def _dispatch_kernel(tok_ref, src_ref, dst_ref, sem):
    base = pl.program_id(0) * DISPATCH_ROWS

    def row_copy(i):
        return pltpu.make_async_copy(src_ref.at[pl.ds(tok_ref[i], 1), :], dst_ref.at[pl.ds(base + i, 1), :], sem)

    def issue(i, c):
        row_copy(i).start()
        return c

    def drain(i, c):
        row_copy(i).wait()
        return c

    lax.fori_loop(0, DISPATCH_ROWS, issue, 0)
    lax.fori_loop(0, DISPATCH_ROWS, drain, 0)


def _dispatch(slot_tok, hp):
    cap = slot_tok.shape[0]
    return pl.pallas_call(
        _dispatch_kernel,
        out_shape=jax.ShapeDtypeStruct((cap, hp.shape[1]), hp.dtype),
        grid=(cap // DISPATCH_ROWS,),
        in_specs=[pl.BlockSpec((DISPATCH_ROWS,), lambda i: (i,), memory_space=pltpu.SMEM),
                  pl.BlockSpec(memory_space=pl.ANY)],
        out_specs=pl.BlockSpec(memory_space=pl.ANY),
        scratch_shapes=[pltpu.SemaphoreType.DMA(())],
        compiler_params=_cparams(("arbitrary",)),
        name="dispatch",
    )(slot_tok, hp)


def _gemm1_kernel(be_ref, first_ref, nu_ref, xs_ref, wg_ref, wl_ref, bg_ref, bl_ref, act_ref, wg_bf, wl_bf):
    m = pl.program_id(1)

    @pl.when(m < nu_ref[0])
    def _():
        @pl.when(first_ref[m] == 1)
        def _():
            wg_bf[...] = wg_ref[...].astype(bf16)
            wl_bf[...] = wl_ref[...].astype(bf16)

        lo, hi = _unpack_halves(xs_ref[...])
        lo = lo.astype(bf16)
        hi = hi.astype(bf16)
        k = lo.shape[-1]
        gate = (jnp.dot(lo, wg_bf[:k, :], preferred_element_type=f32)
                + jnp.dot(hi, wg_bf[k:, :], preferred_element_type=f32) + bg_ref[...])
        lin = (jnp.dot(lo, wl_bf[:k, :], preferred_element_type=f32)
               + jnp.dot(hi, wl_bf[k:, :], preferred_element_type=f32) + bl_ref[...])
        x_glu = jnp.minimum(gate, SWIGLU_LIMIT)
        x_lin = jnp.clip(lin, -SWIGLU_LIMIT, SWIGLU_LIMIT)
        act_ref[...] = (x_glu * jax.nn.sigmoid(SWIGLU_ALPHA * x_glu) * (x_lin + 1.0)).astype(act_ref.dtype)

    @pl.when(m >= nu_ref[0])
    def _():
        act_ref[...] = jnp.zeros_like(act_ref)


def _gemm1(block_exp, first, n_used, xs, w_gu, b_gu, tn):
    cap, kp = xs.shape
    n_e, d, f2 = w_gu.shape
    ff = f2 // 2
    tn = min(tn, ff)
    nb = cap // MOE_TM
    nj = ff // tn
    live = lambda m, nu: jnp.minimum(m, nu[0] - 1)
    return pl.pallas_call(
        _gemm1_kernel,
        out_shape=jax.ShapeDtypeStruct((cap, ff), bf16),
        grid_spec=pltpu.PrefetchScalarGridSpec(
            num_scalar_prefetch=3,
            grid=(nj, nb),
            in_specs=[pl.BlockSpec((MOE_TM, kp), lambda j, m, be, fi, nu: (live(m, nu), 0)),
                      pl.BlockSpec((None, d, tn), lambda j, m, be, fi, nu: (be[m], 0, j)),
                      pl.BlockSpec((None, d, tn), lambda j, m, be, fi, nu: (be[m], 0, nj + j)),
                      pl.BlockSpec((None, 1, tn), lambda j, m, be, fi, nu: (be[m], 0, j)),
                      pl.BlockSpec((None, 1, tn), lambda j, m, be, fi, nu: (be[m], 0, nj + j))],
            out_specs=pl.BlockSpec((MOE_TM, tn), lambda j, m, be, fi, nu: (m, j)),
            scratch_shapes=[pltpu.VMEM((d, tn), bf16), pltpu.VMEM((d, tn), bf16)]),
        compiler_params=_cparams(("arbitrary", "arbitrary"), 56),
        name="gemm1",
    )(block_exp, first, n_used, xs, w_gu, w_gu, b_gu.reshape(n_e, 1, f2), b_gu.reshape(n_e, 1, f2))


def _gemm2_kernel(be_ref, first_ref, nu_ref, act_ref, w_ref, b_ref, y_ref, w_bf):
    m = pl.program_id(0)

    @pl.when(m < nu_ref[0])
    def _():
        @pl.when(first_ref[m] == 1)
        def _():
            w_bf[...] = w_ref[...].astype(bf16)

        y = jnp.dot(act_ref[...], w_bf[...], preferred_element_type=f32) + b_ref[...]
        y_ref[...] = _pack_halves(y)

    @pl.when(m >= nu_ref[0])
    def _():
        y_ref[...] = jnp.zeros_like(y_ref)


def _gemm2(block_exp, first, n_used, act, w_down, b_down):
    cap, ff = act.shape
    n_e, _, d = w_down.shape
    nb = cap // MOE_TM
    live = lambda m, nu: jnp.minimum(m, nu[0] - 1)
    return pl.pallas_call(
        _gemm2_kernel,
        out_shape=jax.ShapeDtypeStruct((cap, d // 2), u32),
        grid_spec=pltpu.PrefetchScalarGridSpec(
            num_scalar_prefetch=3,
            grid=(nb,),
            in_specs=[pl.BlockSpec((MOE_TM, ff), lambda m, be, fi, nu: (live(m, nu), 0)),
                      pl.BlockSpec((None, ff, d), lambda m, be, fi, nu: (be[m], 0, 0)),
                      pl.BlockSpec((None, 1, d), lambda m, be, fi, nu: (be[m], 0, 0))],
            out_specs=pl.BlockSpec((MOE_TM, d // 2), lambda m, be, fi, nu: (m, 0)),
            scratch_shapes=[pltpu.VMEM((ff, d), bf16)]),
        compiler_params=_cparams(("arbitrary",), 56),
        name="gemm2",
    )(block_exp, first, n_used, act, w_down, b_down.reshape(n_e, 1, d))


COMBINE_TM = 256


def _combine_kernel(dest_ref, yb_ref, route_ref, x1_ref, g2_ref, gain_ref, o_ref, buf, sem):
    n_rows = COMBINE_TM * TOP_K

    def row_copy(a):
        return pltpu.make_async_copy(yb_ref.at[pl.ds(dest_ref[a], 1), :],
                                     buf.at[a % TOP_K, pl.ds(a // TOP_K, 1), :], sem)

    def issue(a, c):
        row_copy(a).start()
        return c

    def drain(a, c):
        row_copy(a).wait()
        return c

    lax.fori_loop(0, n_rows, issue, 0)
    lax.fori_loop(0, n_rows, drain, 0)

    route = route_ref[...]
    acc = None
    for k in range(TOP_K):
        lo, hi = _unpack_halves(buf[k])
        term = route[:, TOP_K + k:TOP_K + k + 1] * jnp.concatenate([lo, hi], axis=-1)
        acc = term if acc is None else acc + term
    x2 = x1_ref[...] + g2_ref[...] * acc
    ms = jnp.mean(x2 * x2, axis=-1, keepdims=True)
    o_ref[...] = x2 * lax.rsqrt(ms + EPS) * gain_ref[...]


def _combine(dest, yb, route, x1, g2, gain):
    b, l, d = x1.shape
    tm = min(COMBINE_TM, l)
    assert tm == COMBINE_TM
    per_b = l // tm
    return pl.pallas_call(
        _combine_kernel,
        out_shape=jax.ShapeDtypeStruct((b, l, d), f32),
        grid=(b * per_b,),
        in_specs=[pl.BlockSpec((tm * TOP_K,), lambda i: (i,), memory_space=pltpu.SMEM),
                  pl.BlockSpec(memory_space=pl.ANY),
                  pl.BlockSpec((None, tm, 128), lambda i: (i // per_b, i % per_b, 0)),
                  pl.BlockSpec((None, tm, d), lambda i: (i // per_b, i % per_b, 0)),
                  pl.BlockSpec((None, 1, d), lambda i: (i // per_b, 0, 0)),
                  pl.BlockSpec((1, d), lambda i: (0, 0))],
        out_specs=pl.BlockSpec((None, tm, d), lambda i: (i // per_b, i % per_b, 0)),
        scratch_shapes=[pltpu.VMEM((TOP_K, tm, d // 2), u32), pltpu.SemaphoreType.DMA(())],
        compiler_params=_cparams(("arbitrary",), 40),
        name="combine",
    )(dest, yb, route, x1, g2, gain)


def _routing(top_idx, n_blocks):
    t = top_idx.shape[0]
    a = t * TOP_K
    e_flat = top_idx.reshape(a)
    onehot = (e_flat[:, None] == jnp.arange(N_EXPERTS, dtype=jnp.int32)[None, :]).astype(jnp.int32)
    csum = jnp.cumsum(onehot, axis=0)
    rank = jnp.take_along_axis(csum, e_flat[:, None], axis=1)[:, 0] - 1
    counts = csum[-1]
    padded = (counts + MOE_TM - 1) // MOE_TM * MOE_TM
    pad_end = jnp.cumsum(padded)
    pad_start = pad_end - padded
    dest = (pad_start[e_flat] + rank).astype(jnp.int32)
    cap = n_blocks * MOE_TM
    slot_tok = jnp.zeros((cap,), jnp.int32).at[dest].set(jnp.arange(a, dtype=jnp.int32) // TOP_K)
    block_exp = jnp.minimum(jnp.searchsorted(pad_end, jnp.arange(n_blocks, dtype=jnp.int32) * MOE_TM, side="right"),
                            N_EXPERTS - 1).astype(jnp.int32)
    first = jnp.concatenate([jnp.ones((1,), jnp.int32), (block_exp[1:] != block_exp[:-1]).astype(jnp.int32)])
    n_used = (pad_end[-1:] // MOE_TM).astype(jnp.int32)
    return dest, slot_tok, block_exp, first, n_used


def _moe(hp, route, x1, g2, norm_final, w_gu, b_gu, w_down, b_down):
    b, l, d = x1.shape
    t = b * l
    a = t * TOP_K
    slots = a + N_EXPERTS * (MOE_TM - 1)
    n_blocks = -(-slots // DISPATCH_ROWS) * (DISPATCH_ROWS // MOE_TM)
    top_idx = route[..., :TOP_K].astype(jnp.int32).reshape(t, TOP_K)
    dest, slot_tok, block_exp, first, n_used = _routing(top_idx, n_blocks)
    xs = _dispatch(slot_tok, hp.reshape(t, d // 2))
    act = _gemm1(block_exp, first, n_used, xs, w_gu, b_gu, 1024)
    yb = _gemm2(block_exp, first, n_used, act, w_down, b_down)
    return _combine(dest, yb, route, x1, g2, norm_final.reshape(1, d))


def kernel(x, c, ctx, c_ctx, w_ada, b_ada, norm_mix, norm_ffn, w_in, lb_table, hg_norm, rpb, w_out, w_router,
           b_router, w_gu, b_gu, w_down, b_down, norm_final):
    b, l, d = x.shape
    assert w_ada.shape[0] == 1, "single-layer block"
    rows = l // GRID_W

    c16 = jnp.zeros((16, d), f32).at[:b].set(c).at[b].set(c_ctx)
    mod = _ada(c16, w_ada[0], b_ada[0])
    sh1, sc1, g1, sh2, sc2, g2 = [mod[:b, i * d:(i + 1) * d].reshape(b, 1, d) for i in range(6)]
    csh = jnp.broadcast_to(mod[b, :d].reshape(1, 1, d), (b, 1, d))
    csc = jnp.broadcast_to(mod[b, d:2 * d].reshape(1, 1, d), (b, 1, d))

    w_in_bf = w_in[0].astype(bf16)
    gain_mix = norm_mix[0].reshape(1, d)
    px_att = _inproj(x, sh1, sc1, gain_mix, w_in_bf, P_QN, 3, bf16, 1024)
    px_hg = _inproj(x, sh1, sc1, gain_mix, w_in_bf, P_QH, 5, f32, 1024)
    pc_att = _inproj(ctx, csh, csc, gain_mix, w_in_bf, P_KN, 2, bf16, 256)
    pc_hg = _inproj(ctx, csh, csc, gain_mix, w_in_bf, P_FF, 3, f32, 256)

    o_na = _natten(px_att, pc_att, _natten_bias_table(rpb[0], rows))

    lower_bounds = jnp.cumsum(jax.nn.softmax(lb_table.astype(f32), axis=0), axis=0)
    o_hg = _hgrn(px_hg, pc_hg, lower_bounds[0].reshape(2, HEADS * HEAD_DIM), hg_norm[0])

    x1, hp, route = _outproj(o_na, o_hg, x, w_out[0].astype(bf16), g1, sh2, sc2, norm_ffn[0].reshape(1, d),
                             w_router[0], b_router[0], 256)
    return _moe(hp, route, x1, g2, norm_final, w_gu[0], b_gu[0], w_down[0], b_down[0])
```

```python
import functools

import numpy as np
import jax
import jax.numpy as jnp
from jax import lax
from jax.experimental import pallas as pl
from jax.experimental.pallas import tpu as pltpu

f32 = jnp.float32
bf16 = jnp.bfloat16
u32 = jnp.uint32

GRID_W = 64
HEADS = 8
HEAD_DIM = 128
WIN_R = 8
WIN_C = 16
ROPE_THETA = 10000.0
N_EXPERTS = 32
TOP_K = 4
SWIGLU_LIMIT = 7.0
SWIGLU_ALPHA = 1.702
EPS = 1e-6
P_QN, P_KN, P_VN, P_QH, P_FF, P_FB, P_IH, P_GH = range(8)

CHUNK = 64
SUB = 16
CHUNK_UNROLL = 4
ROW_UNROLL = 4
MOE_TM = 256
NEG_BIG = -1e30
MEMBER_LANE = 32
HI_MASK = 0xFFFF0000

_HIGHEST = lax.Precision.HIGHEST


def _cparams(sem, vmem_mb=None):
    kw = dict(dimension_semantics=sem)
    if vmem_mb is not None:
        kw["vmem_limit_bytes"] = vmem_mb * 1024 * 1024
    return pltpu.CompilerParams(**kw)


def _ada_kernel(c_ref, w_ref, b_ref, o_ref):
    c = c_ref[...]
    cond = c * jax.nn.sigmoid(c)
    o_ref[...] = jnp.dot(cond, w_ref[...], precision=_HIGHEST, preferred_element_type=f32) + b_ref[...]


def _ada(c16, w_ada, b_ada):
    d, n = w_ada.shape
    tn = min(1024, n)
    return pl.pallas_call(
        _ada_kernel,
        out_shape=jax.ShapeDtypeStruct((c16.shape[0], n), f32),
        grid=(n // tn,),
        in_specs=[pl.BlockSpec((c16.shape[0], d), lambda j: (0, 0)),
                  pl.BlockSpec((d, tn), lambda j: (0, j)),
                  pl.BlockSpec((1, tn), lambda j: (0, j))],
        out_specs=pl.BlockSpec((c16.shape[0], tn), lambda j: (0, j)),
        compiler_params=_cparams(("parallel",), 40),
        name="ada",
    )(c16, w_ada, b_ada.reshape(1, n))


def _inproj_kernel(x_ref, shift_ref, scale_ref, gain_ref, w_ref, o_ref, h_scr):
    @pl.when(pl.program_id(2) == 0)
    def _():
        x = x_ref[...]
        ms = jnp.mean(x * x, axis=-1, keepdims=True)
        y = x * lax.rsqrt(ms + EPS) * gain_ref[...]
        h_scr[...] = (y * (1.0 + scale_ref[...]) + shift_ref[...]).astype(bf16)

    acc = jnp.dot(h_scr[...], w_ref[...], preferred_element_type=f32)
    for hh in range(HEADS):
        o_ref[hh] = acc[:, hh * HEAD_DIM:(hh + 1) * HEAD_DIM].astype(o_ref.dtype)


def _inproj(x, shift, scale, gain, w_bf, part_lo, n_parts, out_dtype, tm):
    b, l, d = x.shape
    pw = HEADS * HEAD_DIM
    tm = min(tm, l)
    return pl.pallas_call(
        _inproj_kernel,
        out_shape=jax.ShapeDtypeStruct((b, n_parts * HEADS, l, HEAD_DIM), out_dtype),
        grid=(b, l // tm, n_parts),
        in_specs=[pl.BlockSpec((None, tm, d), lambda bi, mi, ni: (bi, mi, 0)),
                  pl.BlockSpec((None, 1, d), lambda bi, mi, ni: (bi, 0, 0)),
                  pl.BlockSpec((None, 1, d), lambda bi, mi, ni: (bi, 0, 0)),
                  pl.BlockSpec((1, d), lambda bi, mi, ni: (0, 0)),
                  pl.BlockSpec((d, pw), lambda bi, mi, ni: (0, part_lo + ni))],
        out_specs=pl.BlockSpec((None, HEADS, tm, HEAD_DIM), lambda bi, mi, ni: (bi, ni, mi, 0)),
        scratch_shapes=[pltpu.VMEM((tm, d), bf16)],
        compiler_params=_cparams(("parallel", "parallel", "arbitrary"), 48),
        name="inproj",
    )(x, shift, scale, gain, w_bf)


def _natten_bias_table(rpb, rows):
    kr = min(WIN_R, rows)
    q = np.arange(GRID_W)
    col_start = np.clip(q - WIN_C // 2, 0, GRID_W - WIN_C)
    kc = np.arange(GRID_W)
    in_win = (kc[None, :] >= col_start[:, None]) & (kc[None, :] < col_start[:, None] + WIN_C)
    d_col = np.clip(kc[None, :] - q[:, None] + WIN_C - 1, 0, 2 * WIN_C - 2)
    n_d0 = 2 * WIN_R - 1 - (kr - 1)
    d_row = np.arange(n_d0)[:, None] + np.arange(kr)[None, :]
    t = rpb.astype(f32)[:, d_row[:, :, None, None], d_col[None, None, :, :]]
    t = jnp.where(in_win[None, None, None], t, -jnp.inf)
    t = jnp.transpose(t, (0, 1, 3, 2, 4))
    return t.reshape(rpb.shape[0], n_d0, GRID_W, kr * GRID_W)


def _natten_kernel(q_ref, k_ref, v_ref, kc_ref, vc_ref, bias_ref, o_ref, *, rows, kr):
    scale = HEAD_DIM ** -0.5
    kc = kc_ref[...]
    vc = vc_ref[...]
    nt = (((1,), (1,)), ((), ()))

    def body(i, carry):
        q0 = pl.multiple_of(i * (ROW_UNROLL * GRID_W), ROW_UNROLL * GRID_W)
        q_all = q_ref[pl.ds(q0, ROW_UNROLL * GRID_W), :]
        s_c = lax.dot_general(q_all, kc, nt, preferred_element_type=f32) * scale
        m_c = jnp.max(s_c, axis=-1, keepdims=True)
        s_w, vws = [], []
        for u in range(ROW_UNROLL):
            r = i * ROW_UNROLL + u
            kr0 = jnp.clip(r - kr // 2, 0, rows - kr)
            k0 = pl.multiple_of(kr0 * GRID_W, GRID_W)
            kw = k_ref[pl.ds(k0, kr * GRID_W), :]
            vws.append(v_ref[pl.ds(k0, kr * GRID_W), :])
            q = q_all[u * GRID_W:(u + 1) * GRID_W, :]
            s_w.append(lax.dot_general(q, kw, nt, preferred_element_type=f32) * scale
                       + bias_ref[kr0 - r + WIN_R - 1])
        m = jnp.maximum(jnp.concatenate([jnp.max(s, axis=-1, keepdims=True) for s in s_w], axis=0), m_c)
        p_c = jnp.exp(s_c - m)
        p_w = [jnp.exp(s - m[u * GRID_W:(u + 1) * GRID_W, :]) for u, s in enumerate(s_w)]
        denom = (jnp.concatenate([jnp.sum(p, axis=-1, keepdims=True) for p in p_w], axis=0)
                 + jnp.sum(p_c, axis=-1, keepdims=True))
        o_c = jnp.dot(p_c.astype(bf16), vc, preferred_element_type=f32)
        o_w = jnp.concatenate([jnp.dot(p.astype(bf16), vw, preferred_element_type=f32) for p, vw in zip(p_w, vws)],
                              axis=0)
        o_ref[pl.ds(q0, ROW_UNROLL * GRID_W), :] = ((o_w + o_c) / denom).astype(o_ref.dtype)
        return carry

    lax.fori_loop(0, rows // ROW_UNROLL, body, 0)


def _natten(px_att, pc_att, bias_tab):
    b, _, l, _ = px_att.shape
    lc = pc_att.shape[2]
    rows = l // GRID_W
    kr = min(WIN_R, rows)
    n_d0 = bias_tab.shape[1]
    blk = lambda off: pl.BlockSpec((None, None, l, HEAD_DIM), lambda h, bi: (bi, off + h, 0, 0))
    cblk = lambda off: pl.BlockSpec((None, None, lc, HEAD_DIM), lambda h, bi: (bi, off + h, 0, 0))
    return pl.pallas_call(
        functools.partial(_natten_kernel, rows=rows, kr=kr),
        out_shape=jax.ShapeDtypeStruct((b, l, HEADS * HEAD_DIM), bf16),
        grid=(HEADS, b),
        in_specs=[blk(0), blk(HEADS), blk(2 * HEADS), cblk(0), cblk(HEADS),
                  pl.BlockSpec((None, n_d0, GRID_W, kr * GRID_W), lambda h, bi: (h, 0, 0, 0))],
        out_specs=pl.BlockSpec((None, l, HEAD_DIM), lambda h, bi: (bi, 0, h)),
        compiler_params=_cparams(("parallel", "parallel")),
        name="natten",
    )(px_att, px_att, px_att, pc_att, pc_att, bias_tab)


def _hgrn_consts():
    t = np.arange(CHUNK)
    bt, bs = t[:, None] // SUB, t[None, :] // SUB
    tri, masks = [], []
    for sgn in (1, -1):
        before = (t[None, :] <= t[:, None]) if sgn == 1 else (t[None, :] >= t[:, None])
        tri.append(before.astype(np.float32))
        dist = (bt - bs) * sgn
        masks.append(np.stack([dist == 1, dist == 2, dist == 3, (dist == 0) & before]).astype(np.float32))
    return np.stack(tri), np.stack(masks)


def _rope_tables(l):
    t = jnp.arange(l)
    n_freq = HEAD_DIM // 4
    inv_freq = ROPE_THETA ** (-jnp.arange(n_freq, dtype=f32) / n_freq)
    ang_row = (t // GRID_W).astype(f32)[:, None] * inv_freq
    ang_col = (t % GRID_W).astype(f32)[:, None] * inv_freq
    cr, sr, cc, sc = jnp.cos(ang_row), jnp.sin(ang_row), jnp.cos(ang_col), jnp.sin(ang_col)
    z = jnp.zeros_like(sr)
    cos_t = jnp.concatenate([cr, cr, cc, cc], axis=-1)
    sin_up = jnp.concatenate([-sr, z, -sc, z], axis=-1)
    sin_dn = jnp.concatenate([z, sr, z, sc], axis=-1)
    return cos_t, sin_up, sin_dn


def _split3_dot(tri_bf, g):
    g1 = g.astype(bf16)
    r1 = g - g1.astype(f32)
    g2 = r1.astype(bf16)
    g3 = (r1 - g2.astype(f32)).astype(bf16)
    dot = lambda a: jnp.dot(tri_bf, a, preferred_element_type=f32)
    return dot(g1) + dot(g2) + dot(g3)


def _gate(f_raw, lb):
    log_f = jnp.log(lb + (1.0 - lb) * jax.nn.sigmoid(f_raw))
    key = (1.0 - lb) * jax.nn.sigmoid(-f_raw)
    return log_f, key


def _chunk_refs(cum, backward):
    nb = CHUNK // SUB
    if backward:
        ends = [cum[i * SUB:i * SUB + 1, :] for i in range(nb)]
        order = list(range(nb - 1, -1, -1))
    else:
        ends = [cum[i * SUB + SUB - 1:i * SUB + SUB, :] for i in range(nb)]
        order = list(range(nb))
    zero = jnp.zeros_like(ends[0])
    b_rows, g2_rows, g3_rows = [None] * nb, [None] * nb, [None] * nb
    for pos, i in enumerate(order):
        b_i = zero if pos == 0 else ends[order[pos - 1]]
        b_rows[i] = b_i
        g2_rows[i] = b_i - ends[order[pos - 2]] if pos >= 2 else zero
        g3_rows[i] = b_i - ends[order[pos - 3]] if pos >= 3 else zero
    expand = lambda rws: jnp.concatenate([jnp.broadcast_to(r, (SUB, HEAD_DIM)) for r in rws], axis=0)
    total = ends[order[-1]]
    return expand(b_rows), expand(ends), expand(g2_rows), expand(g3_rows), total


def _rope(x, cos_t, sin_up, sin_dn):
    return x * cos_t + pltpu.roll(x, 96, 1) * sin_up + pltpu.roll(x, 32, 1) * sin_dn


def _scan_group(states, items, tris, masks):
    nt = (((1,), (1,)), ((), ()))
    tn = (((0,), (0,)), ((), ()))
    pre = []
    for d, q, f_raw, v, lb, rope in items:
        g, key = _gate(f_raw, lb)
        pre.append((g, _rope(q, *rope), _rope(key, *rope), v.astype(bf16)))
    cums = [_split3_dot(tris[it[0]], p[0]) for it, p in zip(items, pre)]
    ops = []
    for it, (g, qr, kr, v_bf), cum in zip(items, pre, cums):
        b, e, gap2, gap3, total = _chunk_refs(cum, it[0] == 1)
        q_t = qr * jnp.exp(cum - b)
        k_hat = kr * jnp.exp(e - cum)
        k_til = kr * jnp.exp(b - cum)
        lhs = jnp.concatenate([q_t, q_t * jnp.exp(gap2), q_t * jnp.exp(gap3)], axis=0).astype(bf16)
        rhs = jnp.concatenate([k_hat, k_til], axis=0).astype(bf16)
        q_in = (q_t * jnp.exp(b)).astype(bf16)
        k_dec = (k_hat * jnp.exp(total - e)).astype(bf16)
        ops.append((lhs, rhs, q_in, k_dec, jnp.exp(total)))
    scores = [lax.dot_general(o[0], o[1], nt, preferred_element_type=f32) for o in ops]
    upds = [lax.dot_general(p[3], o[3], tn, preferred_element_type=f32) for p, o in zip(pre, ops)]
    intra = []
    for it, p, (g, qr, kr, v_bf) in zip(items, scores, pre):
        m = masks[it[0]]
        att = (jnp.where(m[0] > 0, p[0:CHUNK, 0:CHUNK], 0.0)
               + jnp.where(m[1] > 0, p[CHUNK:2 * CHUNK, 0:CHUNK], 0.0)
               + jnp.where(m[2] > 0, p[2 * CHUNK:3 * CHUNK, 0:CHUNK], 0.0)
               + jnp.where(m[3] > 0, p[0:CHUNK, CHUNK:2 * CHUNK], 0.0))
        intra.append(jnp.dot(att.astype(bf16), v_bf, preferred_element_type=f32))
    states = list(states)
    outs = []
    for it, o, upd, o_in in zip(items, ops, upds, intra):
        st = states[it[0]]
        outs.append(o_in + lax.dot_general(o[2], st.astype(bf16), nt, preferred_element_type=f32))
        states[it[0]] = st * o[4] + upd
    return states, outs


def _hgrn_kernel(q_ref, ff_ref, fb_ref, v_ref, gate_ref, cff_ref, cfb_ref, cv_ref, lb_ref, gain_ref,
                 cos_ref, sup_ref, sdn_ref, tri_ref, mask_ref, o_ref, of_scr, ob_scr, *, n_chunks, n_cchunks):
    lb_f = lb_ref[0]
    lb_b = lb_ref[1]
    tri_f = tri_ref[0].astype(bf16)
    tri_b = tri_ref[1].astype(bf16)
    zero = jnp.zeros((HEAD_DIM, HEAD_DIM), f32)

    def ctx_body(c, carry):
        st_f, st_b = carry
        rf = pl.ds(pl.multiple_of(c * CHUNK, CHUNK), CHUNK)
        rb = pl.ds(pl.multiple_of((n_cchunks - 1 - c) * CHUNK, CHUNK), CHUNK)
        st_f = _state_step(st_f, cv_ref[rf, :], cff_ref[rf, :], lb_f, tri_f, False)
        st_b = _state_step(st_b, cv_ref[rb, :], cfb_ref[rb, :], lb_b, tri_b, True)
        return st_f, st_b

    st_f, st_b = lax.fori_loop(0, n_cchunks, ctx_body, (zero, zero))

    def body(c, carry):
        masks = [[mask_ref[d, i] for i in range(4)] for d in range(2)]
        items, rows = [], []
        for u in range(CHUNK_UNROLL):
            cf = c * CHUNK_UNROLL + u
            rf = pl.ds(pl.multiple_of(cf * CHUNK, CHUNK), CHUNK)
            rb = pl.ds(pl.multiple_of((n_chunks - 1 - cf) * CHUNK, CHUNK), CHUNK)
            items.append((0, q_ref[rf, :], ff_ref[rf, :], v_ref[rf, :], lb_f,
                          (cos_ref[rf, :], sup_ref[rf, :], sdn_ref[rf, :])))
            items.append((1, q_ref[rb, :], fb_ref[rb, :], v_ref[rb, :], lb_b,
                          (cos_ref[rb, :], sup_ref[rb, :], sdn_ref[rb, :])))
            rows += [(of_scr, rf), (ob_scr, rb)]
        states, outs = _scan_group(carry, items, (tri_f, tri_b), masks)
        for (scr, rws), o in zip(rows, outs):
            scr[rws, :] = o
        return tuple(states)

    lax.fori_loop(0, n_chunks // CHUNK_UNROLL, body, (st_f, st_b))

    o = of_scr[...] + ob_scr[...]
    y = o * lax.rsqrt(jnp.mean(o * o, axis=-1, keepdims=True) + EPS) * gain_ref[...]
    gate = gate_ref[...]
    o_ref[...] = (y * (gate * jax.nn.sigmoid(gate))).astype(o_ref.dtype)


def _state_step(st, v, f_raw, lb, tri_bf, backward):
    g, key = _gate(f_raw, lb)
    cum = _split3_dot(tri_bf, g)
    total = cum[0:1, :] if backward else cum[CHUNK - 1:CHUNK, :]
    k_dec = (key * jnp.exp(total - cum)).astype(bf16)
    upd = lax.dot_general(v.astype(bf16), k_dec, (((0,), (0,)), ((), ())), preferred_element_type=f32)
    return st * jnp.exp(total) + upd


def _hgrn(px_hg, pc_hg, lb2, hg_gain):
    b, _, l, _ = px_hg.shape
    lc = pc_hg.shape[2]
    cos_t, sin_up, sin_dn = _rope_tables(l)
    tri, masks = _hgrn_consts()
    blk = lambda off: pl.BlockSpec((None, None, l, HEAD_DIM), lambda h, bi: (bi, off + h, 0, 0))
    cblk = lambda off: pl.BlockSpec((None, None, lc, HEAD_DIM), lambda h, bi: (bi, off + h, 0, 0))
    full = lambda shp: pl.BlockSpec(shp, lambda h, bi: (0,) * len(shp))
    return pl.pallas_call(
        functools.partial(_hgrn_kernel, n_chunks=l // CHUNK, n_cchunks=lc // CHUNK),
        out_shape=jax.ShapeDtypeStruct((b, l, HEADS * HEAD_DIM), bf16),
        grid=(HEADS, b),
        in_specs=[blk(0), blk(HEADS), blk(2 * HEADS), blk(3 * HEADS), blk(4 * HEADS),
                  cblk(0), cblk(HEADS), cblk(2 * HEADS),
                  pl.BlockSpec((2, None, 1, HEAD_DIM), lambda h, bi: (0, h, 0, 0)),
                  full((1, HEAD_DIM)),
                  full((l, HEAD_DIM)), full((l, HEAD_DIM)), full((l, HEAD_DIM)),
                  full((2, CHUNK, CHUNK)), full((2, 4, CHUNK, CHUNK))],
        out_specs=pl.BlockSpec((None, l, HEAD_DIM), lambda h, bi: (bi, 0, h)),
        scratch_shapes=[pltpu.VMEM((l, HEAD_DIM), f32), pltpu.VMEM((l, HEAD_DIM), f32)],
        compiler_params=_cparams(("parallel", "parallel")),
        name="hgrn",
    )(px_hg, px_hg, px_hg, px_hg, px_hg, pc_hg, pc_hg, pc_hg,
      lb2.reshape(2, HEADS, 1, HEAD_DIM), hg_gain.reshape(1, HEAD_DIM),
      cos_t, sin_up, sin_dn, jnp.asarray(tri), jnp.asarray(masks))


def _pack_halves(h):
    k = h.shape[-1] // 2
    lo = lax.bitcast_convert_type(h[:, :k].astype(bf16).astype(f32), u32)
    hi = lax.bitcast_convert_type(h[:, k:].astype(bf16).astype(f32), u32)
    return (lo >> 16) | (hi & u32(HI_MASK))


def _unpack_halves(u):
    lo = lax.bitcast_convert_type(u << 16, f32)
    hi = lax.bitcast_convert_type(u & u32(HI_MASK), f32)
    return lo, hi


def _outproj_kernel(ana_ref, ahg_ref, x_ref, w0_ref, w1_ref, g1_ref, sh_ref, sc_ref, gain_ref, wr_ref, br_ref,
                    x1_ref, hp_ref, route_ref):
    y = (jnp.dot(ana_ref[...], w0_ref[...], preferred_element_type=f32)
         + jnp.dot(ahg_ref[...], w1_ref[...], preferred_element_type=f32))
    x1 = x_ref[...] + g1_ref[...] * y
    x1_ref[...] = x1
    ms = jnp.mean(x1 * x1, axis=-1, keepdims=True)
    h = x1 * lax.rsqrt(ms + EPS) * gain_ref[...]
    h = h * (1.0 + sc_ref[...]) + sh_ref[...]
    hp_ref[...] = _pack_halves(h)
    logits = jnp.dot(h, wr_ref[...], precision=_HIGHEST, preferred_element_type=f32) + br_ref[...]
    lane = lax.broadcasted_iota(jnp.int32, logits.shape, 1).astype(f32)
    cur = logits
    vals, idxs = [], []
    for _ in range(TOP_K):
        m = jnp.max(cur, axis=-1, keepdims=True)
        i = jnp.min(jnp.where(cur == m, lane, float(logits.shape[-1])), axis=-1, keepdims=True)
        vals.append(m)
        idxs.append(i)
        cur = jnp.where(lane == i, -jnp.inf, cur)
    es = [jnp.exp(v - vals[0]) for v in vals]
    denom = es[0] + es[1] + es[2] + es[3]
    route = jnp.zeros(logits.shape, f32)
    for k in range(TOP_K):
        route = jnp.where(lane == k, idxs[k], route)
        route = jnp.where(lane == TOP_K + k, es[k] / denom, route)
        route = jnp.where(lane - float(MEMBER_LANE) == idxs[k], 1.0, route)
    route_ref[...] = route


def _outproj(a_na, a_hg, x, w_out_bf, g1, sh2, sc2, gain, w_router, b_router, tm):
    b, l, d = x.shape
    hw = a_na.shape[-1]
    tm = min(tm, l)
    n_e = w_router.shape[-1]
    wr = jnp.zeros((d, 128), f32).at[:, :n_e].set(w_router)
    br = jnp.full((1, 128), NEG_BIG, f32).at[0, :n_e].set(b_router)
    row = lambda last: pl.BlockSpec((None, tm, last), lambda bi, mi: (bi, mi, 0))
    vec = pl.BlockSpec((None, 1, d), lambda bi, mi: (bi, 0, 0))
    return pl.pallas_call(
        _outproj_kernel,
        out_shape=(jax.ShapeDtypeStruct((b, l, d), f32),
                   jax.ShapeDtypeStruct((b, l, d // 2), u32),
                   jax.ShapeDtypeStruct((b, l, 128), f32)),
        grid=(b, l // tm),
        in_specs=[row(hw), row(hw), row(d),
                  pl.BlockSpec((hw, d), lambda bi, mi: (0, 0)),
                  pl.BlockSpec((hw, d), lambda bi, mi: (1, 0)),
                  vec, vec, vec,
                  pl.BlockSpec((1, d), lambda bi, mi: (0, 0)),
                  pl.BlockSpec((d, 128), lambda bi, mi: (0, 0)),
                  pl.BlockSpec((1, 128), lambda bi, mi: (0, 0))],
        out_specs=(row(d), row(d // 2), row(128)),
        compiler_params=_cparams(("parallel", "parallel"), 48),
        name="outproj",
    )(a_na, a_hg, x, w_out_bf, w_out_bf, g1, sh2, sc2, gain, wr, br)


CUMSUM_TM = 512


def _cumsum_kernel(route_ref, tri_ref, o_ref, carry):
    @pl.when(pl.program_id(0) == 0)
    def _():
        carry[...] = jnp.zeros_like(carry)

    r = route_ref[...]
    lane = lax.broadcasted_iota(jnp.int32, r.shape, 1)
    member = jnp.where((lane >= MEMBER_LANE) & (lane < MEMBER_LANE + N_EXPERTS), r, 0.0).astype(bf16)
    incl = jnp.dot(tri_ref[...], member, preferred_element_type=f32) + carry[...]
    o_ref[...] = incl
    carry[...] = incl[CUMSUM_TM - 1:CUMSUM_TM, :]


def _token_cumsum(route):
    t = route.shape[0]
    tri = jnp.asarray(np.tril(np.ones((CUMSUM_TM, CUMSUM_TM), np.float32)), bf16)
    return pl.pallas_call(
        _cumsum_kernel,
        out_shape=jax.ShapeDtypeStruct((t, 128), f32),
        grid=(t // CUMSUM_TM,),
        in_specs=[pl.BlockSpec((CUMSUM_TM, 128), lambda i: (i, 0)),
                  pl.BlockSpec((CUMSUM_TM, CUMSUM_TM), lambda i: (0, 0))],
        out_specs=pl.BlockSpec((CUMSUM_TM, 128), lambda i: (i, 0)),
        scratch_shapes=[pltpu.VMEM((1, 128), f32)],
        compiler_params=_cparams(("arbitrary",)),
        name="cumsum",
    )(route, tri)


DISPATCH_TM = 256
TOK_UNROLL = 2


def _dispatch_kernel(pe_ref, dest_ref, hp_ref, xs_ref, zero_scr, sem):
    @pl.when(pl.program_id(0) == 0)
    def _():
        zero_scr[...] = jnp.zeros_like(zero_scr)

        def tail_copy(e):
            start = pl.multiple_of(pe_ref[e] - MOE_TM, MOE_TM)
            return pltpu.make_async_copy(zero_scr, xs_ref.at[pl.ds(start, MOE_TM), :], sem)

        def has_rows(e):
            return pe_ref[e] > jnp.where(e == 0, 0, pe_ref[jnp.maximum(e - 1, 0)])

        def issue_tail(e, c):
            @pl.when(has_rows(e))
            def _():
                tail_copy(e).start()
            return c

        def drain_tail(e, c):
            @pl.when(has_rows(e))
            def _():
                tail_copy(e).wait()
            return c

        lax.fori_loop(0, N_EXPERTS, issue_tail, 0)
        lax.fori_loop(0, N_EXPERTS, drain_tail, 0)

        def dead_copy(m):
            return pltpu.make_async_copy(zero_scr, xs_ref.at[pl.ds(pl.multiple_of(m * MOE_TM, MOE_TM), MOE_TM), :], sem)

        def issue_dead(m, c):
            dead_copy(m).start()
            return c

        def drain_dead(m, c):
            dead_copy(m).wait()
            return c

        n_used = pe_ref[N_EXPERTS - 1] // MOE_TM
        lax.fori_loop(n_used, xs_ref.shape[0] // MOE_TM, issue_dead, 0)
        lax.fori_loop(n_used, xs_ref.shape[0] // MOE_TM, drain_dead, 0)

    def row_copy(t, k):
        return pltpu.make_async_copy(hp_ref.at[pl.ds(t, 1), :], xs_ref.at[pl.ds(dest_ref[t * TOP_K + k], 1), :], sem)

    def issue(i, c):
        for u in range(TOK_UNROLL):
            for k in range(TOP_K):
                row_copy(i * TOK_UNROLL + u, k).start()
        return c

    def drain(i, c):
        for u in range(TOK_UNROLL):
            for k in range(TOP_K):
                row_copy(i * TOK_UNROLL + u, k).wait()
        return c

    lax.fori_loop(0, DISPATCH_TM // TOK_UNROLL, issue, 0)
    lax.fori_loop(0, DISPATCH_TM // TOK_UNROLL, drain, 0)


def _dispatch(pad_end, dest, hp, cap):
    t, kp = hp.shape
    return pl.pallas_call(
        _dispatch_kernel,
        out_shape=jax.ShapeDtypeStruct((cap, kp), hp.dtype),
        grid_spec=pltpu.PrefetchScalarGridSpec(
            num_scalar_prefetch=1,
            grid=(t // DISPATCH_TM,),
            in_specs=[pl.BlockSpec((DISPATCH_TM * TOP_K,), lambda i, pe: (i,), memory_space=pltpu.SMEM),
                      pl.BlockSpec((DISPATCH_TM, kp), lambda i, pe: (i, 0))],
            out_specs=pl.BlockSpec(memory_space=pl.ANY),
            scratch_shapes=[pltpu.VMEM((MOE_TM, kp), hp.dtype), pltpu.SemaphoreType.DMA(())]),
        compiler_params=_cparams(("arbitrary",)),
        name="dispatch",
    )(pad_end, dest, hp)


def _gemm1_kernel(be_ref, first_ref, nu_ref, xs_ref, wg_ref, wl_ref, bg_ref, bl_ref, act_ref, wg_bf, wl_bf):
    m = pl.program_id(1)

    @pl.when(m < nu_ref[0])
    def _():
        @pl.when(first_ref[m] == 1)
        def _():
            wg_bf[...] = wg_ref[...].astype(bf16)
            wl_bf[...] = wl_ref[...].astype(bf16)

        lo, hi = _unpack_halves(xs_ref[...])
        lo = lo.astype(bf16)
        hi = hi.astype(bf16)
        k = lo.shape[-1]
        gate = (jnp.dot(lo, wg_bf[:k, :], preferred_element_type=f32)
                + jnp.dot(hi, wg_bf[k:, :], preferred_element_type=f32) + bg_ref[...])
        lin = (jnp.dot(lo, wl_bf[:k, :], preferred_element_type=f32)
               + jnp.dot(hi, wl_bf[k:, :], preferred_element_type=f32) + bl_ref[...])
        x_glu = jnp.minimum(gate, SWIGLU_LIMIT)
        x_lin = jnp.clip(lin, -SWIGLU_LIMIT, SWIGLU_LIMIT)
        act_ref[...] = (x_glu * jax.nn.sigmoid(SWIGLU_ALPHA * x_glu) * (x_lin + 1.0)).astype(act_ref.dtype)

    @pl.when(m >= nu_ref[0])
    def _():
        act_ref[...] = jnp.zeros_like(act_ref)


def _gemm1(block_exp, first, n_used, xs, w_gu, b_gu, tn):
    cap, kp = xs.shape
    n_e, d, f2 = w_gu.shape
    ff = f2 // 2
    tn = min(tn, ff)
    nb = cap // MOE_TM
    nj = ff // tn
    live = lambda m, nu: jnp.minimum(m, nu[0] - 1)
    return pl.pallas_call(
        _gemm1_kernel,
        out_shape=jax.ShapeDtypeStruct((cap, ff), bf16),
        grid_spec=pltpu.PrefetchScalarGridSpec(
            num_scalar_prefetch=3,
            grid=(nj, nb),
            in_specs=[pl.BlockSpec((MOE_TM, kp), lambda j, m, be, fi, nu: (live(m, nu), 0)),
                      pl.BlockSpec((None, d, tn), lambda j, m, be, fi, nu: (be[m], 0, j)),
                      pl.BlockSpec((None, d, tn), lambda j, m, be, fi, nu: (be[m], 0, nj + j)),
                      pl.BlockSpec((None, 1, tn), lambda j, m, be, fi, nu: (be[m], 0, j)),
                      pl.BlockSpec((None, 1, tn), lambda j, m, be, fi, nu: (be[m], 0, nj + j))],
            out_specs=pl.BlockSpec((MOE_TM, tn), lambda j, m, be, fi, nu: (m, j)),
            scratch_shapes=[pltpu.VMEM((d, tn), bf16), pltpu.VMEM((d, tn), bf16)]),
        compiler_params=_cparams(("arbitrary", "arbitrary"), 56),
        name="gemm1",
    )(block_exp, first, n_used, xs, w_gu, w_gu, b_gu.reshape(n_e, 1, f2), b_gu.reshape(n_e, 1, f2))


def _gemm2_kernel(be_ref, first_ref, nu_ref, act_ref, w_ref, b_ref, y_ref, w_bf):
    m = pl.program_id(0)

    @pl.when(m < nu_ref[0])
    def _():
        @pl.when(first_ref[m] == 1)
        def _():
            w_bf[...] = w_ref[...].astype(bf16)

        y = jnp.dot(act_ref[...], w_bf[...], preferred_element_type=f32) + b_ref[...]
        y_ref[...] = _pack_halves(y)

    @pl.when(m >= nu_ref[0])
    def _():
        y_ref[...] = jnp.zeros_like(y_ref)


def _gemm2(block_exp, first, n_used, act, w_down, b_down):
    cap, ff = act.shape
    n_e, _, d = w_down.shape
    nb = cap // MOE_TM
    live = lambda m, nu: jnp.minimum(m, nu[0] - 1)
    return pl.pallas_call(
        _gemm2_kernel,
        out_shape=jax.ShapeDtypeStruct((cap, d // 2), u32),
        grid_spec=pltpu.PrefetchScalarGridSpec(
            num_scalar_prefetch=3,
            grid=(nb,),
            in_specs=[pl.BlockSpec((MOE_TM, ff), lambda m, be, fi, nu: (live(m, nu), 0)),
                      pl.BlockSpec((None, ff, d), lambda m, be, fi, nu: (be[m], 0, 0)),
                      pl.BlockSpec((None, 1, d), lambda m, be, fi, nu: (be[m], 0, 0))],
            out_specs=pl.BlockSpec((MOE_TM, d // 2), lambda m, be, fi, nu: (m, 0)),
            scratch_shapes=[pltpu.VMEM((ff, d), bf16)]),
        compiler_params=_cparams(("arbitrary",), 56),
        name="gemm2",
    )(block_exp, first, n_used, act, w_down, b_down.reshape(n_e, 1, d))


COMBINE_TM = 256


def _combine_kernel(dest_ref, yb_ref, route_ref, x1_ref, g2_ref, gain_ref, o_ref, buf, sem):
    def row_copy(t, k):
        return pltpu.make_async_copy(yb_ref.at[pl.ds(dest_ref[t * TOP_K + k], 1), :], buf.at[k, pl.ds(t, 1), :], sem)

    def issue(i, c):
        for u in range(TOK_UNROLL):
            for k in range(TOP_K):
                row_copy(i * TOK_UNROLL + u, k).start()
        return c

    def drain(i, c):
        for u in range(TOK_UNROLL):
            for k in range(TOP_K):
                row_copy(i * TOK_UNROLL + u, k).wait()
        return c

    lax.fori_loop(0, COMBINE_TM // TOK_UNROLL, issue, 0)
    lax.fori_loop(0, COMBINE_TM // TOK_UNROLL, drain, 0)

    route = route_ref[...]
    acc = None
    for k in range(TOP_K):
        lo, hi = _unpack_halves(buf[k])
        term = route[:, TOP_K + k:TOP_K + k + 1] * jnp.concatenate([lo, hi], axis=-1)
        acc = term if acc is None else acc + term
    x2 = x1_ref[...] + g2_ref[...] * acc
    ms = jnp.mean(x2 * x2, axis=-1, keepdims=True)
    o_ref[...] = x2 * lax.rsqrt(ms + EPS) * gain_ref[...]


def _combine(dest, yb, route, x1, g2, gain):
    b, l, d = x1.shape
    tm = min(COMBINE_TM, l)
    assert tm == COMBINE_TM
    per_b = l // tm
    return pl.pallas_call(
        _combine_kernel,
        out_shape=jax.ShapeDtypeStruct((b, l, d), f32),
        grid=(b * per_b,),
        in_specs=[pl.BlockSpec((tm * TOP_K,), lambda i: (i,), memory_space=pltpu.SMEM),
                  pl.BlockSpec(memory_space=pl.ANY),
                  pl.BlockSpec((None, tm, 128), lambda i: (i // per_b, i % per_b, 0)),
                  pl.BlockSpec((None, tm, d), lambda i: (i // per_b, i % per_b, 0)),
                  pl.BlockSpec((None, 1, d), lambda i: (i // per_b, 0, 0)),
                  pl.BlockSpec((1, d), lambda i: (0, 0))],
        out_specs=pl.BlockSpec((None, tm, d), lambda i: (i // per_b, i % per_b, 0)),
        scratch_shapes=[pltpu.VMEM((TOP_K, tm, d // 2), u32), pltpu.SemaphoreType.DMA(())],
        compiler_params=_cparams(("arbitrary",), 40),
        name="combine",
    )(dest, yb, route, x1, g2, gain)


def _routing(route, n_blocks):
    experts = slice(MEMBER_LANE, MEMBER_LANE + N_EXPERTS)
    top_idx = route[:, :TOP_K].astype(jnp.int32)
    incl = _token_cumsum(route)[:, experts]
    rank = (incl - route[:, experts]).astype(jnp.int32)
    counts = incl[-1].astype(jnp.int32)
    padded = (counts + MOE_TM - 1) // MOE_TM * MOE_TM
    pad_end = jnp.cumsum(padded).astype(jnp.int32)
    slot0 = (pad_end - padded)[None, :] + rank
    onehot = top_idx[:, :, None] == jnp.arange(N_EXPERTS, dtype=jnp.int32)[None, None, :]
    dest = jnp.sum(jnp.where(onehot, slot0[:, None, :], 0), axis=-1).astype(jnp.int32)
    block_start = jnp.arange(n_blocks, dtype=jnp.int32) * MOE_TM
    block_exp = jnp.minimum(jnp.sum(pad_end[None, :] <= block_start[:, None], axis=1), N_EXPERTS - 1).astype(jnp.int32)
    first = jnp.concatenate([jnp.ones((1,), jnp.int32), (block_exp[1:] != block_exp[:-1]).astype(jnp.int32)])
    n_used = (pad_end[-1:] // MOE_TM).astype(jnp.int32)
    return dest.reshape(-1), pad_end, block_exp, first, n_used


def _moe(hp, route, x1, g2, norm_final, w_gu, b_gu, w_down, b_down):
    b, l, d = x1.shape
    t = b * l
    n_blocks = -(-(t * TOP_K + N_EXPERTS * (MOE_TM - 1)) // MOE_TM)
    dest, pad_end, block_exp, first, n_used = _routing(route.reshape(t, 128), n_blocks)
    xs = _dispatch(pad_end, dest, hp.reshape(t, d // 2), n_blocks * MOE_TM)
    act = _gemm1(block_exp, first, n_used, xs, w_gu, b_gu, 1024)
    yb = _gemm2(block_exp, first, n_used, act, w_down, b_down)
    return _combine(dest, yb, route, x1, g2, norm_final.reshape(1, d))


def kernel(x, c, ctx, c_ctx, w_ada, b_ada, norm_mix, norm_ffn, w_in, lb_table, hg_norm, rpb, w_out, w_router,
           b_router, w_gu, b_gu, w_down, b_down, norm_final):
    b, l, d = x.shape
    assert w_ada.shape[0] == 1, "single-layer block"
    rows = l // GRID_W

    c16 = jnp.zeros((16, d), f32).at[:b].set(c).at[b].set(c_ctx)
    mod = _ada(c16, w_ada[0], b_ada[0])
    sh1, sc1, g1, sh2, sc2, g2 = [mod[:b, i * d:(i + 1) * d].reshape(b, 1, d) for i in range(6)]
    csh = jnp.broadcast_to(mod[b, :d].reshape(1, 1, d), (b, 1, d))
    csc = jnp.broadcast_to(mod[b, d:2 * d].reshape(1, 1, d), (b, 1, d))

    w_in_bf = w_in[0].astype(bf16)
    gain_mix = norm_mix[0].reshape(1, d)
    px_att = _inproj(x, sh1, sc1, gain_mix, w_in_bf, P_QN, 3, bf16, 1024)
    px_hg = _inproj(x, sh1, sc1, gain_mix, w_in_bf, P_QH, 5, f32, 1024)
    pc_att = _inproj(ctx, csh, csc, gain_mix, w_in_bf, P_KN, 2, bf16, 256)
    pc_hg = _inproj(ctx, csh, csc, gain_mix, w_in_bf, P_FF, 3, f32, 256)

    o_na = _natten(px_att, pc_att, _natten_bias_table(rpb[0], rows))

    lower_bounds = jnp.cumsum(jax.nn.softmax(lb_table.astype(f32), axis=0), axis=0)
    o_hg = _hgrn(px_hg, pc_hg, lower_bounds[0].reshape(2, HEADS * HEAD_DIM), hg_norm[0])

    x1, hp, route = _outproj(o_na, o_hg, x, w_out[0].astype(bf16), g1, sh2, sc2, norm_ffn[0].reshape(1, d),
                             w_router[0], b_router[0], 256)
    return _moe(hp, route, x1, g2, norm_final, w_gu[0], b_gu[0], w_down[0], b_down[0])
```

```python
import functools

import numpy as np
import jax
import jax.numpy as jnp
from jax import lax
from jax.experimental import pallas as pl
from jax.experimental.pallas import tpu as pltpu

f32 = jnp.float32
bf16 = jnp.bfloat16
u32 = jnp.uint32

GRID_W = 64
HEADS = 8
HEAD_DIM = 128
WIN_R = 8
WIN_C = 16
ROPE_THETA = 10000.0
N_EXPERTS = 32
TOP_K = 4
SWIGLU_LIMIT = 7.0
SWIGLU_ALPHA = 1.702
EPS = 1e-6
P_QN, P_KN, P_VN, P_QH, P_FF, P_FB, P_IH, P_GH = range(8)

CHUNK = 64
SUB = 16
CHUNK_UNROLL = 4
ROW_UNROLL = 4
MOE_TM = 256
NEG_BIG = -1e30
MEMBER_LANE = 32
HI_MASK = 0xFFFF0000

_HIGHEST = lax.Precision.HIGHEST


def _cparams(sem, vmem_mb=None):
    kw = dict(dimension_semantics=sem)
    if vmem_mb is not None:
        kw["vmem_limit_bytes"] = vmem_mb * 1024 * 1024
    return pltpu.CompilerParams(**kw)


def _ada_kernel(c_ref, w_ref, b_ref, o_ref):
    c = c_ref[...]
    cond = c * jax.nn.sigmoid(c)
    o_ref[...] = jnp.dot(cond, w_ref[...], precision=_HIGHEST, preferred_element_type=f32) + b_ref[...]


def _ada(c16, w_ada, b_ada):
    d, n = w_ada.shape
    tn = min(1024, n)
    return pl.pallas_call(
        _ada_kernel,
        out_shape=jax.ShapeDtypeStruct((c16.shape[0], n), f32),
        grid=(n // tn,),
        in_specs=[pl.BlockSpec((c16.shape[0], d), lambda j: (0, 0)),
                  pl.BlockSpec((d, tn), lambda j: (0, j)),
                  pl.BlockSpec((1, tn), lambda j: (0, j))],
        out_specs=pl.BlockSpec((c16.shape[0], tn), lambda j: (0, j)),
        compiler_params=_cparams(("parallel",), 40),
        name="ada",
    )(c16, w_ada, b_ada.reshape(1, n))


def _inproj_kernel(x_ref, shift_ref, scale_ref, gain_ref, w_ref, o_ref, h_scr):
    @pl.when(pl.program_id(2) == 0)
    def _():
        x = x_ref[...]
        ms = jnp.mean(x * x, axis=-1, keepdims=True)
        y = x * lax.rsqrt(ms + EPS) * gain_ref[...]
        h_scr[...] = (y * (1.0 + scale_ref[...]) + shift_ref[...]).astype(bf16)

    acc = jnp.dot(h_scr[...], w_ref[...], preferred_element_type=f32)
    for hh in range(HEADS):
        o_ref[hh] = acc[:, hh * HEAD_DIM:(hh + 1) * HEAD_DIM].astype(o_ref.dtype)


def _inproj(x, shift, scale, gain, w_bf, part_lo, n_parts, out_dtype, tm):
    b, l, d = x.shape
    pw = HEADS * HEAD_DIM
    tm = min(tm, l)
    return pl.pallas_call(
        _inproj_kernel,
        out_shape=jax.ShapeDtypeStruct((b, n_parts * HEADS, l, HEAD_DIM), out_dtype),
        grid=(b, l // tm, n_parts),
        in_specs=[pl.BlockSpec((None, tm, d), lambda bi, mi, ni: (bi, mi, 0)),
                  pl.BlockSpec((None, 1, d), lambda bi, mi, ni: (bi, 0, 0)),
                  pl.BlockSpec((None, 1, d), lambda bi, mi, ni: (bi, 0, 0)),
                  pl.BlockSpec((1, d), lambda bi, mi, ni: (0, 0)),
                  pl.BlockSpec((d, pw), lambda bi, mi, ni: (0, part_lo + ni))],
        out_specs=pl.BlockSpec((None, HEADS, tm, HEAD_DIM), lambda bi, mi, ni: (bi, ni, mi, 0)),
        scratch_shapes=[pltpu.VMEM((tm, d), bf16)],
        compiler_params=_cparams(("parallel", "parallel", "arbitrary"), 48),
        name="inproj",
    )(x, shift, scale, gain, w_bf)


def _natten_bias_table(rpb, rows):
    kr = min(WIN_R, rows)
    q = np.arange(GRID_W)
    col_start = np.clip(q - WIN_C // 2, 0, GRID_W - WIN_C)
    kc = np.arange(GRID_W)
    in_win = (kc[None, :] >= col_start[:, None]) & (kc[None, :] < col_start[:, None] + WIN_C)
    d_col = np.clip(kc[None, :] - q[:, None] + WIN_C - 1, 0, 2 * WIN_C - 2)
    n_d0 = 2 * WIN_R - 1 - (kr - 1)
    onehot = (d_col[None] == np.arange(2 * WIN_C - 1)[:, None, None]).astype(np.float32)
    cols = jnp.einsum("hrc,cqk->hrqk", rpb.astype(f32), jnp.asarray(onehot), precision=_HIGHEST)
    cols = jnp.where(in_win[None, None], cols, -jnp.inf)
    t = jnp.stack([cols[:, d0:d0 + kr] for d0 in range(n_d0)], axis=1)
    t = jnp.transpose(t, (0, 1, 3, 2, 4))
    return t.reshape(rpb.shape[0], n_d0, GRID_W, kr * GRID_W)


def _natten_kernel(q_ref, k_ref, v_ref, kc_ref, vc_ref, bias_ref, o_ref, *, rows, kr):
    scale = HEAD_DIM ** -0.5
    kc = kc_ref[...]
    vc = vc_ref[...]
    nt = (((1,), (1,)), ((), ()))

    def body(i, carry):
        q0 = pl.multiple_of(i * (ROW_UNROLL * GRID_W), ROW_UNROLL * GRID_W)
        q_all = q_ref[pl.ds(q0, ROW_UNROLL * GRID_W), :]
        s_c = lax.dot_general(q_all, kc, nt, preferred_element_type=f32) * scale
        m_c = jnp.max(s_c, axis=-1, keepdims=True)
        s_w, vws = [], []
        for u in range(ROW_UNROLL):
            r = i * ROW_UNROLL + u
            kr0 = jnp.clip(r - kr // 2, 0, rows - kr)
            k0 = pl.multiple_of(kr0 * GRID_W, GRID_W)
            kw = k_ref[pl.ds(k0, kr * GRID_W), :]
            vws.append(v_ref[pl.ds(k0, kr * GRID_W), :])
            q = q_all[u * GRID_W:(u + 1) * GRID_W, :]
            s_w.append(lax.dot_general(q, kw, nt, preferred_element_type=f32) * scale
                       + bias_ref[kr0 - r + WIN_R - 1])
        m = jnp.maximum(jnp.concatenate([jnp.max(s, axis=-1, keepdims=True) for s in s_w], axis=0), m_c)
        p_c = jnp.exp(s_c - m)
        p_w = [jnp.exp(s - m[u * GRID_W:(u + 1) * GRID_W, :]) for u, s in enumerate(s_w)]
        denom = (jnp.concatenate([jnp.sum(p, axis=-1, keepdims=True) for p in p_w], axis=0)
                 + jnp.sum(p_c, axis=-1, keepdims=True))
        o_c = jnp.dot(p_c.astype(bf16), vc, preferred_element_type=f32)
        o_w = jnp.concatenate([jnp.dot(p.astype(bf16), vw, preferred_element_type=f32) for p, vw in zip(p_w, vws)],
                              axis=0)
        o_ref[pl.ds(q0, ROW_UNROLL * GRID_W), :] = ((o_w + o_c) / denom).astype(o_ref.dtype)
        return carry

    lax.fori_loop(0, rows // ROW_UNROLL, body, 0)


def _natten(px_att, pc_att, bias_tab):
    b, _, l, _ = px_att.shape
    lc = pc_att.shape[2]
    rows = l // GRID_W
    kr = min(WIN_R, rows)
    n_d0 = bias_tab.shape[1]
    blk = lambda off: pl.BlockSpec((None, None, l, HEAD_DIM), lambda h, bi: (bi, off + h, 0, 0))
    cblk = lambda off: pl.BlockSpec((None, None, lc, HEAD_DIM), lambda h, bi: (bi, off + h, 0, 0))
    return pl.pallas_call(
        functools.partial(_natten_kernel, rows=rows, kr=kr),
        out_shape=jax.ShapeDtypeStruct((b, l, HEADS * HEAD_DIM), bf16),
        grid=(HEADS, b),
        in_specs=[blk(0), blk(HEADS), blk(2 * HEADS), cblk(0), cblk(HEADS),
                  pl.BlockSpec((None, n_d0, GRID_W, kr * GRID_W), lambda h, bi: (h, 0, 0, 0))],
        out_specs=pl.BlockSpec((None, l, HEAD_DIM), lambda h, bi: (bi, 0, h)),
        compiler_params=_cparams(("parallel", "parallel")),
        name="natten",
    )(px_att, px_att, px_att, pc_att, pc_att, bias_tab)


def _hgrn_consts():
    t = np.arange(CHUNK)
    bt, bs = t[:, None] // SUB, t[None, :] // SUB
    tri, masks = [], []
    for sgn in (1, -1):
        before = (t[None, :] <= t[:, None]) if sgn == 1 else (t[None, :] >= t[:, None])
        tri.append(before.astype(np.float32))
        dist = (bt - bs) * sgn
        masks.append(np.stack([dist == 1, dist == 2, dist == 3, (dist == 0) & before]).astype(np.float32))
    return np.stack(tri), np.stack(masks)


def _rope_tables(l):
    t = jnp.arange(l)
    n_freq = HEAD_DIM // 4
    inv_freq = ROPE_THETA ** (-jnp.arange(n_freq, dtype=f32) / n_freq)
    ang_row = (t // GRID_W).astype(f32)[:, None] * inv_freq
    ang_col = (t % GRID_W).astype(f32)[:, None] * inv_freq
    cr, sr, cc, sc = jnp.cos(ang_row), jnp.sin(ang_row), jnp.cos(ang_col), jnp.sin(ang_col)
    z = jnp.zeros_like(sr)
    cos_t = jnp.concatenate([cr, cr, cc, cc], axis=-1)
    sin_up = jnp.concatenate([-sr, z, -sc, z], axis=-1)
    sin_dn = jnp.concatenate([z, sr, z, sc], axis=-1)
    return cos_t, sin_up, sin_dn


def _split3_dot(tri_bf, g):
    g1 = g.astype(bf16)
    r1 = g - g1.astype(f32)
    g2 = r1.astype(bf16)
    g3 = (r1 - g2.astype(f32)).astype(bf16)
    dot = lambda a: jnp.dot(tri_bf, a, preferred_element_type=f32)
    return dot(g1) + dot(g2) + dot(g3)


def _gate(f_raw, lb):
    log_f = jnp.log(lb + (1.0 - lb) * jax.nn.sigmoid(f_raw))
    key = (1.0 - lb) * jax.nn.sigmoid(-f_raw)
    return log_f, key


def _chunk_refs(cum, backward):
    nb = CHUNK // SUB
    if backward:
        ends = [cum[i * SUB:i * SUB + 1, :] for i in range(nb)]
        order = list(range(nb - 1, -1, -1))
    else:
        ends = [cum[i * SUB + SUB - 1:i * SUB + SUB, :] for i in range(nb)]
        order = list(range(nb))
    zero = jnp.zeros_like(ends[0])
    b_rows, g2_rows, g3_rows = [None] * nb, [None] * nb, [None] * nb
    for pos, i in enumerate(order):
        b_i = zero if pos == 0 else ends[order[pos - 1]]
        b_rows[i] = b_i
        g2_rows[i] = b_i - ends[order[pos - 2]] if pos >= 2 else zero
        g3_rows[i] = b_i - ends[order[pos - 3]] if pos >= 3 else zero
    expand = lambda rws: jnp.concatenate([jnp.broadcast_to(r, (SUB, HEAD_DIM)) for r in rws], axis=0)
    total = ends[order[-1]]
    return expand(b_rows), expand(ends), expand(g2_rows), expand(g3_rows), total


def _rope(x, cos_t, sin_up, sin_dn):
    return x * cos_t + pltpu.roll(x, 96, 1) * sin_up + pltpu.roll(x, 32, 1) * sin_dn


def _scan_group(states, items, tris, masks):
    nt = (((1,), (1,)), ((), ()))
    tn = (((0,), (0,)), ((), ()))
    pre = []
    for d, q, f_raw, v, lb, rope in items:
        g, key = _gate(f_raw, lb)
        pre.append((g, _rope(q, *rope), _rope(key, *rope), v.astype(bf16)))
    cums = [_split3_dot(tris[it[0]], p[0]) for it, p in zip(items, pre)]
    ops = []
    for it, (g, qr, kr, v_bf), cum in zip(items, pre, cums):
        b, e, gap2, gap3, total = _chunk_refs(cum, it[0] == 1)
        q_t = qr * jnp.exp(cum - b)
        k_hat = kr * jnp.exp(e - cum)
        k_til = kr * jnp.exp(b - cum)
        lhs = jnp.concatenate([q_t, q_t * jnp.exp(gap2), q_t * jnp.exp(gap3)], axis=0).astype(bf16)
        rhs = jnp.concatenate([k_hat, k_til], axis=0).astype(bf16)
        q_in = (q_t * jnp.exp(b)).astype(bf16)
        k_dec = (k_hat * jnp.exp(total - e)).astype(bf16)
        ops.append((lhs, rhs, q_in, k_dec, jnp.exp(total)))
    scores = [lax.dot_general(o[0], o[1], nt, preferred_element_type=f32) for o in ops]
    upds = [lax.dot_general(p[3], o[3], tn, preferred_element_type=f32) for p, o in zip(pre, ops)]
    intra = []
    for it, p, (g, qr, kr, v_bf) in zip(items, scores, pre):
        m = masks[it[0]]
        att = (jnp.where(m[0] > 0, p[0:CHUNK, 0:CHUNK], 0.0)
               + jnp.where(m[1] > 0, p[CHUNK:2 * CHUNK, 0:CHUNK], 0.0)
               + jnp.where(m[2] > 0, p[2 * CHUNK:3 * CHUNK, 0:CHUNK], 0.0)
               + jnp.where(m[3] > 0, p[0:CHUNK, CHUNK:2 * CHUNK], 0.0))
        intra.append(jnp.dot(att.astype(bf16), v_bf, preferred_element_type=f32))
    states = list(states)
    outs = []
    for it, o, upd, o_in in zip(items, ops, upds, intra):
        st = states[it[0]]
        outs.append(o_in + lax.dot_general(o[2], st.astype(bf16), nt, preferred_element_type=f32))
        states[it[0]] = st * o[4] + upd
    return states, outs


def _hgrn_kernel(q_ref, ff_ref, fb_ref, v_ref, gate_ref, cff_ref, cfb_ref, cv_ref, lb_ref, gain_ref,
                 cos_ref, sup_ref, sdn_ref, tri_ref, mask_ref, o_ref, of_scr, ob_scr, *, n_chunks, n_cchunks):
    lb_f = lb_ref[0]
    lb_b = lb_ref[1]
    tri_f = tri_ref[0].astype(bf16)
    tri_b = tri_ref[1].astype(bf16)
    zero = jnp.zeros((HEAD_DIM, HEAD_DIM), f32)

    ctx_items = []
    for c in range(n_cchunks):
        rf = pl.ds(c * CHUNK, CHUNK)
        rb = pl.ds((n_cchunks - 1 - c) * CHUNK, CHUNK)
        ctx_items.append((0, cv_ref[rf, :], cff_ref[rf, :], lb_f))
        ctx_items.append((1, cv_ref[rb, :], cfb_ref[rb, :], lb_b))
    st_f, st_b = _state_group([zero, zero], ctx_items, (tri_f, tri_b))

    def body(c, carry):
        masks = [[mask_ref[d, i] for i in range(4)] for d in range(2)]
        items, rows = [], []
        for u in range(CHUNK_UNROLL):
            cf = c * CHUNK_UNROLL + u
            rf = pl.ds(pl.multiple_of(cf * CHUNK, CHUNK), CHUNK)
            rb = pl.ds(pl.multiple_of((n_chunks - 1 - cf) * CHUNK, CHUNK), CHUNK)
            items.append((0, q_ref[rf, :], ff_ref[rf, :], v_ref[rf, :], lb_f,
                          (cos_ref[rf, :], sup_ref[rf, :], sdn_ref[rf, :])))
            items.append((1, q_ref[rb, :], fb_ref[rb, :], v_ref[rb, :], lb_b,
                          (cos_ref[rb, :], sup_ref[rb, :], sdn_ref[rb, :])))
            rows += [(of_scr, rf), (ob_scr, rb)]
        states, outs = _scan_group(carry, items, (tri_f, tri_b), masks)
        for (scr, rws), o in zip(rows, outs):
            scr[rws, :] = o
        return tuple(states)

    lax.fori_loop(0, n_chunks // CHUNK_UNROLL, body, (st_f, st_b))

    o = of_scr[...] + ob_scr[...]
    y = o * lax.rsqrt(jnp.mean(o * o, axis=-1, keepdims=True) + EPS) * gain_ref[...]
    gate = gate_ref[...]
    o_ref[...] = (y * (gate * jax.nn.sigmoid(gate))).astype(o_ref.dtype)


def _state_group(states, items, tris):
    gates = [_gate(f_raw, lb) for _, _, f_raw, lb in items]
    cums = [_split3_dot(tris[it[0]], g) for it, (g, _) in zip(items, gates)]
    terms = []
    for it, (_, key), cum in zip(items, gates, cums):
        total = cum[0:1, :] if it[0] == 1 else cum[CHUNK - 1:CHUNK, :]
        terms.append(((key * jnp.exp(total - cum)).astype(bf16), jnp.exp(total)))
    upds = [lax.dot_general(it[1].astype(bf16), k_dec, (((0,), (0,)), ((), ())), preferred_element_type=f32)
            for it, (k_dec, _) in zip(items, terms)]
    states = list(states)
    for it, (_, decay), upd in zip(items, terms, upds):
        states[it[0]] = states[it[0]] * decay + upd
    return states


def _hgrn(px_hg, pc_hg, lb2, hg_gain):
    b, _, l, _ = px_hg.shape
    lc = pc_hg.shape[2]
    cos_t, sin_up, sin_dn = _rope_tables(l)
    tri, masks = _hgrn_consts()
    blk = lambda off: pl.BlockSpec((None, None, l, HEAD_DIM), lambda h, bi: (bi, off + h, 0, 0))
    cblk = lambda off: pl.BlockSpec((None, None, lc, HEAD_DIM), lambda h, bi: (bi, off + h, 0, 0))
    full = lambda shp: pl.BlockSpec(shp, lambda h, bi: (0,) * len(shp))
    return pl.pallas_call(
        functools.partial(_hgrn_kernel, n_chunks=l // CHUNK, n_cchunks=lc // CHUNK),
        out_shape=jax.ShapeDtypeStruct((b, l, HEADS * HEAD_DIM), bf16),
        grid=(HEADS, b),
        in_specs=[blk(0), blk(HEADS), blk(2 * HEADS), blk(3 * HEADS), blk(4 * HEADS),
                  cblk(0), cblk(HEADS), cblk(2 * HEADS),
                  pl.BlockSpec((2, None, 1, HEAD_DIM), lambda h, bi: (0, h, 0, 0)),
                  full((1, HEAD_DIM)),
                  full((l, HEAD_DIM)), full((l, HEAD_DIM)), full((l, HEAD_DIM)),
                  full((2, CHUNK, CHUNK)), full((2, 4, CHUNK, CHUNK))],
        out_specs=pl.BlockSpec((None, l, HEAD_DIM), lambda h, bi: (bi, 0, h)),
        scratch_shapes=[pltpu.VMEM((l, HEAD_DIM), f32), pltpu.VMEM((l, HEAD_DIM), f32)],
        compiler_params=_cparams(("parallel", "parallel")),
        name="hgrn",
    )(px_hg, px_hg, px_hg, px_hg, px_hg, pc_hg, pc_hg, pc_hg,
      lb2.reshape(2, HEADS, 1, HEAD_DIM), hg_gain.reshape(1, HEAD_DIM),
      cos_t, sin_up, sin_dn, jnp.asarray(tri), jnp.asarray(masks))


def _pack_halves(h):
    k = h.shape[-1] // 2
    lo = lax.bitcast_convert_type(h[:, :k].astype(f32), u32)
    hi = lax.bitcast_convert_type(h[:, k:].astype(f32), u32)
    return (lo >> 16) | (hi & u32(HI_MASK))


def _unpack_halves(u):
    lo = lax.bitcast_convert_type(u << 16, f32)
    hi = lax.bitcast_convert_type(u & u32(HI_MASK), f32)
    return lo, hi


def _outproj_kernel(ana_ref, ahg_ref, x_ref, w0_ref, w1_ref, g1_ref, sh_ref, sc_ref, gain_ref, wr_ref, br_ref,
                    x1_ref, hp_ref, route_ref):
    y = (jnp.dot(ana_ref[...], w0_ref[...], preferred_element_type=f32)
         + jnp.dot(ahg_ref[...], w1_ref[...], preferred_element_type=f32))
    x1 = x_ref[...] + g1_ref[...] * y
    x1_ref[...] = x1
    ms = jnp.mean(x1 * x1, axis=-1, keepdims=True)
    h = x1 * lax.rsqrt(ms + EPS) * gain_ref[...]
    h = h * (1.0 + sc_ref[...]) + sh_ref[...]
    h_hi = h.astype(bf16)
    hp_ref[...] = _pack_halves(h_hi)
    h_lo = (h - h_hi.astype(f32)).astype(bf16)
    t = jnp.dot(h_hi, wr_ref[...], preferred_element_type=f32)
    logits = (t[:, :128] + t[:, 128:] + jnp.dot(h_lo, wr_ref[:, :128], preferred_element_type=f32)) + br_ref[...]
    lane = lax.broadcasted_iota(jnp.int32, logits.shape, 1).astype(f32)
    cur = logits
    vals, idxs = [], []
    for _ in range(TOP_K):
        m = jnp.max(cur, axis=-1, keepdims=True)
        i = jnp.min(jnp.where(cur == m, lane, float(logits.shape[-1])), axis=-1, keepdims=True)
        vals.append(m)
        idxs.append(i)
        cur = jnp.where(lane == i, -jnp.inf, cur)
    es = [jnp.exp(v - vals[0]) for v in vals]
    denom = es[0] + es[1] + es[2] + es[3]
    route = jnp.zeros(logits.shape, f32)
    for k in range(TOP_K):
        route = jnp.where(lane == k, idxs[k], route)
        route = jnp.where(lane == TOP_K + k, es[k] / denom, route)
        route = jnp.where(lane - float(MEMBER_LANE) == idxs[k], 1.0, route)
    route_ref[...] = route


def _outproj(a_na, a_hg, x, w_out_bf, g1, sh2, sc2, gain, w_router, b_router, tm):
    b, l, d = x.shape
    hw = a_na.shape[-1]
    tm = min(tm, l)
    n_e = w_router.shape[-1]
    wr = jnp.zeros((d, 128), f32).at[:, :n_e].set(w_router)
    wr_hi = wr.astype(bf16)
    wr = jnp.concatenate([wr_hi, (wr - wr_hi.astype(f32)).astype(bf16)], axis=1)
    br = jnp.full((1, 128), NEG_BIG, f32).at[0, :n_e].set(b_router)
    row = lambda last: pl.BlockSpec((None, tm, last), lambda bi, mi: (bi, mi, 0))
    vec = pl.BlockSpec((None, 1, d), lambda bi, mi: (bi, 0, 0))
    return pl.pallas_call(
        _outproj_kernel,
        out_shape=(jax.ShapeDtypeStruct((b, l, d), f32),
                   jax.ShapeDtypeStruct((b, l, d // 2), u32),
                   jax.ShapeDtypeStruct((b, l, 128), f32)),
        grid=(b, l // tm),
        in_specs=[row(hw), row(hw), row(d),
                  pl.BlockSpec((hw, d), lambda bi, mi: (0, 0)),
                  pl.BlockSpec((hw, d), lambda bi, mi: (1, 0)),
                  vec, vec, vec,
                  pl.BlockSpec((1, d), lambda bi, mi: (0, 0)),
                  pl.BlockSpec((d, 256), lambda bi, mi: (0, 0)),
                  pl.BlockSpec((1, 128), lambda bi, mi: (0, 0))],
        out_specs=(row(d), row(d // 2), row(128)),
        compiler_params=_cparams(("parallel", "parallel"), 48),
        name="outproj",
    )(a_na, a_hg, x, w_out_bf, w_out_bf, g1, sh2, sc2, gain, wr, br)


CUMSUM_TM = 512


def _cumsum_kernel(route_ref, tri_ref, o_ref, carry):
    @pl.when(pl.program_id(0) == 0)
    def _():
        carry[...] = jnp.zeros_like(carry)

    r = route_ref[...]
    lane = lax.broadcasted_iota(jnp.int32, r.shape, 1)
    member = jnp.where((lane >= MEMBER_LANE) & (lane < MEMBER_LANE + N_EXPERTS), r, 0.0).astype(bf16)
    incl = jnp.dot(tri_ref[...], member, preferred_element_type=f32) + carry[...]
    o_ref[...] = incl
    carry[...] = incl[CUMSUM_TM - 1:CUMSUM_TM, :]


def _token_cumsum(route):
    t = route.shape[0]
    tri = jnp.asarray(np.tril(np.ones((CUMSUM_TM, CUMSUM_TM), np.float32)), bf16)
    return pl.pallas_call(
        _cumsum_kernel,
        out_shape=jax.ShapeDtypeStruct((t, 128), f32),
        grid=(t // CUMSUM_TM,),
        in_specs=[pl.BlockSpec((CUMSUM_TM, 128), lambda i: (i, 0)),
                  pl.BlockSpec((CUMSUM_TM, CUMSUM_TM), lambda i: (0, 0))],
        out_specs=pl.BlockSpec((CUMSUM_TM, 128), lambda i: (i, 0)),
        scratch_shapes=[pltpu.VMEM((1, 128), f32)],
        compiler_params=_cparams(("arbitrary",)),
        name="cumsum",
    )(route, tri)


DISPATCH_TM = 256
TOK_UNROLL = 2


def _dispatch_kernel(pe_ref, dest_ref, hp_ref, xs_ref, zero_scr, sem):
    @pl.when(pl.program_id(0) == 0)
    def _():
        zero_scr[...] = jnp.zeros_like(zero_scr)

        def tail_copy(e):
            start = pl.multiple_of(pe_ref[e] - MOE_TM, MOE_TM)
            return pltpu.make_async_copy(zero_scr, xs_ref.at[pl.ds(start, MOE_TM), :], sem)

        def has_rows(e):
            return pe_ref[e] > jnp.where(e == 0, 0, pe_ref[jnp.maximum(e - 1, 0)])

        def issue_tail(e, c):
            @pl.when(has_rows(e))
            def _():
                tail_copy(e).start()
            return c

        def drain_tail(e, c):
            @pl.when(has_rows(e))
            def _():
                tail_copy(e).wait()
            return c

        lax.fori_loop(0, N_EXPERTS, issue_tail, 0)
        lax.fori_loop(0, N_EXPERTS, drain_tail, 0)

        def dead_copy(m):
            return pltpu.make_async_copy(zero_scr, xs_ref.at[pl.ds(pl.multiple_of(m * MOE_TM, MOE_TM), MOE_TM), :], sem)

        def issue_dead(m, c):
            dead_copy(m).start()
            return c

        def drain_dead(m, c):
            dead_copy(m).wait()
            return c

        n_used = pe_ref[N_EXPERTS - 1] // MOE_TM
        lax.fori_loop(n_used, xs_ref.shape[0] // MOE_TM, issue_dead, 0)
        lax.fori_loop(n_used, xs_ref.shape[0] // MOE_TM, drain_dead, 0)

    def row_copy(t, k):
        return pltpu.make_async_copy(hp_ref.at[pl.ds(t, 1), :], xs_ref.at[pl.ds(dest_ref[t * TOP_K + k], 1), :], sem)

    def issue(i, c):
        for u in range(TOK_UNROLL):
            for k in range(TOP_K):
                row_copy(i * TOK_UNROLL + u, k).start()
        return c

    def drain(i, c):
        for u in range(TOK_UNROLL):
            for k in range(TOP_K):
                row_copy(i * TOK_UNROLL + u, k).wait()
        return c

    lax.fori_loop(0, DISPATCH_TM // TOK_UNROLL, issue, 0)
    lax.fori_loop(0, DISPATCH_TM // TOK_UNROLL, drain, 0)


def _dispatch(pad_end, dest, hp, cap):
    t, kp = hp.shape
    return pl.pallas_call(
        _dispatch_kernel,
        out_shape=jax.ShapeDtypeStruct((cap, kp), hp.dtype),
        grid_spec=pltpu.PrefetchScalarGridSpec(
            num_scalar_prefetch=1,
            grid=(t // DISPATCH_TM,),
            in_specs=[pl.BlockSpec((DISPATCH_TM * TOP_K,), lambda i, pe: (i,), memory_space=pltpu.SMEM),
                      pl.BlockSpec((DISPATCH_TM, kp), lambda i, pe: (i, 0))],
            out_specs=pl.BlockSpec(memory_space=pl.ANY),
            scratch_shapes=[pltpu.VMEM((MOE_TM, kp), hp.dtype), pltpu.SemaphoreType.DMA(())]),
        compiler_params=_cparams(("arbitrary",)),
        name="dispatch",
    )(pad_end, dest, hp)


def _gemm1_kernel(be_ref, first_ref, nu_ref, xs_ref, wg_ref, wl_ref, bg_ref, bl_ref, act_ref, wg_bf, wl_bf):
    m = pl.program_id(1)

    @pl.when(m < nu_ref[0])
    def _():
        @pl.when(first_ref[m] == 1)
        def _():
            wg_bf[...] = wg_ref[...].astype(bf16)
            wl_bf[...] = wl_ref[...].astype(bf16)

        lo, hi = _unpack_halves(xs_ref[...])
        lo = lo.astype(bf16)
        hi = hi.astype(bf16)
        k = lo.shape[-1]
        gate = (jnp.dot(lo, wg_bf[:k, :], preferred_element_type=f32)
                + jnp.dot(hi, wg_bf[k:, :], preferred_element_type=f32) + bg_ref[...])
        lin = (jnp.dot(lo, wl_bf[:k, :], preferred_element_type=f32)
               + jnp.dot(hi, wl_bf[k:, :], preferred_element_type=f32) + bl_ref[...])
        x_glu = jnp.minimum(gate, SWIGLU_LIMIT)
        x_lin = jnp.clip(lin, -SWIGLU_LIMIT, SWIGLU_LIMIT)
        act_ref[...] = (x_glu * jax.nn.sigmoid(SWIGLU_ALPHA * x_glu) * (x_lin + 1.0)).astype(act_ref.dtype)

    @pl.when(m >= nu_ref[0])
    def _():
        act_ref[...] = jnp.zeros_like(act_ref)


def _gemm1(block_exp, first, n_used, xs, w_gu, b_gu, tn):
    cap, kp = xs.shape
    n_e, d, f2 = w_gu.shape
    ff = f2 // 2
    tn = min(tn, ff)
    nb = cap // MOE_TM
    nj = ff // tn
    live = lambda m, nu: jnp.minimum(m, nu[0] - 1)
    return pl.pallas_call(
        _gemm1_kernel,
        out_shape=jax.ShapeDtypeStruct((cap, ff), bf16),
        grid_spec=pltpu.PrefetchScalarGridSpec(
            num_scalar_prefetch=3,
            grid=(nj, nb),
            in_specs=[pl.BlockSpec((MOE_TM, kp), lambda j, m, be, fi, nu: (live(m, nu), 0)),
                      pl.BlockSpec((None, d, tn), lambda j, m, be, fi, nu: (be[m], 0, j)),
                      pl.BlockSpec((None, d, tn), lambda j, m, be, fi, nu: (be[m], 0, nj + j)),
                      pl.BlockSpec((None, 1, tn), lambda j, m, be, fi, nu: (be[m], 0, j)),
                      pl.BlockSpec((None, 1, tn), lambda j, m, be, fi, nu: (be[m], 0, nj + j))],
            out_specs=pl.BlockSpec((MOE_TM, tn), lambda j, m, be, fi, nu: (m, j)),
            scratch_shapes=[pltpu.VMEM((d, tn), bf16), pltpu.VMEM((d, tn), bf16)]),
        compiler_params=_cparams(("arbitrary", "arbitrary"), 56),
        name="gemm1",
    )(block_exp, first, n_used, xs, w_gu, w_gu, b_gu.reshape(n_e, 1, f2), b_gu.reshape(n_e, 1, f2))


def _gemm2_kernel(be_ref, first_ref, nu_ref, act_ref, w_ref, b_ref, y_ref, w_bf):
    m = pl.program_id(0)

    @pl.when(m < nu_ref[0])
    def _():
        @pl.when(first_ref[m] == 1)
        def _():
            w_bf[...] = w_ref[...].astype(bf16)

        y = jnp.dot(act_ref[...], w_bf[...], preferred_element_type=f32) + b_ref[...]
        y_ref[...] = _pack_halves(y.astype(bf16))

    @pl.when(m >= nu_ref[0])
    def _():
        y_ref[...] = jnp.zeros_like(y_ref)


def _gemm2(block_exp, first, n_used, act, w_down, b_down):
    cap, ff = act.shape
    n_e, _, d = w_down.shape
    nb = cap // MOE_TM
    live = lambda m, nu: jnp.minimum(m, nu[0] - 1)
    return pl.pallas_call(
        _gemm2_kernel,
        out_shape=jax.ShapeDtypeStruct((cap, d // 2), u32),
        grid_spec=pltpu.PrefetchScalarGridSpec(
            num_scalar_prefetch=3,
            grid=(nb,),
            in_specs=[pl.BlockSpec((MOE_TM, ff), lambda m, be, fi, nu: (live(m, nu), 0)),
                      pl.BlockSpec((None, ff, d), lambda m, be, fi, nu: (be[m], 0, 0)),
                      pl.BlockSpec((None, 1, d), lambda m, be, fi, nu: (be[m], 0, 0))],
            out_specs=pl.BlockSpec((MOE_TM, d // 2), lambda m, be, fi, nu: (m, 0)),
            scratch_shapes=[pltpu.VMEM((ff, d), bf16)]),
        compiler_params=_cparams(("arbitrary",), 56),
        name="gemm2",
    )(block_exp, first, n_used, act, w_down, b_down.reshape(n_e, 1, d))


COMBINE_TM = 256


def _combine_kernel(dest_ref, yb_ref, route_ref, x1_ref, g2_ref, gain_ref, o_ref, buf, sem):
    def row_copy(t, k):
        return pltpu.make_async_copy(yb_ref.at[pl.ds(dest_ref[t * TOP_K + k], 1), :], buf.at[k, pl.ds(t, 1), :], sem)

    def issue(i, c):
        for u in range(TOK_UNROLL):
            for k in range(TOP_K):
                row_copy(i * TOK_UNROLL + u, k).start()
        return c

    def drain(i, c):
        for u in range(TOK_UNROLL):
            for k in range(TOP_K):
                row_copy(i * TOK_UNROLL + u, k).wait()
        return c

    lax.fori_loop(0, COMBINE_TM // TOK_UNROLL, issue, 0)
    lax.fori_loop(0, COMBINE_TM // TOK_UNROLL, drain, 0)

    route = route_ref[...]
    acc = None
    for k in range(TOP_K):
        lo, hi = _unpack_halves(buf[k])
        term = route[:, TOP_K + k:TOP_K + k + 1] * jnp.concatenate([lo, hi], axis=-1)
        acc = term if acc is None else acc + term
    x2 = x1_ref[...] + g2_ref[...] * acc
    ms = jnp.mean(x2 * x2, axis=-1, keepdims=True)
    o_ref[...] = x2 * lax.rsqrt(ms + EPS) * gain_ref[...]


def _combine(dest, yb, route, x1, g2, gain):
    b, l, d = x1.shape
    tm = min(COMBINE_TM, l)
    assert tm == COMBINE_TM
    per_b = l // tm
    return pl.pallas_call(
        _combine_kernel,
        out_shape=jax.ShapeDtypeStruct((b, l, d), f32),
        grid=(b * per_b,),
        in_specs=[pl.BlockSpec((tm * TOP_K,), lambda i: (i,), memory_space=pltpu.SMEM),
                  pl.BlockSpec(memory_space=pl.ANY),
                  pl.BlockSpec((None, tm, 128), lambda i: (i // per_b, i % per_b, 0)),
                  pl.BlockSpec((None, tm, d), lambda i: (i // per_b, i % per_b, 0)),
                  pl.BlockSpec((None, 1, d), lambda i: (i // per_b, 0, 0)),
                  pl.BlockSpec((1, d), lambda i: (0, 0))],
        out_specs=pl.BlockSpec((None, tm, d), lambda i: (i // per_b, i % per_b, 0)),
        scratch_shapes=[pltpu.VMEM((TOP_K, tm, d // 2), u32), pltpu.SemaphoreType.DMA(())],
        compiler_params=_cparams(("arbitrary",), 40),
        name="combine",
    )(dest, yb, route, x1, g2, gain)


def _routing(route, n_blocks):
    experts = slice(MEMBER_LANE, MEMBER_LANE + N_EXPERTS)
    top_idx = route[:, :TOP_K].astype(jnp.int32)
    incl = _token_cumsum(route)[:, experts]
    rank = (incl - route[:, experts]).astype(jnp.int32)
    counts = incl[-1].astype(jnp.int32)
    padded = (counts + MOE_TM - 1) // MOE_TM * MOE_TM
    pad_end = jnp.cumsum(padded).astype(jnp.int32)
    slot0 = (pad_end - padded)[None, :] + rank
    onehot = top_idx[:, :, None] == jnp.arange(N_EXPERTS, dtype=jnp.int32)[None, None, :]
    dest = jnp.sum(jnp.where(onehot, slot0[:, None, :], 0), axis=-1).astype(jnp.int32)
    block_start = jnp.arange(n_blocks, dtype=jnp.int32) * MOE_TM
    block_exp = jnp.minimum(jnp.sum(pad_end[None, :] <= block_start[:, None], axis=1), N_EXPERTS - 1).astype(jnp.int32)
    first = jnp.concatenate([jnp.ones((1,), jnp.int32), (block_exp[1:] != block_exp[:-1]).astype(jnp.int32)])
    n_used = (pad_end[-1:] // MOE_TM).astype(jnp.int32)
    return dest.reshape(-1), pad_end, block_exp, first, n_used


def _moe(hp, route, x1, g2, norm_final, w_gu, b_gu, w_down, b_down):
    b, l, d = x1.shape
    t = b * l
    n_blocks = -(-(t * TOP_K + N_EXPERTS * (MOE_TM - 1)) // MOE_TM)
    dest, pad_end, block_exp, first, n_used = _routing(route.reshape(t, 128), n_blocks)
    xs = _dispatch(pad_end, dest, hp.reshape(t, d // 2), n_blocks * MOE_TM)
    act = _gemm1(block_exp, first, n_used, xs, w_gu, b_gu, 1024)
    yb = _gemm2(block_exp, first, n_used, act, w_down, b_down)
    return _combine(dest, yb, route, x1, g2, norm_final.reshape(1, d))


def kernel(x, c, ctx, c_ctx, w_ada, b_ada, norm_mix, norm_ffn, w_in, lb_table, hg_norm, rpb, w_out, w_router,
           b_router, w_gu, b_gu, w_down, b_down, norm_final):
    b, l, d = x.shape
    assert w_ada.shape[0] == 1, "single-layer block"
    rows = l // GRID_W

    c16 = jnp.zeros((16, d), f32).at[:b].set(c).at[b].set(c_ctx)
    mod = _ada(c16, w_ada[0], b_ada[0])
    sh1, sc1, g1, sh2, sc2, g2 = [mod[:b, i * d:(i + 1) * d].reshape(b, 1, d) for i in range(6)]
    csh = jnp.broadcast_to(mod[b, :d].reshape(1, 1, d), (b, 1, d))
    csc = jnp.broadcast_to(mod[b, d:2 * d].reshape(1, 1, d), (b, 1, d))

    w_in_bf = w_in[0].astype(bf16)
    gain_mix = norm_mix[0].reshape(1, d)
    px_att = _inproj(x, sh1, sc1, gain_mix, w_in_bf, P_QN, 3, bf16, 1024)
    px_hg = _inproj(x, sh1, sc1, gain_mix, w_in_bf, P_QH, 5, f32, 1024)
    pc_att = _inproj(ctx, csh, csc, gain_mix, w_in_bf, P_KN, 2, bf16, 256)
    pc_hg = _inproj(ctx, csh, csc, gain_mix, w_in_bf, P_FF, 3, f32, 256)

    o_na = _natten(px_att, pc_att, _natten_bias_table(rpb[0], rows))

    lower_bounds = jnp.cumsum(jax.nn.softmax(lb_table.astype(f32), axis=0), axis=0)
    o_hg = _hgrn(px_hg, pc_hg, lower_bounds[0].reshape(2, HEADS * HEAD_DIM), hg_norm[0])

    x1, hp, route = _outproj(o_na, o_hg, x, w_out[0].astype(bf16), g1, sh2, sc2, norm_ffn[0].reshape(1, d),
                             w_router[0], b_router[0], 256)
    return _moe(hp, route, x1, g2, norm_final, w_gu[0], b_gu[0], w_down[0], b_down[0])
```

```python
import functools

import numpy as np
import jax
import jax.numpy as jnp
from jax import lax
from jax.experimental import pallas as pl
from jax.experimental.pallas import tpu as pltpu

f32 = jnp.float32
bf16 = jnp.bfloat16
u32 = jnp.uint32

GRID_W = 64
HEADS = 8
HEAD_DIM = 128
WIN_R = 8
WIN_C = 16
ROPE_THETA = 10000.0
N_EXPERTS = 32
TOP_K = 4
SWIGLU_LIMIT = 7.0
SWIGLU_ALPHA = 1.702
EPS = 1e-6
P_QN, P_KN, P_VN, P_QH, P_FF, P_FB, P_IH, P_GH = range(8)

CHUNK = 64
SUB = 16
CHUNK_UNROLL = 4
ROW_UNROLL = 4
MOE_TM = 256
NEG_BIG = -1e30
MEMBER_LANE = 32
HI_MASK = 0xFFFF0000

_HIGHEST = lax.Precision.HIGHEST


def _cparams(sem, vmem_mb=None):
    kw = dict(dimension_semantics=sem)
    if vmem_mb is not None:
        kw["vmem_limit_bytes"] = vmem_mb * 1024 * 1024
    return pltpu.CompilerParams(**kw)


def _ada_kernel(c_ref, w_ref, b_ref, o_ref):
    c = c_ref[...]
    cond = c * jax.nn.sigmoid(c)
    o_ref[...] = jnp.dot(cond, w_ref[...], precision=_HIGHEST, preferred_element_type=f32) + b_ref[...]


def _ada(c16, w_ada, b_ada):
    d, n = w_ada.shape
    tn = min(1024, n)
    return pl.pallas_call(
        _ada_kernel,
        out_shape=jax.ShapeDtypeStruct((c16.shape[0], n), f32),
        grid=(n // tn,),
        in_specs=[pl.BlockSpec((c16.shape[0], d), lambda j: (0, 0)),
                  pl.BlockSpec((d, tn), lambda j: (0, j)),
                  pl.BlockSpec((1, tn), lambda j: (0, j))],
        out_specs=pl.BlockSpec((c16.shape[0], tn), lambda j: (0, j)),
        compiler_params=_cparams(("parallel",), 40),
        name="ada",
    )(c16, w_ada, b_ada.reshape(1, n))


def _inproj_kernel(x_ref, shift_ref, scale_ref, gain_ref, w_ref, o_ref, h_scr):
    @pl.when(pl.program_id(2) == 0)
    def _():
        x = x_ref[...]
        ms = jnp.mean(x * x, axis=-1, keepdims=True)
        y = x * lax.rsqrt(ms + EPS) * gain_ref[...]
        h_scr[...] = (y * (1.0 + scale_ref[...]) + shift_ref[...]).astype(bf16)

    acc = jnp.dot(h_scr[...], w_ref[...], preferred_element_type=f32)
    for hh in range(HEADS):
        o_ref[hh] = acc[:, hh * HEAD_DIM:(hh + 1) * HEAD_DIM].astype(o_ref.dtype)


def _inproj(x, shift, scale, gain, w_bf, part_lo, n_parts, out_dtype, tm):
    b, l, d = x.shape
    pw = HEADS * HEAD_DIM
    tm = min(tm, l)
    return pl.pallas_call(
        _inproj_kernel,
        out_shape=jax.ShapeDtypeStruct((b, n_parts * HEADS, l, HEAD_DIM), out_dtype),
        grid=(b, l // tm, n_parts),
        in_specs=[pl.BlockSpec((None, tm, d), lambda bi, mi, ni: (bi, mi, 0)),
                  pl.BlockSpec((None, 1, d), lambda bi, mi, ni: (bi, 0, 0)),
                  pl.BlockSpec((None, 1, d), lambda bi, mi, ni: (bi, 0, 0)),
                  pl.BlockSpec((1, d), lambda bi, mi, ni: (0, 0)),
                  pl.BlockSpec((d, pw), lambda bi, mi, ni: (0, part_lo + ni))],
        out_specs=pl.BlockSpec((None, HEADS, tm, HEAD_DIM), lambda bi, mi, ni: (bi, ni, mi, 0)),
        scratch_shapes=[pltpu.VMEM((tm, d), bf16)],
        compiler_params=_cparams(("parallel", "parallel", "arbitrary"), 48),
        name="inproj",
    )(x, shift, scale, gain, w_bf)


def _natten_bias_table(rpb, rows):
    kr = min(WIN_R, rows)
    q = np.arange(GRID_W)
    col_start = np.clip(q - WIN_C // 2, 0, GRID_W - WIN_C)
    kc = np.arange(GRID_W)
    in_win = (kc[None, :] >= col_start[:, None]) & (kc[None, :] < col_start[:, None] + WIN_C)
    d_col = np.clip(kc[None, :] - q[:, None] + WIN_C - 1, 0, 2 * WIN_C - 2)
    n_d0 = 2 * WIN_R - 1 - (kr - 1)
    onehot = (d_col[None] == np.arange(2 * WIN_C - 1)[:, None, None]).astype(np.float32)
    cols = jnp.einsum("hrc,cqk->hrqk", rpb.astype(f32), jnp.asarray(onehot), precision=_HIGHEST)
    cols = jnp.where(in_win[None, None], cols, -jnp.inf)
    t = jnp.stack([cols[:, d0:d0 + kr] for d0 in range(n_d0)], axis=1)
    t = jnp.transpose(t, (0, 1, 3, 2, 4))
    return t.reshape(rpb.shape[0], n_d0, GRID_W, kr * GRID_W)


def _natten_kernel(q_ref, k_ref, v_ref, kc_ref, vc_ref, bias_ref, o_ref, sw0, sc0, sw1, sc1, *, rows, kr):
    scale = HEAD_DIM ** -0.5
    kc = kc_ref[...]
    vc = vc_ref[...]
    nt = (((1,), (1,)), ((), ()))

    n_groups = rows // ROW_UNROLL
    gq = ROW_UNROLL * GRID_W

    def key_start(r):
        return jnp.clip(r - kr // 2, 0, rows - kr)

    def scores(g, sw_scr, sc_scr):
        q_all = q_ref[pl.ds(pl.multiple_of(g * gq, gq), gq), :]
        sc_scr[...] = lax.dot_general(q_all, kc, nt, preferred_element_type=f32) * scale
        for u in range(ROW_UNROLL):
            r = g * ROW_UNROLL + u
            kr0 = key_start(r)
            kw = k_ref[pl.ds(pl.multiple_of(kr0 * GRID_W, GRID_W), kr * GRID_W), :]
            q = q_all[u * GRID_W:(u + 1) * GRID_W, :]
            sw_scr[u] = (lax.dot_general(q, kw, nt, preferred_element_type=f32) * scale
                         + bias_ref[kr0 - r + WIN_R - 1])

    def attend(g, sw_scr, sc_scr):
        s_c = sc_scr[...]
        s_w = [sw_scr[u] for u in range(ROW_UNROLL)]
        m = jnp.maximum(jnp.concatenate([jnp.max(s, axis=-1, keepdims=True) for s in s_w], axis=0),
                        jnp.max(s_c, axis=-1, keepdims=True))
        p_c = jnp.exp(s_c - m)
        p_w = [jnp.exp(s - m[u * GRID_W:(u + 1) * GRID_W, :]) for u, s in enumerate(s_w)]
        denom = (jnp.concatenate([jnp.sum(p, axis=-1, keepdims=True) for p in p_w], axis=0)
                 + jnp.sum(p_c, axis=-1, keepdims=True))
        o_c = jnp.dot(p_c.astype(bf16), vc, preferred_element_type=f32)
        o_w = []
        for u, p in enumerate(p_w):
            k0 = pl.multiple_of(key_start(g * ROW_UNROLL + u) * GRID_W, GRID_W)
            o_w.append(jnp.dot(p.astype(bf16), v_ref[pl.ds(k0, kr * GRID_W), :], preferred_element_type=f32))
        o = (jnp.concatenate(o_w, axis=0) + o_c) / denom
        o_ref[pl.ds(pl.multiple_of(g * gq, gq), gq), :] = o.astype(o_ref.dtype)

    scores(0, sw0, sc0)

    def body(i, carry):
        g = 2 * i
        scores(g + 1, sw1, sc1)
        attend(g, sw0, sc0)
        scores(jnp.minimum(g + 2, n_groups - 1), sw0, sc0)
        attend(g + 1, sw1, sc1)
        return carry

    lax.fori_loop(0, n_groups // 2, body, 0)


def _natten(px_att, pc_att, bias_tab):
    b, _, l, _ = px_att.shape
    lc = pc_att.shape[2]
    rows = l // GRID_W
    kr = min(WIN_R, rows)
    n_d0 = bias_tab.shape[1]
    blk = lambda off: pl.BlockSpec((None, None, l, HEAD_DIM), lambda h, bi: (bi, off + h, 0, 0))
    cblk = lambda off: pl.BlockSpec((None, None, lc, HEAD_DIM), lambda h, bi: (bi, off + h, 0, 0))
    return pl.pallas_call(
        functools.partial(_natten_kernel, rows=rows, kr=kr),
        out_shape=jax.ShapeDtypeStruct((b, l, HEADS * HEAD_DIM), bf16),
        grid=(HEADS, b),
        in_specs=[blk(0), blk(HEADS), blk(2 * HEADS), cblk(0), cblk(HEADS),
                  pl.BlockSpec((None, n_d0, GRID_W, kr * GRID_W), lambda h, bi: (h, 0, 0, 0))],
        out_specs=pl.BlockSpec((None, l, HEAD_DIM), lambda h, bi: (bi, 0, h)),
        scratch_shapes=[pltpu.VMEM((ROW_UNROLL, GRID_W, kr * GRID_W), f32), pltpu.VMEM((ROW_UNROLL * GRID_W, lc), f32),
                        pltpu.VMEM((ROW_UNROLL, GRID_W, kr * GRID_W), f32), pltpu.VMEM((ROW_UNROLL * GRID_W, lc), f32)],
        compiler_params=_cparams(("parallel", "parallel")),
        name="natten",
    )(px_att, px_att, px_att, pc_att, pc_att, bias_tab)


def _hgrn_consts():
    t = np.arange(CHUNK)
    bt, bs = t[:, None] // SUB, t[None, :] // SUB
    tri, masks = [], []
    for sgn in (1, -1):
        before = (t[None, :] <= t[:, None]) if sgn == 1 else (t[None, :] >= t[:, None])
        tri.append(before.astype(np.float32))
        dist = (bt - bs) * sgn
        masks.append(np.stack([dist == 1, dist == 2, dist == 3, (dist == 0) & before]).astype(np.float32))
    return np.stack(tri), np.stack(masks)


def _rope_tables(l):
    t = jnp.arange(l)
    n_freq = HEAD_DIM // 4
    inv_freq = ROPE_THETA ** (-jnp.arange(n_freq, dtype=f32) / n_freq)
    ang_row = (t // GRID_W).astype(f32)[:, None] * inv_freq
    ang_col = (t % GRID_W).astype(f32)[:, None] * inv_freq
    cr, sr, cc, sc = jnp.cos(ang_row), jnp.sin(ang_row), jnp.cos(ang_col), jnp.sin(ang_col)
    z = jnp.zeros_like(sr)
    cos_t = jnp.concatenate([cr, cr, cc, cc], axis=-1)
    sin_up = jnp.concatenate([-sr, z, -sc, z], axis=-1)
    sin_dn = jnp.concatenate([z, sr, z, sc], axis=-1)
    return cos_t, sin_up, sin_dn


def _split3_dot(tri_bf, g):
    g1 = g.astype(bf16)
    r1 = g - g1.astype(f32)
    g2 = r1.astype(bf16)
    g3 = (r1 - g2.astype(f32)).astype(bf16)
    dot = lambda a: jnp.dot(tri_bf, a, preferred_element_type=f32)
    return dot(g1) + dot(g2) + dot(g3)


def _gate(f_raw, lb):
    log_f = jnp.log(lb + (1.0 - lb) * jax.nn.sigmoid(f_raw))
    key = (1.0 - lb) * jax.nn.sigmoid(-f_raw)
    return log_f, key


def _chunk_refs(cum, backward):
    nb = CHUNK // SUB
    if backward:
        ends = [cum[i * SUB:i * SUB + 1, :] for i in range(nb)]
        order = list(range(nb - 1, -1, -1))
    else:
        ends = [cum[i * SUB + SUB - 1:i * SUB + SUB, :] for i in range(nb)]
        order = list(range(nb))
    zero = jnp.zeros_like(ends[0])
    b_rows, g2_rows, g3_rows = [None] * nb, [None] * nb, [None] * nb
    for pos, i in enumerate(order):
        b_i = zero if pos == 0 else ends[order[pos - 1]]
        b_rows[i] = b_i
        g2_rows[i] = b_i - ends[order[pos - 2]] if pos >= 2 else zero
        g3_rows[i] = b_i - ends[order[pos - 3]] if pos >= 3 else zero
    expand = lambda rws: jnp.concatenate([jnp.broadcast_to(r, (SUB, HEAD_DIM)) for r in rws], axis=0)
    total = ends[order[-1]]
    return expand(b_rows), expand(ends), expand(g2_rows), expand(g3_rows), total


def _rope(x, cos_t, sin_up, sin_dn):
    return x * cos_t + pltpu.roll(x, 96, 1) * sin_up + pltpu.roll(x, 32, 1) * sin_dn


def _scan_group(states, items, tris, masks):
    nt = (((1,), (1,)), ((), ()))
    tn = (((0,), (0,)), ((), ()))
    pre = []
    for d, q, f_raw, v, lb, rope in items:
        g, key = _gate(f_raw, lb)
        pre.append((g, _rope(q, *rope), _rope(key, *rope), v.astype(bf16)))
    cums = [_split3_dot(tris[it[0]], p[0]) for it, p in zip(items, pre)]
    ops = []
    for it, (g, qr, kr, v_bf), cum in zip(items, pre, cums):
        b, e, gap2, gap3, total = _chunk_refs(cum, it[0] == 1)
        q_t = qr * jnp.exp(cum - b)
        k_hat = kr * jnp.exp(e - cum)
        k_til = kr * jnp.exp(b - cum)
        lhs = jnp.concatenate([q_t, q_t * jnp.exp(gap2), q_t * jnp.exp(gap3)], axis=0).astype(bf16)
        rhs = jnp.concatenate([k_hat, k_til], axis=0).astype(bf16)
        q_in = (q_t * jnp.exp(b)).astype(bf16)
        k_dec = (k_hat * jnp.exp(total - e)).astype(bf16)
        ops.append((lhs, rhs, q_in, k_dec, jnp.exp(total)))
    scores = [lax.dot_general(o[0], o[1], nt, preferred_element_type=f32) for o in ops]
    upds = [lax.dot_general(p[3], o[3], tn, preferred_element_type=f32) for p, o in zip(pre, ops)]
    intra = []
    for it, p, (g, qr, kr, v_bf) in zip(items, scores, pre):
        m = masks[it[0]]
        att = (jnp.where(m[0] > 0, p[0:CHUNK, 0:CHUNK], 0.0)
               + jnp.where(m[1] > 0, p[CHUNK:2 * CHUNK, 0:CHUNK], 0.0)
               + jnp.where(m[2] > 0, p[2 * CHUNK:3 * CHUNK, 0:CHUNK], 0.0)
               + jnp.where(m[3] > 0, p[0:CHUNK, CHUNK:2 * CHUNK], 0.0))
        intra.append(jnp.dot(att.astype(bf16), v_bf, preferred_element_type=f32))
    states = list(states)
    outs = []
    for it, o, upd, o_in in zip(items, ops, upds, intra):
        st = states[it[0]]
        outs.append(o_in + lax.dot_general(o[2], st.astype(bf16), nt, preferred_element_type=f32))
        states[it[0]] = st * o[4] + upd
    return states, outs


def _hgrn_kernel(q_ref, ff_ref, fb_ref, v_ref, gate_ref, cff_ref, cfb_ref, cv_ref, lb_ref, gain_ref,
                 cos_ref, sup_ref, sdn_ref, tri_ref, mask_ref, o_ref, of_scr, ob_scr, *, n_chunks, n_cchunks):
    lb_f = lb_ref[0]
    lb_b = lb_ref[1]
    tri_f = tri_ref[0].astype(bf16)
    tri_b = tri_ref[1].astype(bf16)
    zero = jnp.zeros((HEAD_DIM, HEAD_DIM), f32)

    ctx_items = []
    for c in range(n_cchunks):
        rf = pl.ds(c * CHUNK, CHUNK)
        rb = pl.ds((n_cchunks - 1 - c) * CHUNK, CHUNK)
        ctx_items.append((0, cv_ref[rf, :], cff_ref[rf, :], lb_f))
        ctx_items.append((1, cv_ref[rb, :], cfb_ref[rb, :], lb_b))
    st_f, st_b = _state_group([zero, zero], ctx_items, (tri_f, tri_b))

    def body(c, carry):
        masks = [[mask_ref[d, i] for i in range(4)] for d in range(2)]
        items, rows = [], []
        for u in range(CHUNK_UNROLL):
            cf = c * CHUNK_UNROLL + u
            rf = pl.ds(pl.multiple_of(cf * CHUNK, CHUNK), CHUNK)
            rb = pl.ds(pl.multiple_of((n_chunks - 1 - cf) * CHUNK, CHUNK), CHUNK)
            items.append((0, q_ref[rf, :], ff_ref[rf, :], v_ref[rf, :], lb_f,
                          (cos_ref[rf, :], sup_ref[rf, :], sdn_ref[rf, :])))
            items.append((1, q_ref[rb, :], fb_ref[rb, :], v_ref[rb, :], lb_b,
                          (cos_ref[rb, :], sup_ref[rb, :], sdn_ref[rb, :])))
            rows += [(of_scr, rf), (ob_scr, rb)]
        states, outs = _scan_group(carry, items, (tri_f, tri_b), masks)
        for (scr, rws), o in zip(rows, outs):
            scr[rws, :] = o
        return tuple(states)

    lax.fori_loop(0, n_chunks // CHUNK_UNROLL, body, (st_f, st_b))

    o = of_scr[...] + ob_scr[...]
    y = o * lax.rsqrt(jnp.mean(o * o, axis=-1, keepdims=True) + EPS) * gain_ref[...]
    gate = gate_ref[...]
    o_ref[...] = (y * (gate * jax.nn.sigmoid(gate))).astype(o_ref.dtype)


def _state_group(states, items, tris):
    gates = [_gate(f_raw, lb) for _, _, f_raw, lb in items]
    cums = [_split3_dot(tris[it[0]], g) for it, (g, _) in zip(items, gates)]
    terms = []
    for it, (_, key), cum in zip(items, gates, cums):
        total = cum[0:1, :] if it[0] == 1 else cum[CHUNK - 1:CHUNK, :]
        terms.append(((key * jnp.exp(total - cum)).astype(bf16), jnp.exp(total)))
    upds = [lax.dot_general(it[1].astype(bf16), k_dec, (((0,), (0,)), ((), ())), preferred_element_type=f32)
            for it, (k_dec, _) in zip(items, terms)]
    states = list(states)
    for it, (_, decay), upd in zip(items, terms, upds):
        states[it[0]] = states[it[0]] * decay + upd
    return states


def _hgrn(px_hg, pc_hg, lb2, hg_gain):
    b, _, l, _ = px_hg.shape
    lc = pc_hg.shape[2]
    cos_t, sin_up, sin_dn = _rope_tables(l)
    tri, masks = _hgrn_consts()
    blk = lambda off: pl.BlockSpec((None, None, l, HEAD_DIM), lambda h, bi: (bi, off + h, 0, 0))
    cblk = lambda off: pl.BlockSpec((None, None, lc, HEAD_DIM), lambda h, bi: (bi, off + h, 0, 0))
    full = lambda shp: pl.BlockSpec(shp, lambda h, bi: (0,) * len(shp))
    return pl.pallas_call(
        functools.partial(_hgrn_kernel, n_chunks=l // CHUNK, n_cchunks=lc // CHUNK),
        out_shape=jax.ShapeDtypeStruct((b, l, HEADS * HEAD_DIM), bf16),
        grid=(HEADS, b),
        in_specs=[blk(0), blk(HEADS), blk(2 * HEADS), blk(3 * HEADS), blk(4 * HEADS),
                  cblk(0), cblk(HEADS), cblk(2 * HEADS),
                  pl.BlockSpec((2, None, 1, HEAD_DIM), lambda h, bi: (0, h, 0, 0)),
                  full((1, HEAD_DIM)),
                  full((l, HEAD_DIM)), full((l, HEAD_DIM)), full((l, HEAD_DIM)),
                  full((2, CHUNK, CHUNK)), full((2, 4, CHUNK, CHUNK))],
        out_specs=pl.BlockSpec((None, l, HEAD_DIM), lambda h, bi: (bi, 0, h)),
        scratch_shapes=[pltpu.VMEM((l, HEAD_DIM), f32), pltpu.VMEM((l, HEAD_DIM), f32)],
        compiler_params=_cparams(("parallel", "parallel")),
        name="hgrn",
    )(px_hg, px_hg, px_hg, px_hg, px_hg, pc_hg, pc_hg, pc_hg,
      lb2.reshape(2, HEADS, 1, HEAD_DIM), hg_gain.reshape(1, HEAD_DIM),
      cos_t, sin_up, sin_dn, jnp.asarray(tri), jnp.asarray(masks))


def _pack_halves(h):
    k = h.shape[-1] // 2
    lo = lax.bitcast_convert_type(h[:, :k].astype(f32), u32)
    hi = lax.bitcast_convert_type(h[:, k:].astype(f32), u32)
    return (lo >> 16) | (hi & u32(HI_MASK))


def _unpack_halves(u):
    lo = lax.bitcast_convert_type(u << 16, f32)
    hi = lax.bitcast_convert_type(u & u32(HI_MASK), f32)
    return lo, hi


def _outproj_kernel(ana_ref, ahg_ref, x_ref, w0_ref, w1_ref, g1_ref, sh_ref, sc_ref, gain_ref, wr_ref, br_ref,
                    x1_ref, hp_ref, route_ref):
    y = (jnp.dot(ana_ref[...], w0_ref[...], preferred_element_type=f32)
         + jnp.dot(ahg_ref[...], w1_ref[...], preferred_element_type=f32))
    x1 = x_ref[...] + g1_ref[...] * y
    x1_ref[...] = x1
    ms = jnp.mean(x1 * x1, axis=-1, keepdims=True)
    h = x1 * lax.rsqrt(ms + EPS) * gain_ref[...]
    h = h * (1.0 + sc_ref[...]) + sh_ref[...]
    h_hi = h.astype(bf16)
    hp_ref[...] = _pack_halves(h_hi)
    h_lo = (h - h_hi.astype(f32)).astype(bf16)
    t = jnp.dot(h_hi, wr_ref[...], preferred_element_type=f32)
    logits = (t[:, :128] + t[:, 128:] + jnp.dot(h_lo, wr_ref[:, :128], preferred_element_type=f32)) + br_ref[...]
    lane = lax.broadcasted_iota(jnp.int32, logits.shape, 1).astype(f32)
    cur = logits
    vals, idxs = [], []
    for _ in range(TOP_K):
        m = jnp.max(cur, axis=-1, keepdims=True)
        i = jnp.min(jnp.where(cur == m, lane, float(logits.shape[-1])), axis=-1, keepdims=True)
        vals.append(m)
        idxs.append(i)
        cur = jnp.where(lane == i, -jnp.inf, cur)
    es = [jnp.exp(v - vals[0]) for v in vals]
    denom = es[0] + es[1] + es[2] + es[3]
    route = jnp.zeros(logits.shape, f32)
    for k in range(TOP_K):
        route = jnp.where(lane == k, idxs[k], route)
        route = jnp.where(lane == TOP_K + k, es[k] / denom, route)
        route = jnp.where(lane - float(MEMBER_LANE) == idxs[k], 1.0, route)
    route_ref[...] = route


def _outproj(a_na, a_hg, x, w_out_bf, g1, sh2, sc2, gain, w_router, b_router, tm):
    b, l, d = x.shape
    hw = a_na.shape[-1]
    tm = min(tm, l)
    n_e = w_router.shape[-1]
    wr = jnp.zeros((d, 128), f32).at[:, :n_e].set(w_router)
    wr_hi = wr.astype(bf16)
    wr = jnp.concatenate([wr_hi, (wr - wr_hi.astype(f32)).astype(bf16)], axis=1)
    br = jnp.full((1, 128), NEG_BIG, f32).at[0, :n_e].set(b_router)
    row = lambda last: pl.BlockSpec((None, tm, last), lambda bi, mi: (bi, mi, 0))
    vec = pl.BlockSpec((None, 1, d), lambda bi, mi: (bi, 0, 0))
    return pl.pallas_call(
        _outproj_kernel,
        out_shape=(jax.ShapeDtypeStruct((b, l, d), f32),
                   jax.ShapeDtypeStruct((b, l, d // 2), u32),
                   jax.ShapeDtypeStruct((b, l, 128), f32)),
        grid=(b, l // tm),
        in_specs=[row(hw), row(hw), row(d),
                  pl.BlockSpec((hw, d), lambda bi, mi: (0, 0)),
                  pl.BlockSpec((hw, d), lambda bi, mi: (1, 0)),
                  vec, vec, vec,
                  pl.BlockSpec((1, d), lambda bi, mi: (0, 0)),
                  pl.BlockSpec((d, 256), lambda bi, mi: (0, 0)),
                  pl.BlockSpec((1, 128), lambda bi, mi: (0, 0))],
        out_specs=(row(d), row(d // 2), row(128)),
        compiler_params=_cparams(("parallel", "parallel"), 48),
        name="outproj",
    )(a_na, a_hg, x, w_out_bf, w_out_bf, g1, sh2, sc2, gain, wr, br)


CUMSUM_TM = 512


def _cumsum_kernel(route_ref, tri_ref, o_ref, carry):
    @pl.when(pl.program_id(0) == 0)
    def _():
        carry[...] = jnp.zeros_like(carry)

    r = route_ref[...]
    lane = lax.broadcasted_iota(jnp.int32, r.shape, 1)
    member = jnp.where((lane >= MEMBER_LANE) & (lane < MEMBER_LANE + N_EXPERTS), r, 0.0).astype(bf16)
    incl = jnp.dot(tri_ref[...], member, preferred_element_type=f32) + carry[...]
    o_ref[...] = incl
    carry[...] = incl[CUMSUM_TM - 1:CUMSUM_TM, :]


def _token_cumsum(route):
    t = route.shape[0]
    tri = jnp.asarray(np.tril(np.ones((CUMSUM_TM, CUMSUM_TM), np.float32)), bf16)
    return pl.pallas_call(
        _cumsum_kernel,
        out_shape=jax.ShapeDtypeStruct((t, 128), f32),
        grid=(t // CUMSUM_TM,),
        in_specs=[pl.BlockSpec((CUMSUM_TM, 128), lambda i: (i, 0)),
                  pl.BlockSpec((CUMSUM_TM, CUMSUM_TM), lambda i: (0, 0))],
        out_specs=pl.BlockSpec((CUMSUM_TM, 128), lambda i: (i, 0)),
        scratch_shapes=[pltpu.VMEM((1, 128), f32)],
        compiler_params=_cparams(("arbitrary",)),
        name="cumsum",
    )(route, tri)


DISPATCH_TM = 256
TOK_UNROLL = 2


def _dispatch_kernel(pe_ref, dest_ref, hp_ref, xs_ref, zero_scr, sem):
    @pl.when(pl.program_id(0) == 0)
    def _():
        zero_scr[...] = jnp.zeros_like(zero_scr)

        def tail_copy(e):
            start = pl.multiple_of(pe_ref[e] - MOE_TM, MOE_TM)
            return pltpu.make_async_copy(zero_scr, xs_ref.at[pl.ds(start, MOE_TM), :], sem)

        def has_rows(e):
            return pe_ref[e] > jnp.where(e == 0, 0, pe_ref[jnp.maximum(e - 1, 0)])

        def issue_tail(e, c):
            @pl.when(has_rows(e))
            def _():
                tail_copy(e).start()
            return c

        def drain_tail(e, c):
            @pl.when(has_rows(e))
            def _():
                tail_copy(e).wait()
            return c

        lax.fori_loop(0, N_EXPERTS, issue_tail, 0)
        lax.fori_loop(0, N_EXPERTS, drain_tail, 0)

        def dead_copy(m):
            return pltpu.make_async_copy(zero_scr, xs_ref.at[pl.ds(pl.multiple_of(m * MOE_TM, MOE_TM), MOE_TM), :], sem)

        def issue_dead(m, c):
            dead_copy(m).start()
            return c

        def drain_dead(m, c):
            dead_copy(m).wait()
            return c

        n_used = pe_ref[N_EXPERTS - 1] // MOE_TM
        lax.fori_loop(n_used, xs_ref.shape[0] // MOE_TM, issue_dead, 0)
        lax.fori_loop(n_used, xs_ref.shape[0] // MOE_TM, drain_dead, 0)

    def row_copy(t, k):
        return pltpu.make_async_copy(hp_ref.at[pl.ds(t, 1), :], xs_ref.at[pl.ds(dest_ref[t * TOP_K + k], 1), :], sem)

    def issue(i, c):
        for u in range(TOK_UNROLL):
            for k in range(TOP_K):
                row_copy(i * TOK_UNROLL + u, k).start()
        return c

    def drain(i, c):
        for u in range(TOK_UNROLL):
            for k in range(TOP_K):
                row_copy(i * TOK_UNROLL + u, k).wait()
        return c

    lax.fori_loop(0, DISPATCH_TM // TOK_UNROLL, issue, 0)
    lax.fori_loop(0, DISPATCH_TM // TOK_UNROLL, drain, 0)


def _dispatch(pad_end, dest, hp, cap):
    t, kp = hp.shape
    return pl.pallas_call(
        _dispatch_kernel,
        out_shape=jax.ShapeDtypeStruct((cap, kp), hp.dtype),
        grid_spec=pltpu.PrefetchScalarGridSpec(
            num_scalar_prefetch=1,
            grid=(t // DISPATCH_TM,),
            in_specs=[pl.BlockSpec((DISPATCH_TM * TOP_K,), lambda i, pe: (i,), memory_space=pltpu.SMEM),
                      pl.BlockSpec((DISPATCH_TM, kp), lambda i, pe: (i, 0))],
            out_specs=pl.BlockSpec(memory_space=pl.ANY),
            scratch_shapes=[pltpu.VMEM((MOE_TM, kp), hp.dtype), pltpu.SemaphoreType.DMA(())]),
        compiler_params=_cparams(("arbitrary",)),
        name="dispatch",
    )(pad_end, dest, hp)


def _gemm1_kernel(be_ref, first_ref, nu_ref, xs_ref, wg_ref, wl_ref, bg_ref, bl_ref, act_ref, wg_bf, wl_bf):
    m = pl.program_id(1)

    @pl.when(m < nu_ref[0])
    def _():
        @pl.when(first_ref[m] == 1)
        def _():
            wg_bf[...] = wg_ref[...].astype(bf16)
            wl_bf[...] = wl_ref[...].astype(bf16)

        lo, hi = _unpack_halves(xs_ref[...])
        lo = lo.astype(bf16)
        hi = hi.astype(bf16)
        k = lo.shape[-1]
        gate = (jnp.dot(lo, wg_bf[:k, :], preferred_element_type=f32)
                + jnp.dot(hi, wg_bf[k:, :], preferred_element_type=f32) + bg_ref[...])
        lin = (jnp.dot(lo, wl_bf[:k, :], preferred_element_type=f32)
               + jnp.dot(hi, wl_bf[k:, :], preferred_element_type=f32) + bl_ref[...])
        x_glu = jnp.minimum(gate, SWIGLU_LIMIT)
        x_lin = jnp.clip(lin, -SWIGLU_LIMIT, SWIGLU_LIMIT)
        act_ref[...] = (x_glu * jax.nn.sigmoid(SWIGLU_ALPHA * x_glu) * (x_lin + 1.0)).astype(act_ref.dtype)

    @pl.when(m >= nu_ref[0])
    def _():
        act_ref[...] = jnp.zeros_like(act_ref)


def _gemm1(block_exp, first, n_used, xs, w_gu, b_gu, tn):
    cap, kp = xs.shape
    n_e, d, f2 = w_gu.shape
    ff = f2 // 2
    tn = min(tn, ff)
    nb = cap // MOE_TM
    nj = ff // tn
    live = lambda m, nu: jnp.minimum(m, nu[0] - 1)
    return pl.pallas_call(
        _gemm1_kernel,
        out_shape=jax.ShapeDtypeStruct((cap, ff), bf16),
        grid_spec=pltpu.PrefetchScalarGridSpec(
            num_scalar_prefetch=3,
            grid=(nj, nb),
            in_specs=[pl.BlockSpec((MOE_TM, kp), lambda j, m, be, fi, nu: (live(m, nu), 0)),
                      pl.BlockSpec((None, d, tn), lambda j, m, be, fi, nu: (be[m], 0, j)),
                      pl.BlockSpec((None, d, tn), lambda j, m, be, fi, nu: (be[m], 0, nj + j)),
                      pl.BlockSpec((None, 1, tn), lambda j, m, be, fi, nu: (be[m], 0, j)),
                      pl.BlockSpec((None, 1, tn), lambda j, m, be, fi, nu: (be[m], 0, nj + j))],
            out_specs=pl.BlockSpec((MOE_TM, tn), lambda j, m, be, fi, nu: (m, j)),
            scratch_shapes=[pltpu.VMEM((d, tn), bf16), pltpu.VMEM((d, tn), bf16)]),
        compiler_params=_cparams(("arbitrary", "arbitrary"), 56),
        name="gemm1",
    )(block_exp, first, n_used, xs, w_gu, w_gu, b_gu.reshape(n_e, 1, f2), b_gu.reshape(n_e, 1, f2))


def _gemm2_kernel(be_ref, first_ref, nu_ref, act_ref, w_ref, b_ref, y_ref, w_bf):
    m = pl.program_id(0)

    @pl.when(m < nu_ref[0])
    def _():
        @pl.when(first_ref[m] == 1)
        def _():
            w_bf[...] = w_ref[...].astype(bf16)

        y = jnp.dot(act_ref[...], w_bf[...], preferred_element_type=f32) + b_ref[...]
        y_ref[...] = _pack_halves(y.astype(bf16))

    @pl.when(m >= nu_ref[0])
    def _():
        y_ref[...] = jnp.zeros_like(y_ref)


def _gemm2(block_exp, first, n_used, act, w_down, b_down):
    cap, ff = act.shape
    n_e, _, d = w_down.shape
    nb = cap // MOE_TM
    live = lambda m, nu: jnp.minimum(m, nu[0] - 1)
    return pl.pallas_call(
        _gemm2_kernel,
        out_shape=jax.ShapeDtypeStruct((cap, d // 2), u32),
        grid_spec=pltpu.PrefetchScalarGridSpec(
            num_scalar_prefetch=3,
            grid=(nb,),
            in_specs=[pl.BlockSpec((MOE_TM, ff), lambda m, be, fi, nu: (live(m, nu), 0)),
                      pl.BlockSpec((None, ff, d), lambda m, be, fi, nu: (be[m], 0, 0)),
                      pl.BlockSpec((None, 1, d), lambda m, be, fi, nu: (be[m], 0, 0))],
            out_specs=pl.BlockSpec((MOE_TM, d // 2), lambda m, be, fi, nu: (m, 0)),
            scratch_shapes=[pltpu.VMEM((ff, d), bf16)]),
        compiler_params=_cparams(("arbitrary",), 56),
        name="gemm2",
    )(block_exp, first, n_used, act, w_down, b_down.reshape(n_e, 1, d))


COMBINE_TM = 256


def _combine_kernel(dest_ref, yb_ref, route_ref, x1_ref, g2_ref, gain_ref, o_ref, buf, sem):
    def row_copy(t, k):
        return pltpu.make_async_copy(yb_ref.at[pl.ds(dest_ref[t * TOP_K + k], 1), :], buf.at[k, pl.ds(t, 1), :], sem)

    def issue(i, c):
        for u in range(TOK_UNROLL):
            for k in range(TOP_K):
                row_copy(i * TOK_UNROLL + u, k).start()
        return c

    def drain(i, c):
        for u in range(TOK_UNROLL):
            for k in range(TOP_K):
                row_copy(i * TOK_UNROLL + u, k).wait()
        return c

    lax.fori_loop(0, COMBINE_TM // TOK_UNROLL, issue, 0)
    lax.fori_loop(0, COMBINE_TM // TOK_UNROLL, drain, 0)

    route = route_ref[...]
    acc = None
    for k in range(TOP_K):
        lo, hi = _unpack_halves(buf[k])
        term = route[:, TOP_K + k:TOP_K + k + 1] * jnp.concatenate([lo, hi], axis=-1)
        acc = term if acc is None else acc + term
    x2 = x1_ref[...] + g2_ref[...] * acc
    ms = jnp.mean(x2 * x2, axis=-1, keepdims=True)
    o_ref[...] = x2 * lax.rsqrt(ms + EPS) * gain_ref[...]


def _combine(dest, yb, route, x1, g2, gain):
    b, l, d = x1.shape
    tm = min(COMBINE_TM, l)
    assert tm == COMBINE_TM
    per_b = l // tm
    return pl.pallas_call(
        _combine_kernel,
        out_shape=jax.ShapeDtypeStruct((b, l, d), f32),
        grid=(b * per_b,),
        in_specs=[pl.BlockSpec((tm * TOP_K,), lambda i: (i,), memory_space=pltpu.SMEM),
                  pl.BlockSpec(memory_space=pl.ANY),
                  pl.BlockSpec((None, tm, 128), lambda i: (i // per_b, i % per_b, 0)),
                  pl.BlockSpec((None, tm, d), lambda i: (i // per_b, i % per_b, 0)),
                  pl.BlockSpec((None, 1, d), lambda i: (i // per_b, 0, 0)),
                  pl.BlockSpec((1, d), lambda i: (0, 0))],
        out_specs=pl.BlockSpec((None, tm, d), lambda i: (i // per_b, i % per_b, 0)),
        scratch_shapes=[pltpu.VMEM((TOP_K, tm, d // 2), u32), pltpu.SemaphoreType.DMA(())],
        compiler_params=_cparams(("arbitrary",), 40),
        name="combine",
    )(dest, yb, route, x1, g2, gain)


def _routing(route, n_blocks):
    experts = slice(MEMBER_LANE, MEMBER_LANE + N_EXPERTS)
    top_idx = route[:, :TOP_K].astype(jnp.int32)
    incl = _token_cumsum(route)[:, experts]
    rank = (incl - route[:, experts]).astype(jnp.int32)
    counts = incl[-1].astype(jnp.int32)
    padded = (counts + MOE_TM - 1) // MOE_TM * MOE_TM
    pad_end = jnp.cumsum(padded).astype(jnp.int32)
    slot0 = (pad_end - padded)[None, :] + rank
    onehot = top_idx[:, :, None] == jnp.arange(N_EXPERTS, dtype=jnp.int32)[None, None, :]
    dest = jnp.sum(jnp.where(onehot, slot0[:, None, :], 0), axis=-1).astype(jnp.int32)
    block_start = jnp.arange(n_blocks, dtype=jnp.int32) * MOE_TM
    block_exp = jnp.minimum(jnp.sum(pad_end[None, :] <= block_start[:, None], axis=1), N_EXPERTS - 1).astype(jnp.int32)
    first = jnp.concatenate([jnp.ones((1,), jnp.int32), (block_exp[1:] != block_exp[:-1]).astype(jnp.int32)])
    n_used = (pad_end[-1:] // MOE_TM).astype(jnp.int32)
    return dest.reshape(-1), pad_end, block_exp, first, n_used


def _moe(hp, route, x1, g2, norm_final, w_gu, b_gu, w_down, b_down):
    b, l, d = x1.shape
    t = b * l
    n_blocks = -(-(t * TOP_K + N_EXPERTS * (MOE_TM - 1)) // MOE_TM)
    dest, pad_end, block_exp, first, n_used = _routing(route.reshape(t, 128), n_blocks)
    xs = _dispatch(pad_end, dest, hp.reshape(t, d // 2), n_blocks * MOE_TM)
    act = _gemm1(block_exp, first, n_used, xs, w_gu, b_gu, 1024)
    yb = _gemm2(block_exp, first, n_used, act, w_down, b_down)
    return _combine(dest, yb, route, x1, g2, norm_final.reshape(1, d))


def kernel(x, c, ctx, c_ctx, w_ada, b_ada, norm_mix, norm_ffn, w_in, lb_table, hg_norm, rpb, w_out, w_router,
           b_router, w_gu, b_gu, w_down, b_down, norm_final):
    b, l, d = x.shape
    assert w_ada.shape[0] == 1, "single-layer block"
    rows = l // GRID_W

    c16 = jnp.zeros((16, d), f32).at[:b].set(c).at[b].set(c_ctx)
    mod = _ada(c16, w_ada[0], b_ada[0])
    sh1, sc1, g1, sh2, sc2, g2 = [mod[:b, i * d:(i + 1) * d].reshape(b, 1, d) for i in range(6)]
    csh = jnp.broadcast_to(mod[b, :d].reshape(1, 1, d), (b, 1, d))
    csc = jnp.broadcast_to(mod[b, d:2 * d].reshape(1, 1, d), (b, 1, d))

    w_in_bf = w_in[0].astype(bf16)
    gain_mix = norm_mix[0].reshape(1, d)
    px_att = _inproj(x, sh1, sc1, gain_mix, w_in_bf, P_QN, 3, bf16, 1024)
    px_hg = _inproj(x, sh1, sc1, gain_mix, w_in_bf, P_QH, 5, f32, 1024)
    pc_att = _inproj(ctx, csh, csc, gain_mix, w_in_bf, P_KN, 2, bf16, 256)
    pc_hg = _inproj(ctx, csh, csc, gain_mix, w_in_bf, P_FF, 3, f32, 256)

    o_na = _natten(px_att, pc_att, _natten_bias_table(rpb[0], rows))

    lower_bounds = jnp.cumsum(jax.nn.softmax(lb_table.astype(f32), axis=0), axis=0)
    o_hg = _hgrn(px_hg, pc_hg, lower_bounds[0].reshape(2, HEADS * HEAD_DIM), hg_norm[0])

    x1, hp, route = _outproj(o_na, o_hg, x, w_out[0].astype(bf16), g1, sh2, sc2, norm_ffn[0].reshape(1, d),
                             w_router[0], b_router[0], 256)
    return _moe(hp, route, x1, g2, norm_final, w_gu[0], b_gu[0], w_down[0], b_down[0])
```

```python
import functools

import numpy as np
import jax
import jax.numpy as jnp
from jax import lax
from jax.experimental import pallas as pl
from jax.experimental.pallas import tpu as pltpu

f32 = jnp.float32
bf16 = jnp.bfloat16
u32 = jnp.uint32

GRID_W = 64
HEADS = 8
HEAD_DIM = 128
WIN_R = 8
WIN_C = 16
ROPE_THETA = 10000.0
N_EXPERTS = 32
TOP_K = 4
SWIGLU_LIMIT = 7.0
SWIGLU_ALPHA = 1.702
EPS = 1e-6
P_QN, P_KN, P_VN, P_QH, P_FF, P_FB, P_IH, P_GH = range(8)

CHUNK = 64
SUB = 16
CHUNK_UNROLL = 4
ROW_UNROLL = 4
MOE_TM = 512
MOE_HALF = 256
NEG_BIG = -1e30
MEMBER_LANE = 32
HI_MASK = 0xFFFF0000

_HIGHEST = lax.Precision.HIGHEST


def _cparams(sem, vmem_mb=None):
    kw = dict(dimension_semantics=sem)
    if vmem_mb is not None:
        kw["vmem_limit_bytes"] = vmem_mb * 1024 * 1024
    return pltpu.CompilerParams(**kw)


def _ada_kernel(c_ref, w_ref, b_ref, o_ref):
    c = c_ref[...]
    cond = c * jax.nn.sigmoid(c)
    o_ref[...] = jnp.dot(cond, w_ref[...], precision=_HIGHEST, preferred_element_type=f32) + b_ref[...]


def _ada(c16, w_ada, b_ada):
    d, n = w_ada.shape
    tn = min(1024, n)
    return pl.pallas_call(
        _ada_kernel,
        out_shape=jax.ShapeDtypeStruct((c16.shape[0], n), f32),
        grid=(n // tn,),
        in_specs=[pl.BlockSpec((c16.shape[0], d), lambda j: (0, 0)),
                  pl.BlockSpec((d, tn), lambda j: (0, j)),
                  pl.BlockSpec((1, tn), lambda j: (0, j))],
        out_specs=pl.BlockSpec((c16.shape[0], tn), lambda j: (0, j)),
        compiler_params=_cparams(("parallel",), 40),
        name="ada",
    )(c16, w_ada, b_ada.reshape(1, n))


def _inproj_kernel(x_ref, shift_ref, scale_ref, gain_ref, w_ref, o_ref, h_scr):
    @pl.when(pl.program_id(2) == 0)
    def _():
        x = x_ref[...]
        ms = jnp.mean(x * x, axis=-1, keepdims=True)
        y = x * lax.rsqrt(ms + EPS) * gain_ref[...]
        h_scr[...] = (y * (1.0 + scale_ref[...]) + shift_ref[...]).astype(bf16)

    acc = jnp.dot(h_scr[...], w_ref[...], preferred_element_type=f32)
    for hh in range(HEADS):
        o_ref[hh] = acc[:, hh * HEAD_DIM:(hh + 1) * HEAD_DIM].astype(o_ref.dtype)


def _inproj(x, shift, scale, gain, w_bf, part_lo, n_parts, out_dtype, tm):
    b, l, d = x.shape
    pw = HEADS * HEAD_DIM
    tm = min(tm, l)
    return pl.pallas_call(
        _inproj_kernel,
        out_shape=jax.ShapeDtypeStruct((b, n_parts * HEADS, l, HEAD_DIM), out_dtype),
        grid=(b, l // tm, n_parts),
        in_specs=[pl.BlockSpec((None, tm, d), lambda bi, mi, ni: (bi, mi, 0)),
                  pl.BlockSpec((None, 1, d), lambda bi, mi, ni: (bi, 0, 0)),
                  pl.BlockSpec((None, 1, d), lambda bi, mi, ni: (bi, 0, 0)),
                  pl.BlockSpec((1, d), lambda bi, mi, ni: (0, 0)),
                  pl.BlockSpec((d, pw), lambda bi, mi, ni: (0, part_lo + ni))],
        out_specs=pl.BlockSpec((None, HEADS, tm, HEAD_DIM), lambda bi, mi, ni: (bi, ni, mi, 0)),
        scratch_shapes=[pltpu.VMEM((tm, d), bf16)],
        compiler_params=_cparams(("parallel", "parallel", "arbitrary"), 48),
        name="inproj",
    )(x, shift, scale, gain, w_bf)


def _natten_bias_table(rpb, rows):
    kr = min(WIN_R, rows)
    q = np.arange(GRID_W)
    col_start = np.clip(q - WIN_C // 2, 0, GRID_W - WIN_C)
    kc = np.arange(GRID_W)
    in_win = (kc[None, :] >= col_start[:, None]) & (kc[None, :] < col_start[:, None] + WIN_C)
    d_col = np.clip(kc[None, :] - q[:, None] + WIN_C - 1, 0, 2 * WIN_C - 2)
    n_d0 = 2 * WIN_R - 1 - (kr - 1)
    onehot = (d_col[None] == np.arange(2 * WIN_C - 1)[:, None, None]).astype(np.float32)
    cols = jnp.einsum("hrc,cqk->hrqk", rpb.astype(f32), jnp.asarray(onehot), precision=_HIGHEST)
    cols = jnp.where(in_win[None, None], cols, -jnp.inf)
    t = jnp.stack([cols[:, d0:d0 + kr] for d0 in range(n_d0)], axis=1)
    t = jnp.transpose(t, (0, 1, 3, 2, 4))
    return t.reshape(rpb.shape[0], n_d0, GRID_W, kr * GRID_W)


def _natten_kernel(q_ref, k_ref, v_ref, kc_ref, vc_ref, bias_ref, o_ref, sw0, sc0, sw1, sc1, *, rows, kr):
    scale = HEAD_DIM ** -0.5
    kc = kc_ref[...]
    vc = vc_ref[...]
    nt = (((1,), (1,)), ((), ()))

    n_groups = rows // ROW_UNROLL
    gq = ROW_UNROLL * GRID_W

    def key_start(r):
        return jnp.clip(r - kr // 2, 0, rows - kr)

    def scores(g, sw_scr, sc_scr):
        q_all = q_ref[pl.ds(pl.multiple_of(g * gq, gq), gq), :]
        sc_scr[...] = lax.dot_general(q_all, kc, nt, preferred_element_type=f32) * scale
        for u in range(ROW_UNROLL):
            r = g * ROW_UNROLL + u
            kr0 = key_start(r)
            kw = k_ref[pl.ds(pl.multiple_of(kr0 * GRID_W, GRID_W), kr * GRID_W), :]
            q = q_all[u * GRID_W:(u + 1) * GRID_W, :]
            sw_scr[u] = (lax.dot_general(q, kw, nt, preferred_element_type=f32) * scale
                         + bias_ref[kr0 - r + WIN_R - 1])

    def attend(g, sw_scr, sc_scr):
        s_c = sc_scr[...]
        s_w = [sw_scr[u] for u in range(ROW_UNROLL)]
        m = jnp.maximum(jnp.concatenate([jnp.max(s, axis=-1, keepdims=True) for s in s_w], axis=0),
                        jnp.max(s_c, axis=-1, keepdims=True))
        p_c = jnp.exp(s_c - m)
        p_w = [jnp.exp(s - m[u * GRID_W:(u + 1) * GRID_W, :]) for u, s in enumerate(s_w)]
        denom = (jnp.concatenate([jnp.sum(p, axis=-1, keepdims=True) for p in p_w], axis=0)
                 + jnp.sum(p_c, axis=-1, keepdims=True))
        o_c = jnp.dot(p_c.astype(bf16), vc, preferred_element_type=f32)
        o_w = []
        for u, p in enumerate(p_w):
            k0 = pl.multiple_of(key_start(g * ROW_UNROLL + u) * GRID_W, GRID_W)
            o_w.append(jnp.dot(p.astype(bf16), v_ref[pl.ds(k0, kr * GRID_W), :], preferred_element_type=f32))
        o = (jnp.concatenate(o_w, axis=0) + o_c) / denom
        o_ref[pl.ds(pl.multiple_of(g * gq, gq), gq), :] = o.astype(o_ref.dtype)

    scores(0, sw0, sc0)

    def body(i, carry):
        g = 2 * i
        scores(g + 1, sw1, sc1)
        attend(g, sw0, sc0)
        scores(jnp.minimum(g + 2, n_groups - 1), sw0, sc0)
        attend(g + 1, sw1, sc1)
        return carry

    lax.fori_loop(0, n_groups // 2, body, 0)


def _natten(px_att, pc_att, bias_tab):
    b, _, l, _ = px_att.shape
    lc = pc_att.shape[2]
    rows = l // GRID_W
    kr = min(WIN_R, rows)
    n_d0 = bias_tab.shape[1]
    blk = lambda off: pl.BlockSpec((None, None, l, HEAD_DIM), lambda h, bi: (bi, off + h, 0, 0))
    cblk = lambda off: pl.BlockSpec((None, None, lc, HEAD_DIM), lambda h, bi: (bi, off + h, 0, 0))
    return pl.pallas_call(
        functools.partial(_natten_kernel, rows=rows, kr=kr),
        out_shape=jax.ShapeDtypeStruct((b, l, HEADS * HEAD_DIM), bf16),
        grid=(HEADS, b),
        in_specs=[blk(0), blk(HEADS), blk(2 * HEADS), cblk(0), cblk(HEADS),
                  pl.BlockSpec((None, n_d0, GRID_W, kr * GRID_W), lambda h, bi: (h, 0, 0, 0))],
        out_specs=pl.BlockSpec((None, l, HEAD_DIM), lambda h, bi: (bi, 0, h)),
        scratch_shapes=[pltpu.VMEM((ROW_UNROLL, GRID_W, kr * GRID_W), f32), pltpu.VMEM((ROW_UNROLL * GRID_W, lc), f32),
                        pltpu.VMEM((ROW_UNROLL, GRID_W, kr * GRID_W), f32), pltpu.VMEM((ROW_UNROLL * GRID_W, lc), f32)],
        compiler_params=_cparams(("parallel", "parallel")),
        name="natten",
    )(px_att, px_att, px_att, pc_att, pc_att, bias_tab)


def _hgrn_consts():
    t = np.arange(CHUNK)
    bt, bs = t[:, None] // SUB, t[None, :] // SUB
    tri, masks = [], []
    for sgn in (1, -1):
        before = (t[None, :] <= t[:, None]) if sgn == 1 else (t[None, :] >= t[:, None])
        tri.append(before.astype(np.float32))
        dist = (bt - bs) * sgn
        masks.append(np.stack([dist == 1, dist == 2, dist == 3, (dist == 0) & before]).astype(np.float32))
    return np.stack(tri), np.stack(masks)


def _rope_tables(l):
    t = jnp.arange(l)
    n_freq = HEAD_DIM // 4
    inv_freq = ROPE_THETA ** (-jnp.arange(n_freq, dtype=f32) / n_freq)
    ang_row = (t // GRID_W).astype(f32)[:, None] * inv_freq
    ang_col = (t % GRID_W).astype(f32)[:, None] * inv_freq
    cr, sr, cc, sc = jnp.cos(ang_row), jnp.sin(ang_row), jnp.cos(ang_col), jnp.sin(ang_col)
    z = jnp.zeros_like(sr)
    cos_t = jnp.concatenate([cr, cr, cc, cc], axis=-1)
    sin_up = jnp.concatenate([-sr, z, -sc, z], axis=-1)
    sin_dn = jnp.concatenate([z, sr, z, sc], axis=-1)
    return cos_t, sin_up, sin_dn


def _split3_dot(tri_bf, g):
    g1 = g.astype(bf16)
    r1 = g - g1.astype(f32)
    g2 = r1.astype(bf16)
    g3 = (r1 - g2.astype(f32)).astype(bf16)
    dot = lambda a: jnp.dot(tri_bf, a, preferred_element_type=f32)
    return dot(g1) + dot(g2) + dot(g3)


def _gate(f_raw, lb):
    log_f = jnp.log(lb + (1.0 - lb) * jax.nn.sigmoid(f_raw))
    key = (1.0 - lb) * jax.nn.sigmoid(-f_raw)
    return log_f, key


def _chunk_refs(cum, backward):
    nb = CHUNK // SUB
    if backward:
        ends = [cum[i * SUB:i * SUB + 1, :] for i in range(nb)]
        order = list(range(nb - 1, -1, -1))
    else:
        ends = [cum[i * SUB + SUB - 1:i * SUB + SUB, :] for i in range(nb)]
        order = list(range(nb))
    zero = jnp.zeros_like(ends[0])
    b_rows, g2_rows, g3_rows = [None] * nb, [None] * nb, [None] * nb
    for pos, i in enumerate(order):
        b_i = zero if pos == 0 else ends[order[pos - 1]]
        b_rows[i] = b_i
        g2_rows[i] = b_i - ends[order[pos - 2]] if pos >= 2 else zero
        g3_rows[i] = b_i - ends[order[pos - 3]] if pos >= 3 else zero
    expand = lambda rws: jnp.concatenate([jnp.broadcast_to(r, (SUB, HEAD_DIM)) for r in rws], axis=0)
    total = ends[order[-1]]
    return expand(b_rows), expand(ends), expand(g2_rows), expand(g3_rows), total


def _rope(x, cos_t, sin_up, sin_dn):
    return x * cos_t + pltpu.roll(x, 96, 1) * sin_up + pltpu.roll(x, 32, 1) * sin_dn


def _scan_group(states, items, tris, masks):
    nt = (((1,), (1,)), ((), ()))
    tn = (((0,), (0,)), ((), ()))
    pre = []
    for d, q, f_raw, v, lb, rope in items:
        g, key = _gate(f_raw, lb)
        pre.append((g, _rope(q, *rope), _rope(key, *rope), v.astype(bf16)))
    cums = [_split3_dot(tris[it[0]], p[0]) for it, p in zip(items, pre)]
    ops = []
    for it, (g, qr, kr, v_bf), cum in zip(items, pre, cums):
        b, e, gap2, gap3, total = _chunk_refs(cum, it[0] == 1)
        q_t = qr * jnp.exp(cum - b)
        k_hat = kr * jnp.exp(e - cum)
        k_til = kr * jnp.exp(b - cum)
        lhs = jnp.concatenate([q_t, q_t * jnp.exp(gap2), q_t * jnp.exp(gap3)], axis=0).astype(bf16)
        rhs = jnp.concatenate([k_hat, k_til], axis=0).astype(bf16)
        q_in = (q_t * jnp.exp(b)).astype(bf16)
        k_dec = (k_hat * jnp.exp(total - e)).astype(bf16)
        ops.append((lhs, rhs, q_in, k_dec, jnp.exp(total)))
    scores = [lax.dot_general(o[0], o[1], nt, preferred_element_type=f32) for o in ops]
    upds = [lax.dot_general(p[3], o[3], tn, preferred_element_type=f32) for p, o in zip(pre, ops)]
    intra = []
    for it, p, (g, qr, kr, v_bf) in zip(items, scores, pre):
        m = masks[it[0]]
        att = (jnp.where(m[0] > 0, p[0:CHUNK, 0:CHUNK], 0.0)
               + jnp.where(m[1] > 0, p[CHUNK:2 * CHUNK, 0:CHUNK], 0.0)
               + jnp.where(m[2] > 0, p[2 * CHUNK:3 * CHUNK, 0:CHUNK], 0.0)
               + jnp.where(m[3] > 0, p[0:CHUNK, CHUNK:2 * CHUNK], 0.0))
        intra.append(jnp.dot(att.astype(bf16), v_bf, preferred_element_type=f32))
    states = list(states)
    outs = []
    for it, o, upd, o_in in zip(items, ops, upds, intra):
        st = states[it[0]]
        outs.append(o_in + lax.dot_general(o[2], st.astype(bf16), nt, preferred_element_type=f32))
        states[it[0]] = st * o[4] + upd
    return states, outs


def _hgrn_kernel(q_ref, ff_ref, fb_ref, v_ref, gate_ref, cff_ref, cfb_ref, cv_ref, lb_ref, gain_ref,
                 cos_ref, sup_ref, sdn_ref, tri_ref, mask_ref, o_ref, of_scr, ob_scr, *, n_chunks, n_cchunks):
    lb_f = lb_ref[0]
    lb_b = lb_ref[1]
    tri_f = tri_ref[0].astype(bf16)
    tri_b = tri_ref[1].astype(bf16)
    zero = jnp.zeros((HEAD_DIM, HEAD_DIM), f32)

    ctx_items = []
    for c in range(n_cchunks):
        rf = pl.ds(c * CHUNK, CHUNK)
        rb = pl.ds((n_cchunks - 1 - c) * CHUNK, CHUNK)
        ctx_items.append((0, cv_ref[rf, :], cff_ref[rf, :], lb_f))
        ctx_items.append((1, cv_ref[rb, :], cfb_ref[rb, :], lb_b))
    st_f, st_b = _state_group([zero, zero], ctx_items, (tri_f, tri_b))

    def body(c, carry):
        masks = [[mask_ref[d, i] for i in range(4)] for d in range(2)]
        items, rows = [], []
        for u in range(CHUNK_UNROLL):
            cf = c * CHUNK_UNROLL + u
            rf = pl.ds(pl.multiple_of(cf * CHUNK, CHUNK), CHUNK)
            rb = pl.ds(pl.multiple_of((n_chunks - 1 - cf) * CHUNK, CHUNK), CHUNK)
            items.append((0, q_ref[rf, :], ff_ref[rf, :], v_ref[rf, :], lb_f,
                          (cos_ref[rf, :], sup_ref[rf, :], sdn_ref[rf, :])))
            items.append((1, q_ref[rb, :], fb_ref[rb, :], v_ref[rb, :], lb_b,
                          (cos_ref[rb, :], sup_ref[rb, :], sdn_ref[rb, :])))
            rows += [(of_scr, rf), (ob_scr, rb)]
        states, outs = _scan_group(carry, items, (tri_f, tri_b), masks)
        for (scr, rws), o in zip(rows, outs):
            scr[rws, :] = o
        return tuple(states)

    lax.fori_loop(0, n_chunks // CHUNK_UNROLL, body, (st_f, st_b))

    o = of_scr[...] + ob_scr[...]
    y = o * lax.rsqrt(jnp.mean(o * o, axis=-1, keepdims=True) + EPS) * gain_ref[...]
    gate = gate_ref[...]
    o_ref[...] = (y * (gate * jax.nn.sigmoid(gate))).astype(o_ref.dtype)


def _state_group(states, items, tris):
    gates = [_gate(f_raw, lb) for _, _, f_raw, lb in items]
    cums = [_split3_dot(tris[it[0]], g) for it, (g, _) in zip(items, gates)]
    terms = []
    for it, (_, key), cum in zip(items, gates, cums):
        total = cum[0:1, :] if it[0] == 1 else cum[CHUNK - 1:CHUNK, :]
        terms.append(((key * jnp.exp(total - cum)).astype(bf16), jnp.exp(total)))
    upds = [lax.dot_general(it[1].astype(bf16), k_dec, (((0,), (0,)), ((), ())), preferred_element_type=f32)
            for it, (k_dec, _) in zip(items, terms)]
    states = list(states)
    for it, (_, decay), upd in zip(items, terms, upds):
        states[it[0]] = states[it[0]] * decay + upd
    return states


def _hgrn(px_hg, pc_hg, lb2, hg_gain):
    b, _, l, _ = px_hg.shape
    lc = pc_hg.shape[2]
    cos_t, sin_up, sin_dn = _rope_tables(l)
    tri, masks = _hgrn_consts()
    blk = lambda off: pl.BlockSpec((None, None, l, HEAD_DIM), lambda h, bi: (bi, off + h, 0, 0))
    cblk = lambda off: pl.BlockSpec((None, None, lc, HEAD_DIM), lambda h, bi: (bi, off + h, 0, 0))
    full = lambda shp: pl.BlockSpec(shp, lambda h, bi: (0,) * len(shp))
    return pl.pallas_call(
        functools.partial(_hgrn_kernel, n_chunks=l // CHUNK, n_cchunks=lc // CHUNK),
        out_shape=jax.ShapeDtypeStruct((b, l, HEADS * HEAD_DIM), bf16),
        grid=(HEADS, b),
        in_specs=[blk(0), blk(HEADS), blk(2 * HEADS), blk(3 * HEADS), blk(4 * HEADS),
                  cblk(0), cblk(HEADS), cblk(2 * HEADS),
                  pl.BlockSpec((2, None, 1, HEAD_DIM), lambda h, bi: (0, h, 0, 0)),
                  full((1, HEAD_DIM)),
                  full((l, HEAD_DIM)), full((l, HEAD_DIM)), full((l, HEAD_DIM)),
                  full((2, CHUNK, CHUNK)), full((2, 4, CHUNK, CHUNK))],
        out_specs=pl.BlockSpec((None, l, HEAD_DIM), lambda h, bi: (bi, 0, h)),
        scratch_shapes=[pltpu.VMEM((l, HEAD_DIM), f32), pltpu.VMEM((l, HEAD_DIM), f32)],
        compiler_params=_cparams(("parallel", "parallel")),
        name="hgrn",
    )(px_hg, px_hg, px_hg, px_hg, px_hg, pc_hg, pc_hg, pc_hg,
      lb2.reshape(2, HEADS, 1, HEAD_DIM), hg_gain.reshape(1, HEAD_DIM),
      cos_t, sin_up, sin_dn, jnp.asarray(tri), jnp.asarray(masks))


def _pack_halves(h):
    k = h.shape[-1] // 2
    lo = lax.bitcast_convert_type(h[:, :k].astype(f32), u32)
    hi = lax.bitcast_convert_type(h[:, k:].astype(f32), u32)
    return (lo >> 16) | (hi & u32(HI_MASK))


def _unpack_halves(u):
    lo = lax.bitcast_convert_type(u << 16, f32)
    hi = lax.bitcast_convert_type(u & u32(HI_MASK), f32)
    return lo, hi


def _outproj_kernel(ana_ref, ahg_ref, x_ref, w0_ref, w1_ref, g1_ref, sh_ref, sc_ref, gain_ref, wr_ref, br_ref,
                    x1_ref, hp_ref, route_ref):
    y = (jnp.dot(ana_ref[...], w0_ref[...], preferred_element_type=f32)
         + jnp.dot(ahg_ref[...], w1_ref[...], preferred_element_type=f32))
    x1 = x_ref[...] + g1_ref[...] * y
    x1_ref[...] = x1
    ms = jnp.mean(x1 * x1, axis=-1, keepdims=True)
    h = x1 * lax.rsqrt(ms + EPS) * gain_ref[...]
    h = h * (1.0 + sc_ref[...]) + sh_ref[...]
    h_hi = h.astype(bf16)
    hp_ref[...] = _pack_halves(h_hi)
    h_lo = (h - h_hi.astype(f32)).astype(bf16)
    t = jnp.dot(h_hi, wr_ref[...], preferred_element_type=f32)
    logits = (t[:, :128] + t[:, 128:] + jnp.dot(h_lo, wr_ref[:, :128], preferred_element_type=f32)) + br_ref[...]
    lane = lax.broadcasted_iota(jnp.int32, logits.shape, 1).astype(f32)
    cur = logits
    vals, idxs = [], []
    for _ in range(TOP_K):
        m = jnp.max(cur, axis=-1, keepdims=True)
        i = jnp.min(jnp.where(cur == m, lane, float(logits.shape[-1])), axis=-1, keepdims=True)
        vals.append(m)
        idxs.append(i)
        cur = jnp.where(lane == i, -jnp.inf, cur)
    es = [jnp.exp(v - vals[0]) for v in vals]
    denom = es[0] + es[1] + es[2] + es[3]
    route = jnp.zeros(logits.shape, f32)
    for k in range(TOP_K):
        route = jnp.where(lane == k, idxs[k], route)
        route = jnp.where(lane == TOP_K + k, es[k] / denom, route)
        route = jnp.where(lane - float(MEMBER_LANE) == idxs[k], 1.0, route)
    route_ref[...] = route


def _outproj(a_na, a_hg, x, w_out_bf, g1, sh2, sc2, gain, w_router, b_router, tm):
    b, l, d = x.shape
    hw = a_na.shape[-1]
    tm = min(tm, l)
    n_e = w_router.shape[-1]
    wr = jnp.zeros((d, 128), f32).at[:, :n_e].set(w_router)
    wr_hi = wr.astype(bf16)
    wr = jnp.concatenate([wr_hi, (wr - wr_hi.astype(f32)).astype(bf16)], axis=1)
    br = jnp.full((1, 128), NEG_BIG, f32).at[0, :n_e].set(b_router)
    row = lambda last: pl.BlockSpec((None, tm, last), lambda bi, mi: (bi, mi, 0))
    vec = pl.BlockSpec((None, 1, d), lambda bi, mi: (bi, 0, 0))
    return pl.pallas_call(
        _outproj_kernel,
        out_shape=(jax.ShapeDtypeStruct((b, l, d), f32),
                   jax.ShapeDtypeStruct((b, l, d // 2), u32),
                   jax.ShapeDtypeStruct((b, l, 128), f32)),
        grid=(b, l // tm),
        in_specs=[row(hw), row(hw), row(d),
                  pl.BlockSpec((hw, d), lambda bi, mi: (0, 0)),
                  pl.BlockSpec((hw, d), lambda bi, mi: (1, 0)),
                  vec, vec, vec,
                  pl.BlockSpec((1, d), lambda bi, mi: (0, 0)),
                  pl.BlockSpec((d, 256), lambda bi, mi: (0, 0)),
                  pl.BlockSpec((1, 128), lambda bi, mi: (0, 0))],
        out_specs=(row(d), row(d // 2), row(128)),
        compiler_params=_cparams(("parallel", "parallel"), 48),
        name="outproj",
    )(a_na, a_hg, x, w_out_bf, w_out_bf, g1, sh2, sc2, gain, wr, br)


CUMSUM_TM = 512


def _cumsum_kernel(route_ref, tri_ref, o_ref, carry):
    @pl.when(pl.program_id(0) == 0)
    def _():
        carry[...] = jnp.zeros_like(carry)

    r = route_ref[...]
    lane = lax.broadcasted_iota(jnp.int32, r.shape, 1)
    member = jnp.where((lane >= MEMBER_LANE) & (lane < MEMBER_LANE + N_EXPERTS), r, 0.0).astype(bf16)
    incl = jnp.dot(tri_ref[...], member, preferred_element_type=f32) + carry[...]
    o_ref[...] = incl
    carry[...] = incl[CUMSUM_TM - 1:CUMSUM_TM, :]


def _token_cumsum(route):
    t = route.shape[0]
    tri = jnp.asarray(np.tril(np.ones((CUMSUM_TM, CUMSUM_TM), np.float32)), bf16)
    return pl.pallas_call(
        _cumsum_kernel,
        out_shape=jax.ShapeDtypeStruct((t, 128), f32),
        grid=(t // CUMSUM_TM,),
        in_specs=[pl.BlockSpec((CUMSUM_TM, 128), lambda i: (i, 0)),
                  pl.BlockSpec((CUMSUM_TM, CUMSUM_TM), lambda i: (0, 0))],
        out_specs=pl.BlockSpec((CUMSUM_TM, 128), lambda i: (i, 0)),
        scratch_shapes=[pltpu.VMEM((1, 128), f32)],
        compiler_params=_cparams(("arbitrary",)),
        name="cumsum",
    )(route, tri)


DISPATCH_TM = 256
TOK_UNROLL = 2


def _dispatch_kernel(pe_ref, dest_ref, hp_ref, xs_ref, zero_scr, sem):
    @pl.when(pl.program_id(0) == 0)
    def _():
        zero_scr[...] = jnp.zeros_like(zero_scr)

        def tail_copy(e):
            start = pl.multiple_of(pe_ref[e] - MOE_TM, MOE_TM)
            return pltpu.make_async_copy(zero_scr, xs_ref.at[pl.ds(start, MOE_TM), :], sem)

        def has_rows(e):
            return pe_ref[e] > jnp.where(e == 0, 0, pe_ref[jnp.maximum(e - 1, 0)])

        def issue_tail(e, c):
            @pl.when(has_rows(e))
            def _():
                tail_copy(e).start()
            return c

        def drain_tail(e, c):
            @pl.when(has_rows(e))
            def _():
                tail_copy(e).wait()
            return c

        lax.fori_loop(0, N_EXPERTS, issue_tail, 0)
        lax.fori_loop(0, N_EXPERTS, drain_tail, 0)

        def dead_copy(m):
            return pltpu.make_async_copy(zero_scr, xs_ref.at[pl.ds(pl.multiple_of(m * MOE_TM, MOE_TM), MOE_TM), :], sem)

        def issue_dead(m, c):
            dead_copy(m).start()
            return c

        def drain_dead(m, c):
            dead_copy(m).wait()
            return c

        n_used = pe_ref[N_EXPERTS - 1] // MOE_TM
        lax.fori_loop(n_used, xs_ref.shape[0] // MOE_TM, issue_dead, 0)
        lax.fori_loop(n_used, xs_ref.shape[0] // MOE_TM, drain_dead, 0)

    def row_copy(t, k):
        return pltpu.make_async_copy(hp_ref.at[pl.ds(t, 1), :], xs_ref.at[pl.ds(dest_ref[t * TOP_K + k], 1), :], sem)

    def issue(i, c):
        for u in range(TOK_UNROLL):
            for k in range(TOP_K):
                row_copy(i * TOK_UNROLL + u, k).start()
        return c

    def drain(i, c):
        for u in range(TOK_UNROLL):
            for k in range(TOP_K):
                row_copy(i * TOK_UNROLL + u, k).wait()
        return c

    lax.fori_loop(0, DISPATCH_TM // TOK_UNROLL, issue, 0)
    lax.fori_loop(0, DISPATCH_TM // TOK_UNROLL, drain, 0)


def _dispatch(pad_end, dest, hp, cap):
    t, kp = hp.shape
    return pl.pallas_call(
        _dispatch_kernel,
        out_shape=jax.ShapeDtypeStruct((cap, kp), hp.dtype),
        grid_spec=pltpu.PrefetchScalarGridSpec(
            num_scalar_prefetch=1,
            grid=(t // DISPATCH_TM,),
            in_specs=[pl.BlockSpec((DISPATCH_TM * TOP_K,), lambda i, pe: (i,), memory_space=pltpu.SMEM),
                      pl.BlockSpec((DISPATCH_TM, kp), lambda i, pe: (i, 0))],
            out_specs=pl.BlockSpec(memory_space=pl.ANY),
            scratch_shapes=[pltpu.VMEM((MOE_TM, kp), hp.dtype), pltpu.SemaphoreType.DMA(())]),
        compiler_params=_cparams(("arbitrary",)),
        name="dispatch",
    )(pad_end, dest, hp)


def _block_halves(m, half2_ref, nu_ref, out_ref, prepare, half):
    live = m < nu_ref[0]
    second = jnp.logical_and(live, half2_ref[m] == 1)
    zeros = jnp.zeros((MOE_HALF, out_ref.shape[1]), out_ref.dtype)

    @pl.when(live)
    def _():
        prepare()
        half(pl.ds(0, MOE_HALF))

    @pl.when(second)
    def _():
        half(pl.ds(MOE_HALF, MOE_HALF))

    @pl.when(jnp.logical_not(live))
    def _():
        out_ref[pl.ds(0, MOE_HALF), :] = zeros

    @pl.when(jnp.logical_not(second))
    def _():
        out_ref[pl.ds(MOE_HALF, MOE_HALF), :] = zeros


def _gemm1_kernel(be_ref, first_ref, half2_ref, nu_ref, xs_ref, wg_ref, wl_ref, bg_ref, bl_ref, act_ref, wg_bf, wl_bf):
    m = pl.program_id(1)

    def prepare():
        @pl.when(first_ref[m] == 1)
        def _():
            wg_bf[...] = wg_ref[...].astype(bf16)
            wl_bf[...] = wl_ref[...].astype(bf16)

    def half(rows):
        lo, hi = _unpack_halves(xs_ref[rows, :])
        lo = lo.astype(bf16)
        hi = hi.astype(bf16)
        k = lo.shape[-1]
        gate = (jnp.dot(lo, wg_bf[:k, :], preferred_element_type=f32)
                + jnp.dot(hi, wg_bf[k:, :], preferred_element_type=f32) + bg_ref[...])
        lin = (jnp.dot(lo, wl_bf[:k, :], preferred_element_type=f32)
               + jnp.dot(hi, wl_bf[k:, :], preferred_element_type=f32) + bl_ref[...])
        x_glu = jnp.minimum(gate, SWIGLU_LIMIT)
        x_lin = jnp.clip(lin, -SWIGLU_LIMIT, SWIGLU_LIMIT)
        act_ref[rows, :] = (x_glu * jax.nn.sigmoid(SWIGLU_ALPHA * x_glu) * (x_lin + 1.0)).astype(act_ref.dtype)

    _block_halves(m, half2_ref, nu_ref, act_ref, prepare, half)


def _gemm1(block_exp, first, half2, n_used, xs, w_gu, b_gu, tn):
    cap, kp = xs.shape
    n_e, d, f2 = w_gu.shape
    ff = f2 // 2
    tn = min(tn, ff)
    nb = cap // MOE_TM
    nj = ff // tn
    live = lambda m, nu: jnp.minimum(m, nu[0] - 1)
    return pl.pallas_call(
        _gemm1_kernel,
        out_shape=jax.ShapeDtypeStruct((cap, ff), bf16),
        grid_spec=pltpu.PrefetchScalarGridSpec(
            num_scalar_prefetch=4,
            grid=(nj, nb),
            in_specs=[pl.BlockSpec((MOE_TM, kp), lambda j, m, be, fi, h2, nu: (live(m, nu), 0)),
                      pl.BlockSpec((None, d, tn), lambda j, m, be, fi, h2, nu: (be[m], 0, j)),
                      pl.BlockSpec((None, d, tn), lambda j, m, be, fi, h2, nu: (be[m], 0, nj + j)),
                      pl.BlockSpec((None, 1, tn), lambda j, m, be, fi, h2, nu: (be[m], 0, j)),
                      pl.BlockSpec((None, 1, tn), lambda j, m, be, fi, h2, nu: (be[m], 0, nj + j))],
            out_specs=pl.BlockSpec((MOE_TM, tn), lambda j, m, be, fi, h2, nu: (m, j)),
            scratch_shapes=[pltpu.VMEM((d, tn), bf16), pltpu.VMEM((d, tn), bf16)]),
        compiler_params=_cparams(("arbitrary", "arbitrary"), 56),
        name="gemm1",
    )(block_exp, first, half2, n_used, xs, w_gu, w_gu, b_gu.reshape(n_e, 1, f2), b_gu.reshape(n_e, 1, f2))


def _gemm2_kernel(be_ref, first_ref, half2_ref, nu_ref, act_ref, w_ref, b_ref, y_ref, w_bf):
    m = pl.program_id(0)

    def prepare():
        @pl.when(first_ref[m] == 1)
        def _():
            w_bf[...] = w_ref[...].astype(bf16)

    def half(rows):
        y = jnp.dot(act_ref[rows, :], w_bf[...], preferred_element_type=f32) + b_ref[...]
        y_ref[rows, :] = _pack_halves(y.astype(bf16))

    _block_halves(m, half2_ref, nu_ref, y_ref, prepare, half)


def _gemm2(block_exp, first, half2, n_used, act, w_down, b_down):
    cap, ff = act.shape
    n_e, _, d = w_down.shape
    nb = cap // MOE_TM
    live = lambda m, nu: jnp.minimum(m, nu[0] - 1)
    return pl.pallas_call(
        _gemm2_kernel,
        out_shape=jax.ShapeDtypeStruct((cap, d // 2), u32),
        grid_spec=pltpu.PrefetchScalarGridSpec(
            num_scalar_prefetch=4,
            grid=(nb,),
            in_specs=[pl.BlockSpec((MOE_TM, ff), lambda m, be, fi, h2, nu: (live(m, nu), 0)),
                      pl.BlockSpec((None, ff, d), lambda m, be, fi, h2, nu: (be[m], 0, 0)),
                      pl.BlockSpec((None, 1, d), lambda m, be, fi, h2, nu: (be[m], 0, 0))],
            out_specs=pl.BlockSpec((MOE_TM, d // 2), lambda m, be, fi, h2, nu: (m, 0)),
            scratch_shapes=[pltpu.VMEM((ff, d), bf16)]),
        compiler_params=_cparams(("arbitrary",), 56),
        name="gemm2",
    )(block_exp, first, half2, n_used, act, w_down, b_down.reshape(n_e, 1, d))


COMBINE_TM = 256


def _combine_kernel(dest_ref, yb_ref, route_ref, x1_ref, g2_ref, gain_ref, o_ref, buf, sem):
    def row_copy(t, k):
        return pltpu.make_async_copy(yb_ref.at[pl.ds(dest_ref[t * TOP_K + k], 1), :], buf.at[k, pl.ds(t, 1), :], sem)

    def issue(i, c):
        for u in range(TOK_UNROLL):
            for k in range(TOP_K):
                row_copy(i * TOK_UNROLL + u, k).start()
        return c

    def drain(i, c):
        for u in range(TOK_UNROLL):
            for k in range(TOP_K):
                row_copy(i * TOK_UNROLL + u, k).wait()
        return c

    lax.fori_loop(0, COMBINE_TM // TOK_UNROLL, issue, 0)
    lax.fori_loop(0, COMBINE_TM // TOK_UNROLL, drain, 0)

    route = route_ref[...]
    acc = None
    for k in range(TOP_K):
        lo, hi = _unpack_halves(buf[k])
        term = route[:, TOP_K + k:TOP_K + k + 1] * jnp.concatenate([lo, hi], axis=-1)
        acc = term if acc is None else acc + term
    x2 = x1_ref[...] + g2_ref[...] * acc
    ms = jnp.mean(x2 * x2, axis=-1, keepdims=True)
    o_ref[...] = x2 * lax.rsqrt(ms + EPS) * gain_ref[...]


def _combine(dest, yb, route, x1, g2, gain):
    b, l, d = x1.shape
    tm = min(COMBINE_TM, l)
    assert tm == COMBINE_TM
    per_b = l // tm
    return pl.pallas_call(
        _combine_kernel,
        out_shape=jax.ShapeDtypeStruct((b, l, d), f32),
        grid=(b * per_b,),
        in_specs=[pl.BlockSpec((tm * TOP_K,), lambda i: (i,), memory_space=pltpu.SMEM),
                  pl.BlockSpec(memory_space=pl.ANY),
                  pl.BlockSpec((None, tm, 128), lambda i: (i // per_b, i % per_b, 0)),
                  pl.BlockSpec((None, tm, d), lambda i: (i // per_b, i % per_b, 0)),
                  pl.BlockSpec((None, 1, d), lambda i: (i // per_b, 0, 0)),
                  pl.BlockSpec((1, d), lambda i: (0, 0))],
        out_specs=pl.BlockSpec((None, tm, d), lambda i: (i // per_b, i % per_b, 0)),
        scratch_shapes=[pltpu.VMEM((TOP_K, tm, d // 2), u32), pltpu.SemaphoreType.DMA(())],
        compiler_params=_cparams(("arbitrary",), 40),
        name="combine",
    )(dest, yb, route, x1, g2, gain)


def _routing(route, n_blocks):
    experts = slice(MEMBER_LANE, MEMBER_LANE + N_EXPERTS)
    top_idx = route[:, :TOP_K].astype(jnp.int32)
    incl = _token_cumsum(route)[:, experts]
    rank = (incl - route[:, experts]).astype(jnp.int32)
    counts = incl[-1].astype(jnp.int32)
    padded = (counts + MOE_TM - 1) // MOE_TM * MOE_TM
    pad_end = jnp.cumsum(padded).astype(jnp.int32)
    slot0 = (pad_end - padded)[None, :] + rank
    onehot = top_idx[:, :, None] == jnp.arange(N_EXPERTS, dtype=jnp.int32)[None, None, :]
    dest = jnp.sum(jnp.where(onehot, slot0[:, None, :], 0), axis=-1).astype(jnp.int32)
    block_start = jnp.arange(n_blocks, dtype=jnp.int32) * MOE_TM
    block_exp = jnp.minimum(jnp.sum(pad_end[None, :] <= block_start[:, None], axis=1), N_EXPERTS - 1).astype(jnp.int32)
    first = jnp.concatenate([jnp.ones((1,), jnp.int32), (block_exp[1:] != block_exp[:-1]).astype(jnp.int32)])
    n_used = (pad_end[-1:] // MOE_TM).astype(jnp.int32)
    token_end = (pad_end - padded + counts)[block_exp]
    half2 = (token_end > block_start + MOE_HALF).astype(jnp.int32)
    return dest.reshape(-1), pad_end, block_exp, first, half2, n_used


def _moe(hp, route, x1, g2, norm_final, w_gu, b_gu, w_down, b_down):
    b, l, d = x1.shape
    t = b * l
    n_blocks = -(-(t * TOP_K + N_EXPERTS * (MOE_TM - 1)) // MOE_TM)
    dest, pad_end, block_exp, first, half2, n_used = _routing(route.reshape(t, 128), n_blocks)
    xs = _dispatch(pad_end, dest, hp.reshape(t, d // 2), n_blocks * MOE_TM)
    act = _gemm1(block_exp, first, half2, n_used, xs, w_gu, b_gu, 1024)
    yb = _gemm2(block_exp, first, half2, n_used, act, w_down, b_down)
    return _combine(dest, yb, route, x1, g2, norm_final.reshape(1, d))


def kernel(x, c, ctx, c_ctx, w_ada, b_ada, norm_mix, norm_ffn, w_in, lb_table, hg_norm, rpb, w_out, w_router,
           b_router, w_gu, b_gu, w_down, b_down, norm_final):
    b, l, d = x.shape
    assert w_ada.shape[0] == 1, "single-layer block"
    rows = l // GRID_W

    c16 = jnp.zeros((16, d), f32).at[:b].set(c).at[b].set(c_ctx)
    mod = _ada(c16, w_ada[0], b_ada[0])
    sh1, sc1, g1, sh2, sc2, g2 = [mod[:b, i * d:(i + 1) * d].reshape(b, 1, d) for i in range(6)]
    csh = jnp.broadcast_to(mod[b, :d].reshape(1, 1, d), (b, 1, d))
    csc = jnp.broadcast_to(mod[b, d:2 * d].reshape(1, 1, d), (b, 1, d))

    w_in_bf = w_in[0].astype(bf16)
    gain_mix = norm_mix[0].reshape(1, d)
    px_att = _inproj(x, sh1, sc1, gain_mix, w_in_bf, P_QN, 3, bf16, 1024)
    px_hg = _inproj(x, sh1, sc1, gain_mix, w_in_bf, P_QH, 5, f32, 1024)
    pc_att = _inproj(ctx, csh, csc, gain_mix, w_in_bf, P_KN, 2, bf16, 256)
    pc_hg = _inproj(ctx, csh, csc, gain_mix, w_in_bf, P_FF, 3, f32, 256)

    o_na = _natten(px_att, pc_att, _natten_bias_table(rpb[0], rows))

    lower_bounds = jnp.cumsum(jax.nn.softmax(lb_table.astype(f32), axis=0), axis=0)
    o_hg = _hgrn(px_hg, pc_hg, lower_bounds[0].reshape(2, HEADS * HEAD_DIM), hg_norm[0])

    x1, hp, route = _outproj(o_na, o_hg, x, w_out[0].astype(bf16), g1, sh2, sc2, norm_ffn[0].reshape(1, d),
                             w_router[0], b_router[0], 256)
    return _moe(hp, route, x1, g2, norm_final, w_gu[0], b_gu[0], w_down[0], b_down[0])
```

```python
import functools

import numpy as np
import jax
import jax.numpy as jnp
from jax import lax
from jax.experimental import pallas as pl
from jax.experimental.pallas import tpu as pltpu

f32 = jnp.float32
bf16 = jnp.bfloat16
u32 = jnp.uint32

GRID_W = 64
HEADS = 8
HEAD_DIM = 128
WIN_R = 8
WIN_C = 16
ROPE_THETA = 10000.0
N_EXPERTS = 32
TOP_K = 4
SWIGLU_LIMIT = 7.0
SWIGLU_ALPHA = 1.702
EPS = 1e-6
P_QN, P_KN, P_VN, P_QH, P_FF, P_FB, P_IH, P_GH = range(8)

CHUNK = 64
SUB = 16
CHUNK_UNROLL = 4
ROW_UNROLL = 4
MOE_TM = 512
MOE_HALF = 256
W_STREAMS = 4
NEG_BIG = -1e30
MEMBER_LANE = 32
HI_MASK = 0xFFFF0000

_HIGHEST = lax.Precision.HIGHEST


def _cparams(sem, vmem_mb=None):
    kw = dict(dimension_semantics=sem)
    if vmem_mb is not None:
        kw["vmem_limit_bytes"] = vmem_mb * 1024 * 1024
    return pltpu.CompilerParams(**kw)


def _ada_kernel(c_ref, w_ref, b_ref, o_ref):
    c = c_ref[...]
    cond = c * jax.nn.sigmoid(c)
    o_ref[...] = jnp.dot(cond, w_ref[...], precision=_HIGHEST, preferred_element_type=f32) + b_ref[...]


def _ada(c16, w_ada, b_ada):
    d, n = w_ada.shape
    tn = min(1024, n)
    return pl.pallas_call(
        _ada_kernel,
        out_shape=jax.ShapeDtypeStruct((c16.shape[0], n), f32),
        grid=(n // tn,),
        in_specs=[pl.BlockSpec((c16.shape[0], d), lambda j: (0, 0)),
                  pl.BlockSpec((d, tn), lambda j: (0, j)),
                  pl.BlockSpec((1, tn), lambda j: (0, j))],
        out_specs=pl.BlockSpec((c16.shape[0], tn), lambda j: (0, j)),
        compiler_params=_cparams(("parallel",), 40),
        name="ada",
    )(c16, w_ada, b_ada.reshape(1, n))


def _inproj_kernel(x_ref, shift_ref, scale_ref, gain_ref, w_ref, o_ref, h_scr):
    @pl.when(pl.program_id(2) == 0)
    def _():
        x = x_ref[...]
        ms = jnp.mean(x * x, axis=-1, keepdims=True)
        y = x * lax.rsqrt(ms + EPS) * gain_ref[...]
        h_scr[...] = (y * (1.0 + scale_ref[...]) + shift_ref[...]).astype(bf16)

    acc = jnp.dot(h_scr[...], w_ref[...], preferred_element_type=f32)
    for hh in range(HEADS):
        o_ref[hh] = acc[:, hh * HEAD_DIM:(hh + 1) * HEAD_DIM].astype(o_ref.dtype)


def _inproj(x, shift, scale, gain, w_bf, part_lo, n_parts, out_dtype, tm):
    b, l, d = x.shape
    pw = HEADS * HEAD_DIM
    tm = min(tm, l)
    return pl.pallas_call(
        _inproj_kernel,
        out_shape=jax.ShapeDtypeStruct((b, n_parts * HEADS, l, HEAD_DIM), out_dtype),
        grid=(b, l // tm, n_parts),
        in_specs=[pl.BlockSpec((None, tm, d), lambda bi, mi, ni: (bi, mi, 0)),
                  pl.BlockSpec((None, 1, d), lambda bi, mi, ni: (bi, 0, 0)),
                  pl.BlockSpec((None, 1, d), lambda bi, mi, ni: (bi, 0, 0)),
                  pl.BlockSpec((1, d), lambda bi, mi, ni: (0, 0)),
                  pl.BlockSpec((d, pw), lambda bi, mi, ni: (0, part_lo + ni))],
        out_specs=pl.BlockSpec((None, HEADS, tm, HEAD_DIM), lambda bi, mi, ni: (bi, ni, mi, 0)),
        scratch_shapes=[pltpu.VMEM((tm, d), bf16)],
        compiler_params=_cparams(("parallel", "parallel", "arbitrary"), 48),
        name="inproj",
    )(x, shift, scale, gain, w_bf)


def _natten_bias_table(rpb, rows):
    kr = min(WIN_R, rows)
    q = np.arange(GRID_W)
    col_start = np.clip(q - WIN_C // 2, 0, GRID_W - WIN_C)
    kc = np.arange(GRID_W)
    in_win = (kc[None, :] >= col_start[:, None]) & (kc[None, :] < col_start[:, None] + WIN_C)
    d_col = np.clip(kc[None, :] - q[:, None] + WIN_C - 1, 0, 2 * WIN_C - 2)
    n_d0 = 2 * WIN_R - 1 - (kr - 1)
    onehot = (d_col[None] == np.arange(2 * WIN_C - 1)[:, None, None]).astype(np.float32)
    cols = jnp.einsum("hrc,cqk->hrqk", rpb.astype(f32), jnp.asarray(onehot), precision=_HIGHEST)
    cols = jnp.where(in_win[None, None], cols, -jnp.inf)
    t = jnp.stack([cols[:, d0:d0 + kr] for d0 in range(n_d0)], axis=1)
    t = jnp.transpose(t, (0, 1, 3, 2, 4))
    return t.reshape(rpb.shape[0], n_d0, GRID_W, kr * GRID_W)


def _natten_kernel(q_ref, k_ref, v_ref, kc_ref, vc_ref, bias_ref, o_ref, sw0, sc0, sw1, sc1, *, rows, kr):
    scale = HEAD_DIM ** -0.5
    kc = kc_ref[...]
    vc = vc_ref[...]
    nt = (((1,), (1,)), ((), ()))

    n_groups = rows // ROW_UNROLL
    gq = ROW_UNROLL * GRID_W

    def key_start(r):
        return jnp.clip(r - kr // 2, 0, rows - kr)

    def scores(g, sw_scr, sc_scr):
        q_all = q_ref[pl.ds(pl.multiple_of(g * gq, gq), gq), :]
        sc_scr[...] = lax.dot_general(q_all, kc, nt, preferred_element_type=f32) * scale
        for u in range(ROW_UNROLL):
            r = g * ROW_UNROLL + u
            kr0 = key_start(r)
            kw = k_ref[pl.ds(pl.multiple_of(kr0 * GRID_W, GRID_W), kr * GRID_W), :]
            q = q_all[u * GRID_W:(u + 1) * GRID_W, :]
            sw_scr[u] = (lax.dot_general(q, kw, nt, preferred_element_type=f32) * scale
                         + bias_ref[kr0 - r + WIN_R - 1])

    def attend(g, sw_scr, sc_scr):
        s_c = sc_scr[...]
        s_w = [sw_scr[u] for u in range(ROW_UNROLL)]
        m = jnp.maximum(jnp.concatenate([jnp.max(s, axis=-1, keepdims=True) for s in s_w], axis=0),
                        jnp.max(s_c, axis=-1, keepdims=True))
        p_c = jnp.exp(s_c - m)
        p_w = [jnp.exp(s - m[u * GRID_W:(u + 1) * GRID_W, :]) for u, s in enumerate(s_w)]
        denom = (jnp.concatenate([jnp.sum(p, axis=-1, keepdims=True) for p in p_w], axis=0)
                 + jnp.sum(p_c, axis=-1, keepdims=True))
        o_c = jnp.dot(p_c.astype(bf16), vc, preferred_element_type=f32)
        o_w = []
        for u, p in enumerate(p_w):
            k0 = pl.multiple_of(key_start(g * ROW_UNROLL + u) * GRID_W, GRID_W)
            o_w.append(jnp.dot(p.astype(bf16), v_ref[pl.ds(k0, kr * GRID_W), :], preferred_element_type=f32))
        o = (jnp.concatenate(o_w, axis=0) + o_c) / denom
        o_ref[pl.ds(pl.multiple_of(g * gq, gq), gq), :] = o.astype(o_ref.dtype)

    scores(0, sw0, sc0)

    def body(i, carry):
        g = 2 * i
        scores(g + 1, sw1, sc1)
        attend(g, sw0, sc0)
        scores(jnp.minimum(g + 2, n_groups - 1), sw0, sc0)
        attend(g + 1, sw1, sc1)
        return carry

    lax.fori_loop(0, n_groups // 2, body, 0)


def _natten(px_att, pc_att, bias_tab):
    b, _, l, _ = px_att.shape
    lc = pc_att.shape[2]
    rows = l // GRID_W
    kr = min(WIN_R, rows)
    n_d0 = bias_tab.shape[1]
    blk = lambda off: pl.BlockSpec((None, None, l, HEAD_DIM), lambda h, bi: (bi, off + h, 0, 0))
    cblk = lambda off: pl.BlockSpec((None, None, lc, HEAD_DIM), lambda h, bi: (bi, off + h, 0, 0))
    return pl.pallas_call(
        functools.partial(_natten_kernel, rows=rows, kr=kr),
        out_shape=jax.ShapeDtypeStruct((b, l, HEADS * HEAD_DIM), bf16),
        grid=(HEADS, b),
        in_specs=[blk(0), blk(HEADS), blk(2 * HEADS), cblk(0), cblk(HEADS),
                  pl.BlockSpec((None, n_d0, GRID_W, kr * GRID_W), lambda h, bi: (h, 0, 0, 0))],
        out_specs=pl.BlockSpec((None, l, HEAD_DIM), lambda h, bi: (bi, 0, h)),
        scratch_shapes=[pltpu.VMEM((ROW_UNROLL, GRID_W, kr * GRID_W), f32), pltpu.VMEM((ROW_UNROLL * GRID_W, lc), f32),
                        pltpu.VMEM((ROW_UNROLL, GRID_W, kr * GRID_W), f32), pltpu.VMEM((ROW_UNROLL * GRID_W, lc), f32)],
        compiler_params=_cparams(("parallel", "parallel")),
        name="natten",
    )(px_att, px_att, px_att, pc_att, pc_att, bias_tab)


def _hgrn_consts():
    t = np.arange(CHUNK)
    bt, bs = t[:, None] // SUB, t[None, :] // SUB
    tri, masks = [], []
    for sgn in (1, -1):
        before = (t[None, :] <= t[:, None]) if sgn == 1 else (t[None, :] >= t[:, None])
        tri.append(before.astype(np.float32))
        dist = (bt - bs) * sgn
        masks.append(np.stack([dist == 1, dist == 2, dist == 3, (dist == 0) & before]).astype(np.float32))
    return np.stack(tri), np.stack(masks)


def _rope_tables(l):
    t = jnp.arange(l)
    n_freq = HEAD_DIM // 4
    inv_freq = ROPE_THETA ** (-jnp.arange(n_freq, dtype=f32) / n_freq)
    ang_row = (t // GRID_W).astype(f32)[:, None] * inv_freq
    ang_col = (t % GRID_W).astype(f32)[:, None] * inv_freq
    cr, sr, cc, sc = jnp.cos(ang_row), jnp.sin(ang_row), jnp.cos(ang_col), jnp.sin(ang_col)
    z = jnp.zeros_like(sr)
    cos_t = jnp.concatenate([cr, cr, cc, cc], axis=-1)
    sin_up = jnp.concatenate([-sr, z, -sc, z], axis=-1)
    sin_dn = jnp.concatenate([z, sr, z, sc], axis=-1)
    return cos_t, sin_up, sin_dn


def _split3_dot(tri_bf, g):
    g1 = g.astype(bf16)
    r1 = g - g1.astype(f32)
    g2 = r1.astype(bf16)
    g3 = (r1 - g2.astype(f32)).astype(bf16)
    dot = lambda a: jnp.dot(tri_bf, a, preferred_element_type=f32)
    return dot(g1) + dot(g2) + dot(g3)


def _gate(f_raw, lb):
    log_f = jnp.log(lb + (1.0 - lb) * jax.nn.sigmoid(f_raw))
    key = (1.0 - lb) * jax.nn.sigmoid(-f_raw)
    return log_f, key


def _chunk_refs(cum, backward):
    nb = CHUNK // SUB
    if backward:
        ends = [cum[i * SUB:i * SUB + 1, :] for i in range(nb)]
        order = list(range(nb - 1, -1, -1))
    else:
        ends = [cum[i * SUB + SUB - 1:i * SUB + SUB, :] for i in range(nb)]
        order = list(range(nb))
    zero = jnp.zeros_like(ends[0])
    b_rows, g2_rows, g3_rows = [None] * nb, [None] * nb, [None] * nb
    for pos, i in enumerate(order):
        b_i = zero if pos == 0 else ends[order[pos - 1]]
        b_rows[i] = b_i
        g2_rows[i] = b_i - ends[order[pos - 2]] if pos >= 2 else zero
        g3_rows[i] = b_i - ends[order[pos - 3]] if pos >= 3 else zero
    expand = lambda rws: jnp.concatenate([jnp.broadcast_to(r, (SUB, HEAD_DIM)) for r in rws], axis=0)
    total = ends[order[-1]]
    return expand(b_rows), expand(ends), expand(g2_rows), expand(g3_rows), total


def _rope(x, cos_t, sin_up, sin_dn):
    return x * cos_t + pltpu.roll(x, 96, 1) * sin_up + pltpu.roll(x, 32, 1) * sin_dn


def _scan_group(states, items, tris, masks):
    nt = (((1,), (1,)), ((), ()))
    tn = (((0,), (0,)), ((), ()))
    pre = []
    for d, q, f_raw, v, lb, rope in items:
        g, key = _gate(f_raw, lb)
        pre.append((g, _rope(q, *rope), _rope(key, *rope), v.astype(bf16)))
    cums = [_split3_dot(tris[it[0]], p[0]) for it, p in zip(items, pre)]
    ops = []
    for it, (g, qr, kr, v_bf), cum in zip(items, pre, cums):
        b, e, gap2, gap3, total = _chunk_refs(cum, it[0] == 1)
        q_t = qr * jnp.exp(cum - b)
        k_hat = kr * jnp.exp(e - cum)
        k_til = kr * jnp.exp(b - cum)
        lhs = jnp.concatenate([q_t, q_t * jnp.exp(gap2), q_t * jnp.exp(gap3)], axis=0).astype(bf16)
        rhs = jnp.concatenate([k_hat, k_til], axis=0).astype(bf16)
        q_in = (q_t * jnp.exp(b)).astype(bf16)
        k_dec = (k_hat * jnp.exp(total - e)).astype(bf16)
        ops.append((lhs, rhs, q_in, k_dec, jnp.exp(total)))
    scores = [lax.dot_general(o[0], o[1], nt, preferred_element_type=f32) for o in ops]
    upds = [lax.dot_general(p[3], o[3], tn, preferred_element_type=f32) for p, o in zip(pre, ops)]
    intra = []
    for it, p, (g, qr, kr, v_bf) in zip(items, scores, pre):
        m = masks[it[0]]
        att = (jnp.where(m[0] > 0, p[0:CHUNK, 0:CHUNK], 0.0)
               + jnp.where(m[1] > 0, p[CHUNK:2 * CHUNK, 0:CHUNK], 0.0)
               + jnp.where(m[2] > 0, p[2 * CHUNK:3 * CHUNK, 0:CHUNK], 0.0)
               + jnp.where(m[3] > 0, p[0:CHUNK, CHUNK:2 * CHUNK], 0.0))
        intra.append(jnp.dot(att.astype(bf16), v_bf, preferred_element_type=f32))
    states = list(states)
    outs = []
    for it, o, upd, o_in in zip(items, ops, upds, intra):
        st = states[it[0]]
        outs.append(o_in + lax.dot_general(o[2], st.astype(bf16), nt, preferred_element_type=f32))
        states[it[0]] = st * o[4] + upd
    return states, outs


def _hgrn_kernel(q_ref, ff_ref, fb_ref, v_ref, gate_ref, cff_ref, cfb_ref, cv_ref, lb_ref, gain_ref,
                 cos_ref, sup_ref, sdn_ref, tri_ref, mask_ref, o_ref, of_scr, ob_scr, *, n_chunks, n_cchunks):
    lb_f = lb_ref[0]
    lb_b = lb_ref[1]
    tri_f = tri_ref[0].astype(bf16)
    tri_b = tri_ref[1].astype(bf16)
    zero = jnp.zeros((HEAD_DIM, HEAD_DIM), f32)

    ctx_items = []
    for c in range(n_cchunks):
        rf = pl.ds(c * CHUNK, CHUNK)
        rb = pl.ds((n_cchunks - 1 - c) * CHUNK, CHUNK)
        ctx_items.append((0, cv_ref[rf, :], cff_ref[rf, :], lb_f))
        ctx_items.append((1, cv_ref[rb, :], cfb_ref[rb, :], lb_b))
    st_f, st_b = _state_group([zero, zero], ctx_items, (tri_f, tri_b))

    def body(c, carry):
        masks = [[mask_ref[d, i] for i in range(4)] for d in range(2)]
        items, rows = [], []
        for u in range(CHUNK_UNROLL):
            cf = c * CHUNK_UNROLL + u
            rf = pl.ds(pl.multiple_of(cf * CHUNK, CHUNK), CHUNK)
            rb = pl.ds(pl.multiple_of((n_chunks - 1 - cf) * CHUNK, CHUNK), CHUNK)
            items.append((0, q_ref[rf, :], ff_ref[rf, :], v_ref[rf, :], lb_f,
                          (cos_ref[rf, :], sup_ref[rf, :], sdn_ref[rf, :])))
            items.append((1, q_ref[rb, :], fb_ref[rb, :], v_ref[rb, :], lb_b,
                          (cos_ref[rb, :], sup_ref[rb, :], sdn_ref[rb, :])))
            rows += [(of_scr, rf), (ob_scr, rb)]
        states, outs = _scan_group(carry, items, (tri_f, tri_b), masks)
        for (scr, rws), o in zip(rows, outs):
            scr[rws, :] = o
        return tuple(states)

    lax.fori_loop(0, n_chunks // CHUNK_UNROLL, body, (st_f, st_b))

    o = of_scr[...] + ob_scr[...]
    y = o * lax.rsqrt(jnp.mean(o * o, axis=-1, keepdims=True) + EPS) * gain_ref[...]
    gate = gate_ref[...]
    o_ref[...] = (y * (gate * jax.nn.sigmoid(gate))).astype(o_ref.dtype)


def _state_group(states, items, tris):
    gates = [_gate(f_raw, lb) for _, _, f_raw, lb in items]
    cums = [_split3_dot(tris[it[0]], g) for it, (g, _) in zip(items, gates)]
    terms = []
    for it, (_, key), cum in zip(items, gates, cums):
        total = cum[0:1, :] if it[0] == 1 else cum[CHUNK - 1:CHUNK, :]
        terms.append(((key * jnp.exp(total - cum)).astype(bf16), jnp.exp(total)))
    upds = [lax.dot_general(it[1].astype(bf16), k_dec, (((0,), (0,)), ((), ())), preferred_element_type=f32)
            for it, (k_dec, _) in zip(items, terms)]
    states = list(states)
    for it, (_, decay), upd in zip(items, terms, upds):
        states[it[0]] = states[it[0]] * decay + upd
    return states


def _hgrn(px_hg, pc_hg, lb2, hg_gain):
    b, _, l, _ = px_hg.shape
    lc = pc_hg.shape[2]
    cos_t, sin_up, sin_dn = _rope_tables(l)
    tri, masks = _hgrn_consts()
    blk = lambda off: pl.BlockSpec((None, None, l, HEAD_DIM), lambda h, bi: (bi, off + h, 0, 0))
    cblk = lambda off: pl.BlockSpec((None, None, lc, HEAD_DIM), lambda h, bi: (bi, off + h, 0, 0))
    full = lambda shp: pl.BlockSpec(shp, lambda h, bi: (0,) * len(shp))
    return pl.pallas_call(
        functools.partial(_hgrn_kernel, n_chunks=l // CHUNK, n_cchunks=lc // CHUNK),
        out_shape=jax.ShapeDtypeStruct((b, l, HEADS * HEAD_DIM), bf16),
        grid=(HEADS, b),
        in_specs=[blk(0), blk(HEADS), blk(2 * HEADS), blk(3 * HEADS), blk(4 * HEADS),
                  cblk(0), cblk(HEADS), cblk(2 * HEADS),
                  pl.BlockSpec((2, None, 1, HEAD_DIM), lambda h, bi: (0, h, 0, 0)),
                  full((1, HEAD_DIM)),
                  full((l, HEAD_DIM)), full((l, HEAD_DIM)), full((l, HEAD_DIM)),
                  full((2, CHUNK, CHUNK)), full((2, 4, CHUNK, CHUNK))],
        out_specs=pl.BlockSpec((None, l, HEAD_DIM), lambda h, bi: (bi, 0, h)),
        scratch_shapes=[pltpu.VMEM((l, HEAD_DIM), f32), pltpu.VMEM((l, HEAD_DIM), f32)],
        compiler_params=_cparams(("parallel", "parallel")),
        name="hgrn",
    )(px_hg, px_hg, px_hg, px_hg, px_hg, pc_hg, pc_hg, pc_hg,
      lb2.reshape(2, HEADS, 1, HEAD_DIM), hg_gain.reshape(1, HEAD_DIM),
      cos_t, sin_up, sin_dn, jnp.asarray(tri), jnp.asarray(masks))


def _pack_halves(h):
    k = h.shape[-1] // 2
    lo = lax.bitcast_convert_type(h[:, :k].astype(f32), u32)
    hi = lax.bitcast_convert_type(h[:, k:].astype(f32), u32)
    return (lo >> 16) | (hi & u32(HI_MASK))


def _unpack_halves(u):
    lo = lax.bitcast_convert_type(u << 16, f32)
    hi = lax.bitcast_convert_type(u & u32(HI_MASK), f32)
    return lo, hi


def _outproj_kernel(ana_ref, ahg_ref, x_ref, w0_ref, w1_ref, g1_ref, sh_ref, sc_ref, gain_ref, wr_ref, br_ref,
                    x1_ref, hp_ref, route_ref):
    y = (jnp.dot(ana_ref[...], w0_ref[...], preferred_element_type=f32)
         + jnp.dot(ahg_ref[...], w1_ref[...], preferred_element_type=f32))
    x1 = x_ref[...] + g1_ref[...] * y
    x1_ref[...] = x1
    ms = jnp.mean(x1 * x1, axis=-1, keepdims=True)
    h = x1 * lax.rsqrt(ms + EPS) * gain_ref[...]
    h = h * (1.0 + sc_ref[...]) + sh_ref[...]
    h_hi = h.astype(bf16)
    hp_ref[...] = _pack_halves(h_hi)
    h_lo = (h - h_hi.astype(f32)).astype(bf16)
    t = jnp.dot(h_hi, wr_ref[...], preferred_element_type=f32)
    logits = (t[:, :128] + t[:, 128:] + jnp.dot(h_lo, wr_ref[:, :128], preferred_element_type=f32)) + br_ref[...]
    lane = lax.broadcasted_iota(jnp.int32, logits.shape, 1).astype(f32)
    cur = logits
    vals, idxs = [], []
    for _ in range(TOP_K):
        m = jnp.max(cur, axis=-1, keepdims=True)
        i = jnp.min(jnp.where(cur == m, lane, float(logits.shape[-1])), axis=-1, keepdims=True)
        vals.append(m)
        idxs.append(i)
        cur = jnp.where(lane == i, -jnp.inf, cur)
    es = [jnp.exp(v - vals[0]) for v in vals]
    denom = es[0] + es[1] + es[2] + es[3]
    route = jnp.zeros(logits.shape, f32)
    for k in range(TOP_K):
        route = jnp.where(lane == k, idxs[k], route)
        route = jnp.where(lane == TOP_K + k, es[k] / denom, route)
        route = jnp.where(lane - float(MEMBER_LANE) == idxs[k], 1.0, route)
    route_ref[...] = route


def _outproj(a_na, a_hg, x, w_out_bf, g1, sh2, sc2, gain, w_router, b_router, tm):
    b, l, d = x.shape
    hw = a_na.shape[-1]
    tm = min(tm, l)
    n_e = w_router.shape[-1]
    wr = jnp.zeros((d, 128), f32).at[:, :n_e].set(w_router)
    wr_hi = wr.astype(bf16)
    wr = jnp.concatenate([wr_hi, (wr - wr_hi.astype(f32)).astype(bf16)], axis=1)
    br = jnp.full((1, 128), NEG_BIG, f32).at[0, :n_e].set(b_router)
    row = lambda last: pl.BlockSpec((None, tm, last), lambda bi, mi: (bi, mi, 0))
    vec = pl.BlockSpec((None, 1, d), lambda bi, mi: (bi, 0, 0))
    return pl.pallas_call(
        _outproj_kernel,
        out_shape=(jax.ShapeDtypeStruct((b, l, d), f32),
                   jax.ShapeDtypeStruct((b, l, d // 2), u32),
                   jax.ShapeDtypeStruct((b, l, 128), f32)),
        grid=(b, l // tm),
        in_specs=[row(hw), row(hw), row(d),
                  pl.BlockSpec((hw, d), lambda bi, mi: (0, 0)),
                  pl.BlockSpec((hw, d), lambda bi, mi: (1, 0)),
                  vec, vec, vec,
                  pl.BlockSpec((1, d), lambda bi, mi: (0, 0)),
                  pl.BlockSpec((d, 256), lambda bi, mi: (0, 0)),
                  pl.BlockSpec((1, 128), lambda bi, mi: (0, 0))],
        out_specs=(row(d), row(d // 2), row(128)),
        compiler_params=_cparams(("parallel", "parallel"), 48),
        name="outproj",
    )(a_na, a_hg, x, w_out_bf, w_out_bf, g1, sh2, sc2, gain, wr, br)


CUMSUM_TM = 512


def _cumsum_kernel(route_ref, tri_ref, o_ref, carry):
    @pl.when(pl.program_id(0) == 0)
    def _():
        carry[...] = jnp.zeros_like(carry)

    r = route_ref[...]
    lane = lax.broadcasted_iota(jnp.int32, r.shape, 1)
    member = jnp.where((lane >= MEMBER_LANE) & (lane < MEMBER_LANE + N_EXPERTS), r, 0.0).astype(bf16)
    incl = jnp.dot(tri_ref[...], member, preferred_element_type=f32) + carry[...]
    o_ref[...] = incl
    carry[...] = incl[CUMSUM_TM - 1:CUMSUM_TM, :]


def _token_cumsum(route):
    t = route.shape[0]
    tri = jnp.asarray(np.tril(np.ones((CUMSUM_TM, CUMSUM_TM), np.float32)), bf16)
    return pl.pallas_call(
        _cumsum_kernel,
        out_shape=jax.ShapeDtypeStruct((t, 128), f32),
        grid=(t // CUMSUM_TM,),
        in_specs=[pl.BlockSpec((CUMSUM_TM, 128), lambda i: (i, 0)),
                  pl.BlockSpec((CUMSUM_TM, CUMSUM_TM), lambda i: (0, 0))],
        out_specs=pl.BlockSpec((CUMSUM_TM, 128), lambda i: (i, 0)),
        scratch_shapes=[pltpu.VMEM((1, 128), f32)],
        compiler_params=_cparams(("arbitrary",)),
        name="cumsum",
    )(route, tri)


DISPATCH_TM = 256
TOK_UNROLL = 2


def _dispatch_kernel(pe_ref, dest_ref, hp_ref, xs_ref, zero_scr, sem):
    @pl.when(pl.program_id(0) == 0)
    def _():
        zero_scr[...] = jnp.zeros_like(zero_scr)

        def tail_copy(e):
            start = pl.multiple_of(pe_ref[e] - MOE_TM, MOE_TM)
            return pltpu.make_async_copy(zero_scr, xs_ref.at[pl.ds(start, MOE_TM), :], sem)

        def has_rows(e):
            return pe_ref[e] > jnp.where(e == 0, 0, pe_ref[jnp.maximum(e - 1, 0)])

        def issue_tail(e, c):
            @pl.when(has_rows(e))
            def _():
                tail_copy(e).start()
            return c

        def drain_tail(e, c):
            @pl.when(has_rows(e))
            def _():
                tail_copy(e).wait()
            return c

        lax.fori_loop(0, N_EXPERTS, issue_tail, 0)
        lax.fori_loop(0, N_EXPERTS, drain_tail, 0)

        def dead_copy(m):
            return pltpu.make_async_copy(zero_scr, xs_ref.at[pl.ds(pl.multiple_of(m * MOE_TM, MOE_TM), MOE_TM), :], sem)

        def issue_dead(m, c):
            dead_copy(m).start()
            return c

        def drain_dead(m, c):
            dead_copy(m).wait()
            return c

        n_used = pe_ref[N_EXPERTS - 1] // MOE_TM
        lax.fori_loop(n_used, xs_ref.shape[0] // MOE_TM, issue_dead, 0)
        lax.fori_loop(n_used, xs_ref.shape[0] // MOE_TM, drain_dead, 0)

    def row_copy(t, k):
        return pltpu.make_async_copy(hp_ref.at[pl.ds(t, 1), :], xs_ref.at[pl.ds(dest_ref[t * TOP_K + k], 1), :], sem)

    def issue(i, c):
        for u in range(TOK_UNROLL):
            for k in range(TOP_K):
                row_copy(i * TOK_UNROLL + u, k).start()
        return c

    def drain(i, c):
        for u in range(TOK_UNROLL):
            for k in range(TOP_K):
                row_copy(i * TOK_UNROLL + u, k).wait()
        return c

    lax.fori_loop(0, DISPATCH_TM // TOK_UNROLL, issue, 0)
    lax.fori_loop(0, DISPATCH_TM // TOK_UNROLL, drain, 0)


def _dispatch(pad_end, dest, hp, cap):
    t, kp = hp.shape
    return pl.pallas_call(
        _dispatch_kernel,
        out_shape=jax.ShapeDtypeStruct((cap, kp), hp.dtype),
        grid_spec=pltpu.PrefetchScalarGridSpec(
            num_scalar_prefetch=1,
            grid=(t // DISPATCH_TM,),
            in_specs=[pl.BlockSpec((DISPATCH_TM * TOP_K,), lambda i, pe: (i,), memory_space=pltpu.SMEM),
                      pl.BlockSpec((DISPATCH_TM, kp), lambda i, pe: (i, 0))],
            out_specs=pl.BlockSpec(memory_space=pl.ANY),
            scratch_shapes=[pltpu.VMEM((MOE_TM, kp), hp.dtype), pltpu.SemaphoreType.DMA(())]),
        compiler_params=_cparams(("arbitrary",)),
        name="dispatch",
    )(pad_end, dest, hp)


def _block_halves(m, half2_ref, nu_ref, out_ref, prepare, half):
    live = m < nu_ref[0]
    second = jnp.logical_and(live, half2_ref[m] == 1)
    zeros = jnp.zeros((MOE_HALF, out_ref.shape[1]), out_ref.dtype)

    @pl.when(live)
    def _():
        prepare()
        half(pl.ds(0, MOE_HALF))

    @pl.when(second)
    def _():
        half(pl.ds(MOE_HALF, MOE_HALF))

    @pl.when(jnp.logical_not(live))
    def _():
        out_ref[pl.ds(0, MOE_HALF), :] = zeros

    @pl.when(jnp.logical_not(second))
    def _():
        out_ref[pl.ds(MOE_HALF, MOE_HALF), :] = zeros


def _cast_chunks(w_refs, w_bf):
    rows = w_refs[0].shape[0]
    for i, w_ref in enumerate(w_refs):
        w_bf[pl.ds(i * rows, rows), :] = w_ref[...].astype(bf16)


def _gemm1_kernel(be_ref, first_ref, half2_ref, nu_ref, xs_ref, *refs):
    wg_refs, wl_refs = refs[:W_STREAMS], refs[W_STREAMS:2 * W_STREAMS]
    bg_ref, bl_ref, act_ref, wg_bf, wl_bf = refs[2 * W_STREAMS:]
    m = pl.program_id(1)

    def prepare():
        @pl.when(first_ref[m] == 1)
        def _():
            _cast_chunks(wg_refs, wg_bf)
            _cast_chunks(wl_refs, wl_bf)

    def half(rows):
        lo, hi = _unpack_halves(xs_ref[rows, :])
        lo = lo.astype(bf16)
        hi = hi.astype(bf16)
        k = lo.shape[-1]
        gate = (jnp.dot(lo, wg_bf[:k, :], preferred_element_type=f32)
                + jnp.dot(hi, wg_bf[k:, :], preferred_element_type=f32) + bg_ref[...])
        lin = (jnp.dot(lo, wl_bf[:k, :], preferred_element_type=f32)
               + jnp.dot(hi, wl_bf[k:, :], preferred_element_type=f32) + bl_ref[...])
        x_glu = jnp.minimum(gate, SWIGLU_LIMIT)
        x_lin = jnp.clip(lin, -SWIGLU_LIMIT, SWIGLU_LIMIT)
        act_ref[rows, :] = (x_glu * jax.nn.sigmoid(SWIGLU_ALPHA * x_glu) * (x_lin + 1.0)).astype(act_ref.dtype)

    _block_halves(m, half2_ref, nu_ref, act_ref, prepare, half)


def _gemm1(block_exp, first, half2, n_used, xs, w_gu, b_gu, tn):
    cap, kp = xs.shape
    n_e, d, f2 = w_gu.shape
    ff = f2 // 2
    tn = min(tn, ff)
    nb = cap // MOE_TM
    nj = ff // tn
    live = lambda m, nu: jnp.minimum(m, nu[0] - 1)
    dk = d // W_STREAMS
    wspec = lambda col0, c: pl.BlockSpec((None, dk, tn), lambda j, m, be, fi, h2, nu: (be[m], c, col0 + j))
    return pl.pallas_call(
        _gemm1_kernel,
        out_shape=jax.ShapeDtypeStruct((cap, ff), bf16),
        grid_spec=pltpu.PrefetchScalarGridSpec(
            num_scalar_prefetch=4,
            grid=(nj, nb),
            in_specs=[pl.BlockSpec((MOE_TM, kp), lambda j, m, be, fi, h2, nu: (live(m, nu), 0))]
                     + [wspec(0, c) for c in range(W_STREAMS)] + [wspec(nj, c) for c in range(W_STREAMS)]
                     + [pl.BlockSpec((None, 1, tn), lambda j, m, be, fi, h2, nu: (be[m], 0, j)),
                        pl.BlockSpec((None, 1, tn), lambda j, m, be, fi, h2, nu: (be[m], 0, nj + j))],
            out_specs=pl.BlockSpec((MOE_TM, tn), lambda j, m, be, fi, h2, nu: (m, j)),
            scratch_shapes=[pltpu.VMEM((d, tn), bf16), pltpu.VMEM((d, tn), bf16)]),
        compiler_params=_cparams(("arbitrary", "arbitrary"), 56),
        name="gemm1",
    )(block_exp, first, half2, n_used, xs, *([w_gu] * (2 * W_STREAMS)),
      b_gu.reshape(n_e, 1, f2), b_gu.reshape(n_e, 1, f2))


def _gemm2_kernel(be_ref, first_ref, half2_ref, nu_ref, act_ref, *refs):
    w_refs = refs[:W_STREAMS]
    b_ref, y_ref, w_bf = refs[W_STREAMS:]
    m = pl.program_id(0)

    def prepare():
        @pl.when(first_ref[m] == 1)
        def _():
            _cast_chunks(w_refs, w_bf)

    def half(rows):
        y = jnp.dot(act_ref[rows, :], w_bf[...], preferred_element_type=f32) + b_ref[...]
        y_ref[rows, :] = _pack_halves(y.astype(bf16))

    _block_halves(m, half2_ref, nu_ref, y_ref, prepare, half)


def _gemm2(block_exp, first, half2, n_used, act, w_down, b_down):
    cap, ff = act.shape
    n_e, _, d = w_down.shape
    nb = cap // MOE_TM
    live = lambda m, nu: jnp.minimum(m, nu[0] - 1)
    wspec = lambda c: pl.BlockSpec((None, ff // W_STREAMS, d), lambda m, be, fi, h2, nu: (be[m], c, 0))
    return pl.pallas_call(
        _gemm2_kernel,
        out_shape=jax.ShapeDtypeStruct((cap, d // 2), u32),
        grid_spec=pltpu.PrefetchScalarGridSpec(
            num_scalar_prefetch=4,
            grid=(nb,),
            in_specs=[pl.BlockSpec((MOE_TM, ff), lambda m, be, fi, h2, nu: (live(m, nu), 0))]
                     + [wspec(c) for c in range(W_STREAMS)]
                     + [pl.BlockSpec((None, 1, d), lambda m, be, fi, h2, nu: (be[m], 0, 0))],
            out_specs=pl.BlockSpec((MOE_TM, d // 2), lambda m, be, fi, h2, nu: (m, 0)),
            scratch_shapes=[pltpu.VMEM((ff, d), bf16)]),
        compiler_params=_cparams(("arbitrary",), 56),
        name="gemm2",
    )(block_exp, first, half2, n_used, act, *([w_down] * W_STREAMS), b_down.reshape(n_e, 1, d))


COMBINE_TM = 256


def _combine_kernel(dest_ref, yb_ref, route_ref, x1_ref, g2_ref, gain_ref, o_ref, buf, sem):
    def row_copy(t, k):
        return pltpu.make_async_copy(yb_ref.at[pl.ds(dest_ref[t * TOP_K + k], 1), :], buf.at[k, pl.ds(t, 1), :], sem)

    def issue(i, c):
        for u in range(TOK_UNROLL):
            for k in range(TOP_K):
                row_copy(i * TOK_UNROLL + u, k).start()
        return c

    def drain(i, c):
        for u in range(TOK_UNROLL):
            for k in range(TOP_K):
                row_copy(i * TOK_UNROLL + u, k).wait()
        return c

    lax.fori_loop(0, COMBINE_TM // TOK_UNROLL, issue, 0)
    lax.fori_loop(0, COMBINE_TM // TOK_UNROLL, drain, 0)

    route = route_ref[...]
    acc = None
    for k in range(TOP_K):
        lo, hi = _unpack_halves(buf[k])
        term = route[:, TOP_K + k:TOP_K + k + 1] * jnp.concatenate([lo, hi], axis=-1)
        acc = term if acc is None else acc + term
    x2 = x1_ref[...] + g2_ref[...] * acc
    ms = jnp.mean(x2 * x2, axis=-1, keepdims=True)
    o_ref[...] = x2 * lax.rsqrt(ms + EPS) * gain_ref[...]


def _combine(dest, yb, route, x1, g2, gain):
    b, l, d = x1.shape
    tm = min(COMBINE_TM, l)
    assert tm == COMBINE_TM
    per_b = l // tm
    return pl.pallas_call(
        _combine_kernel,
        out_shape=jax.ShapeDtypeStruct((b, l, d), f32),
        grid=(b * per_b,),
        in_specs=[pl.BlockSpec((tm * TOP_K,), lambda i: (i,), memory_space=pltpu.SMEM),
                  pl.BlockSpec(memory_space=pl.ANY),
                  pl.BlockSpec((None, tm, 128), lambda i: (i // per_b, i % per_b, 0)),
                  pl.BlockSpec((None, tm, d), lambda i: (i // per_b, i % per_b, 0)),
                  pl.BlockSpec((None, 1, d), lambda i: (i // per_b, 0, 0)),
                  pl.BlockSpec((1, d), lambda i: (0, 0))],
        out_specs=pl.BlockSpec((None, tm, d), lambda i: (i // per_b, i % per_b, 0)),
        scratch_shapes=[pltpu.VMEM((TOP_K, tm, d // 2), u32), pltpu.SemaphoreType.DMA(())],
        compiler_params=_cparams(("arbitrary",), 40),
        name="combine",
    )(dest, yb, route, x1, g2, gain)


def _routing(route, n_blocks):
    experts = slice(MEMBER_LANE, MEMBER_LANE + N_EXPERTS)
    top_idx = route[:, :TOP_K].astype(jnp.int32)
    incl = _token_cumsum(route)[:, experts]
    rank = (incl - route[:, experts]).astype(jnp.int32)
    counts = incl[-1].astype(jnp.int32)
    padded = (counts + MOE_TM - 1) // MOE_TM * MOE_TM
    pad_end = jnp.cumsum(padded).astype(jnp.int32)
    slot0 = (pad_end - padded)[None, :] + rank
    onehot = top_idx[:, :, None] == jnp.arange(N_EXPERTS, dtype=jnp.int32)[None, None, :]
    dest = jnp.sum(jnp.where(onehot, slot0[:, None, :], 0), axis=-1).astype(jnp.int32)
    block_start = jnp.arange(n_blocks, dtype=jnp.int32) * MOE_TM
    block_exp = jnp.minimum(jnp.sum(pad_end[None, :] <= block_start[:, None], axis=1), N_EXPERTS - 1).astype(jnp.int32)
    first = jnp.concatenate([jnp.ones((1,), jnp.int32), (block_exp[1:] != block_exp[:-1]).astype(jnp.int32)])
    n_used = (pad_end[-1:] // MOE_TM).astype(jnp.int32)
    token_end = (pad_end - padded + counts)[block_exp]
    half2 = (token_end > block_start + MOE_HALF).astype(jnp.int32)
    return dest.reshape(-1), pad_end, block_exp, first, half2, n_used


def _moe(hp, route, x1, g2, norm_final, w_gu, b_gu, w_down, b_down):
    b, l, d = x1.shape
    t = b * l
    n_blocks = -(-(t * TOP_K + N_EXPERTS * (MOE_TM - 1)) // MOE_TM)
    dest, pad_end, block_exp, first, half2, n_used = _routing(route.reshape(t, 128), n_blocks)
    xs = _dispatch(pad_end, dest, hp.reshape(t, d // 2), n_blocks * MOE_TM)
    act = _gemm1(block_exp, first, half2, n_used, xs, w_gu, b_gu, 1024)
    yb = _gemm2(block_exp, first, half2, n_used, act, w_down, b_down)
    return _combine(dest, yb, route, x1, g2, norm_final.reshape(1, d))


def kernel(x, c, ctx, c_ctx, w_ada, b_ada, norm_mix, norm_ffn, w_in, lb_table, hg_norm, rpb, w_out, w_router,
           b_router, w_gu, b_gu, w_down, b_down, norm_final):
    b, l, d = x.shape
    assert w_ada.shape[0] == 1, "single-layer block"
    rows = l // GRID_W

    c16 = jnp.zeros((16, d), f32).at[:b].set(c).at[b].set(c_ctx)
    mod = _ada(c16, w_ada[0], b_ada[0])
    sh1, sc1, g1, sh2, sc2, g2 = [mod[:b, i * d:(i + 1) * d].reshape(b, 1, d) for i in range(6)]
    csh = jnp.broadcast_to(mod[b, :d].reshape(1, 1, d), (b, 1, d))
    csc = jnp.broadcast_to(mod[b, d:2 * d].reshape(1, 1, d), (b, 1, d))

    w_in_bf = w_in[0].astype(bf16)
    gain_mix = norm_mix[0].reshape(1, d)
    px_att = _inproj(x, sh1, sc1, gain_mix, w_in_bf, P_QN, 3, bf16, 1024)
    px_hg = _inproj(x, sh1, sc1, gain_mix, w_in_bf, P_QH, 5, f32, 1024)
    pc_att = _inproj(ctx, csh, csc, gain_mix, w_in_bf, P_KN, 2, bf16, 256)
    pc_hg = _inproj(ctx, csh, csc, gain_mix, w_in_bf, P_FF, 3, f32, 256)

    o_na = _natten(px_att, pc_att, _natten_bias_table(rpb[0], rows))

    lower_bounds = jnp.cumsum(jax.nn.softmax(lb_table.astype(f32), axis=0), axis=0)
    o_hg = _hgrn(px_hg, pc_hg, lower_bounds[0].reshape(2, HEADS * HEAD_DIM), hg_norm[0])

    x1, hp, route = _outproj(o_na, o_hg, x, w_out[0].astype(bf16), g1, sh2, sc2, norm_ffn[0].reshape(1, d),
                             w_router[0], b_router[0], 256)
    return _moe(hp, route, x1, g2, norm_final, w_gu[0], b_gu[0], w_down[0], b_down[0])
```

```python
import functools

import numpy as np
import jax
import jax.numpy as jnp
from jax import lax
from jax.experimental import pallas as pl
from jax.experimental.pallas import tpu as pltpu

f32 = jnp.float32
bf16 = jnp.bfloat16
u32 = jnp.uint32

GRID_W = 64
HEADS = 8
HEAD_DIM = 128
WIN_R = 8
WIN_C = 16
ROPE_THETA = 10000.0
N_EXPERTS = 32
TOP_K = 4
SWIGLU_LIMIT = 7.0
SWIGLU_ALPHA = 1.702
EPS = 1e-6
P_QN, P_KN, P_VN, P_QH, P_FF, P_FB, P_IH, P_GH = range(8)

CHUNK = 64
SUB = 16
CHUNK_UNROLL = 4
ROW_UNROLL = 4
MOE_TM = 512
MOE_HALF = 256
W_STREAMS = 1
NEG_BIG = -1e30
MEMBER_LANE = 32
HI_MASK = 0xFFFF0000

_HIGHEST = lax.Precision.HIGHEST


def _cparams(sem, vmem_mb=None):
    kw = dict(dimension_semantics=sem)
    if vmem_mb is not None:
        kw["vmem_limit_bytes"] = vmem_mb * 1024 * 1024
    return pltpu.CompilerParams(**kw)


def _ada_kernel(c_ref, w_ref, b_ref, o_ref):
    c = c_ref[...]
    cond = c * jax.nn.sigmoid(c)
    o_ref[...] = jnp.dot(cond, w_ref[...], precision=_HIGHEST, preferred_element_type=f32) + b_ref[...]


def _ada(c16, w_ada, b_ada):
    d, n = w_ada.shape
    tn = min(1024, n)
    return pl.pallas_call(
        _ada_kernel,
        out_shape=jax.ShapeDtypeStruct((c16.shape[0], n), f32),
        grid=(n // tn,),
        in_specs=[pl.BlockSpec((c16.shape[0], d), lambda j: (0, 0)),
                  pl.BlockSpec((d, tn), lambda j: (0, j)),
                  pl.BlockSpec((1, tn), lambda j: (0, j))],
        out_specs=pl.BlockSpec((c16.shape[0], tn), lambda j: (0, j)),
        compiler_params=_cparams(("parallel",), 40),
        name="ada",
    )(c16, w_ada, b_ada.reshape(1, n))


def _inproj_kernel(x_ref, shift_ref, scale_ref, gain_ref, w_ref, o_ref, h_scr):
    @pl.when(pl.program_id(2) == 0)
    def _():
        x = x_ref[...]
        ms = jnp.mean(x * x, axis=-1, keepdims=True)
        y = x * lax.rsqrt(ms + EPS) * gain_ref[...]
        h_scr[...] = (y * (1.0 + scale_ref[...]) + shift_ref[...]).astype(bf16)

    acc = jnp.dot(h_scr[...], w_ref[...], preferred_element_type=f32)
    for hh in range(HEADS):
        o_ref[hh] = acc[:, hh * HEAD_DIM:(hh + 1) * HEAD_DIM].astype(o_ref.dtype)


def _inproj(x, shift, scale, gain, w_bf, part_lo, n_parts, out_dtype, tm):
    b, l, d = x.shape
    pw = HEADS * HEAD_DIM
    tm = min(tm, l)
    return pl.pallas_call(
        _inproj_kernel,
        out_shape=jax.ShapeDtypeStruct((b, n_parts * HEADS, l, HEAD_DIM), out_dtype),
        grid=(b, l // tm, n_parts),
        in_specs=[pl.BlockSpec((None, tm, d), lambda bi, mi, ni: (bi, mi, 0)),
                  pl.BlockSpec((None, 1, d), lambda bi, mi, ni: (bi, 0, 0)),
                  pl.BlockSpec((None, 1, d), lambda bi, mi, ni: (bi, 0, 0)),
                  pl.BlockSpec((1, d), lambda bi, mi, ni: (0, 0)),
                  pl.BlockSpec((d, pw), lambda bi, mi, ni: (0, part_lo + ni))],
        out_specs=pl.BlockSpec((None, HEADS, tm, HEAD_DIM), lambda bi, mi, ni: (bi, ni, mi, 0)),
        scratch_shapes=[pltpu.VMEM((tm, d), bf16)],
        compiler_params=_cparams(("parallel", "parallel", "arbitrary"), 48),
        name="inproj",
    )(x, shift, scale, gain, w_bf)


def _natten_bias_table(rpb, rows):
    kr = min(WIN_R, rows)
    q = np.arange(GRID_W)
    col_start = np.clip(q - WIN_C // 2, 0, GRID_W - WIN_C)
    kc = np.arange(GRID_W)
    in_win = (kc[None, :] >= col_start[:, None]) & (kc[None, :] < col_start[:, None] + WIN_C)
    d_col = np.clip(kc[None, :] - q[:, None] + WIN_C - 1, 0, 2 * WIN_C - 2)
    n_d0 = 2 * WIN_R - 1 - (kr - 1)
    onehot = (d_col[None] == np.arange(2 * WIN_C - 1)[:, None, None]).astype(np.float32)
    cols = jnp.einsum("hrc,cqk->hrqk", rpb.astype(f32), jnp.asarray(onehot), precision=_HIGHEST)
    cols = jnp.where(in_win[None, None], cols, -jnp.inf)
    t = jnp.stack([cols[:, d0:d0 + kr] for d0 in range(n_d0)], axis=1)
    t = jnp.transpose(t, (0, 1, 3, 2, 4))
    return t.reshape(rpb.shape[0], n_d0, GRID_W, kr * GRID_W)


def _natten_kernel(q_ref, k_ref, v_ref, kc_ref, vc_ref, bias_ref, o_ref, sw0, sc0, sw1, sc1, *, rows, kr):
    scale = HEAD_DIM ** -0.5
    kc = kc_ref[...]
    vc = vc_ref[...]
    nt = (((1,), (1,)), ((), ()))

    n_groups = rows // ROW_UNROLL
    gq = ROW_UNROLL * GRID_W

    def key_start(r):
        return jnp.clip(r - kr // 2, 0, rows - kr)

    def scores(g, sw_scr, sc_scr):
        q_all = q_ref[pl.ds(pl.multiple_of(g * gq, gq), gq), :]
        sc_scr[...] = lax.dot_general(q_all, kc, nt, preferred_element_type=f32) * scale
        for u in range(ROW_UNROLL):
            r = g * ROW_UNROLL + u
            kr0 = key_start(r)
            kw = k_ref[pl.ds(pl.multiple_of(kr0 * GRID_W, GRID_W), kr * GRID_W), :]
            q = q_all[u * GRID_W:(u + 1) * GRID_W, :]
            sw_scr[u] = (lax.dot_general(q, kw, nt, preferred_element_type=f32) * scale
                         + bias_ref[kr0 - r + WIN_R - 1])

    def attend(g, sw_scr, sc_scr):
        s_c = sc_scr[...]
        s_w = [sw_scr[u] for u in range(ROW_UNROLL)]
        m = jnp.maximum(jnp.concatenate([jnp.max(s, axis=-1, keepdims=True) for s in s_w], axis=0),
                        jnp.max(s_c, axis=-1, keepdims=True))
        p_c = jnp.exp(s_c - m)
        p_w = [jnp.exp(s - m[u * GRID_W:(u + 1) * GRID_W, :]) for u, s in enumerate(s_w)]
        denom = (jnp.concatenate([jnp.sum(p, axis=-1, keepdims=True) for p in p_w], axis=0)
                 + jnp.sum(p_c, axis=-1, keepdims=True))
        o_c = jnp.dot(p_c.astype(bf16), vc, preferred_element_type=f32)
        o_w = []
        for u, p in enumerate(p_w):
            k0 = pl.multiple_of(key_start(g * ROW_UNROLL + u) * GRID_W, GRID_W)
            o_w.append(jnp.dot(p.astype(bf16), v_ref[pl.ds(k0, kr * GRID_W), :], preferred_element_type=f32))
        o = (jnp.concatenate(o_w, axis=0) + o_c) / denom
        o_ref[pl.ds(pl.multiple_of(g * gq, gq), gq), :] = o.astype(o_ref.dtype)

    scores(0, sw0, sc0)

    def body(i, carry):
        g = 2 * i
        scores(g + 1, sw1, sc1)
        attend(g, sw0, sc0)
        scores(jnp.minimum(g + 2, n_groups - 1), sw0, sc0)
        attend(g + 1, sw1, sc1)
        return carry

    lax.fori_loop(0, n_groups // 2, body, 0)


def _natten(px_att, pc_att, bias_tab):
    b, _, l, _ = px_att.shape
    lc = pc_att.shape[2]
    rows = l // GRID_W
    kr = min(WIN_R, rows)
    n_d0 = bias_tab.shape[1]
    blk = lambda off: pl.BlockSpec((None, None, l, HEAD_DIM), lambda h, bi: (bi, off + h, 0, 0))
    cblk = lambda off: pl.BlockSpec((None, None, lc, HEAD_DIM), lambda h, bi: (bi, off + h, 0, 0))
    return pl.pallas_call(
        functools.partial(_natten_kernel, rows=rows, kr=kr),
        out_shape=jax.ShapeDtypeStruct((b, l, HEADS * HEAD_DIM), bf16),
        grid=(HEADS, b),
        in_specs=[blk(0), blk(HEADS), blk(2 * HEADS), cblk(0), cblk(HEADS),
                  pl.BlockSpec((None, n_d0, GRID_W, kr * GRID_W), lambda h, bi: (h, 0, 0, 0))],
        out_specs=pl.BlockSpec((None, l, HEAD_DIM), lambda h, bi: (bi, 0, h)),
        scratch_shapes=[pltpu.VMEM((ROW_UNROLL, GRID_W, kr * GRID_W), f32), pltpu.VMEM((ROW_UNROLL * GRID_W, lc), f32),
                        pltpu.VMEM((ROW_UNROLL, GRID_W, kr * GRID_W), f32), pltpu.VMEM((ROW_UNROLL * GRID_W, lc), f32)],
        compiler_params=_cparams(("parallel", "parallel")),
        name="natten",
    )(px_att, px_att, px_att, pc_att, pc_att, bias_tab)


def _hgrn_consts():
    t = np.arange(CHUNK)
    bt, bs = t[:, None] // SUB, t[None, :] // SUB
    tri, masks = [], []
    for sgn in (1, -1):
        before = (t[None, :] <= t[:, None]) if sgn == 1 else (t[None, :] >= t[:, None])
        tri.append(before.astype(np.float32))
        dist = (bt - bs) * sgn
        masks.append(np.stack([dist == 1, dist == 2, dist == 3, (dist == 0) & before]).astype(np.float32))
    return np.stack(tri), np.stack(masks)


def _rope_tables(l):
    t = jnp.arange(l)
    n_freq = HEAD_DIM // 4
    inv_freq = ROPE_THETA ** (-jnp.arange(n_freq, dtype=f32) / n_freq)
    ang_row = (t // GRID_W).astype(f32)[:, None] * inv_freq
    ang_col = (t % GRID_W).astype(f32)[:, None] * inv_freq
    cr, sr, cc, sc = jnp.cos(ang_row), jnp.sin(ang_row), jnp.cos(ang_col), jnp.sin(ang_col)
    z = jnp.zeros_like(sr)
    cos_t = jnp.concatenate([cr, cr, cc, cc], axis=-1)
    sin_up = jnp.concatenate([-sr, z, -sc, z], axis=-1)
    sin_dn = jnp.concatenate([z, sr, z, sc], axis=-1)
    return cos_t, sin_up, sin_dn


def _split3_dot(tri_bf, g):
    g1 = g.astype(bf16)
    r1 = g - g1.astype(f32)
    g2 = r1.astype(bf16)
    g3 = (r1 - g2.astype(f32)).astype(bf16)
    dot = lambda a: jnp.dot(tri_bf, a, preferred_element_type=f32)
    return dot(g1) + dot(g2) + dot(g3)


def _gate(f_raw, lb):
    log_f = jnp.log(lb + (1.0 - lb) * jax.nn.sigmoid(f_raw))
    key = (1.0 - lb) * jax.nn.sigmoid(-f_raw)
    return log_f, key


def _chunk_refs(cum, backward):
    nb = CHUNK // SUB
    if backward:
        ends = [cum[i * SUB:i * SUB + 1, :] for i in range(nb)]
        order = list(range(nb - 1, -1, -1))
    else:
        ends = [cum[i * SUB + SUB - 1:i * SUB + SUB, :] for i in range(nb)]
        order = list(range(nb))
    zero = jnp.zeros_like(ends[0])
    b_rows, g2_rows, g3_rows = [None] * nb, [None] * nb, [None] * nb
    for pos, i in enumerate(order):
        b_i = zero if pos == 0 else ends[order[pos - 1]]
        b_rows[i] = b_i
        g2_rows[i] = b_i - ends[order[pos - 2]] if pos >= 2 else zero
        g3_rows[i] = b_i - ends[order[pos - 3]] if pos >= 3 else zero
    expand = lambda rws: jnp.concatenate([jnp.broadcast_to(r, (SUB, HEAD_DIM)) for r in rws], axis=0)
    total = ends[order[-1]]
    return expand(b_rows), expand(ends), expand(g2_rows), expand(g3_rows), total


def _rope(x, cos_t, sin_up, sin_dn):
    return x * cos_t + pltpu.roll(x, 96, 1) * sin_up + pltpu.roll(x, 32, 1) * sin_dn


def _scan_group(states, items, tris, masks):
    nt = (((1,), (1,)), ((), ()))
    tn = (((0,), (0,)), ((), ()))
    pre = []
    for d, q, f_raw, v, lb, rope in items:
        g, key = _gate(f_raw, lb)
        pre.append((g, _rope(q, *rope), _rope(key, *rope), v.astype(bf16)))
    cums = [_split3_dot(tris[it[0]], p[0]) for it, p in zip(items, pre)]
    ops = []
    for it, (g, qr, kr, v_bf), cum in zip(items, pre, cums):
        b, e, gap2, gap3, total = _chunk_refs(cum, it[0] == 1)
        q_t = qr * jnp.exp(cum - b)
        k_hat = kr * jnp.exp(e - cum)
        k_til = kr * jnp.exp(b - cum)
        lhs = jnp.concatenate([q_t, q_t * jnp.exp(gap2), q_t * jnp.exp(gap3)], axis=0).astype(bf16)
        rhs = jnp.concatenate([k_hat, k_til], axis=0).astype(bf16)
        q_in = (q_t * jnp.exp(b)).astype(bf16)
        k_dec = (k_hat * jnp.exp(total - e)).astype(bf16)
        ops.append((lhs, rhs, q_in, k_dec, jnp.exp(total)))
    scores = [lax.dot_general(o[0], o[1], nt, preferred_element_type=f32) for o in ops]
    upds = [lax.dot_general(p[3], o[3], tn, preferred_element_type=f32) for p, o in zip(pre, ops)]
    intra = []
    for it, p, (g, qr, kr, v_bf) in zip(items, scores, pre):
        m = masks[it[0]]
        att = (jnp.where(m[0] > 0, p[0:CHUNK, 0:CHUNK], 0.0)
               + jnp.where(m[1] > 0, p[CHUNK:2 * CHUNK, 0:CHUNK], 0.0)
               + jnp.where(m[2] > 0, p[2 * CHUNK:3 * CHUNK, 0:CHUNK], 0.0)
               + jnp.where(m[3] > 0, p[0:CHUNK, CHUNK:2 * CHUNK], 0.0))
        intra.append(jnp.dot(att.astype(bf16), v_bf, preferred_element_type=f32))
    states = list(states)
    outs = []
    for it, o, upd, o_in in zip(items, ops, upds, intra):
        st = states[it[0]]
        outs.append(o_in + lax.dot_general(o[2], st.astype(bf16), nt, preferred_element_type=f32))
        states[it[0]] = st * o[4] + upd
    return states, outs


def _hgrn_kernel(q_ref, ff_ref, fb_ref, v_ref, gate_ref, cff_ref, cfb_ref, cv_ref, lb_ref, gain_ref,
                 cos_ref, sup_ref, sdn_ref, tri_ref, mask_ref, o_ref, of_scr, ob_scr, *, n_chunks, n_cchunks):
    lb_f = lb_ref[0]
    lb_b = lb_ref[1]
    tri_f = tri_ref[0].astype(bf16)
    tri_b = tri_ref[1].astype(bf16)
    zero = jnp.zeros((HEAD_DIM, HEAD_DIM), f32)

    ctx_items = []
    for c in range(n_cchunks):
        rf = pl.ds(c * CHUNK, CHUNK)
        rb = pl.ds((n_cchunks - 1 - c) * CHUNK, CHUNK)
        ctx_items.append((0, cv_ref[rf, :], cff_ref[rf, :], lb_f))
        ctx_items.append((1, cv_ref[rb, :], cfb_ref[rb, :], lb_b))
    st_f, st_b = _state_group([zero, zero], ctx_items, (tri_f, tri_b))

    def body(c, carry):
        masks = [[mask_ref[d, i] for i in range(4)] for d in range(2)]
        items, rows = [], []
        for u in range(CHUNK_UNROLL):
            cf = c * CHUNK_UNROLL + u
            rf = pl.ds(pl.multiple_of(cf * CHUNK, CHUNK), CHUNK)
            rb = pl.ds(pl.multiple_of((n_chunks - 1 - cf) * CHUNK, CHUNK), CHUNK)
            items.append((0, q_ref[rf, :], ff_ref[rf, :], v_ref[rf, :], lb_f,
                          (cos_ref[rf, :], sup_ref[rf, :], sdn_ref[rf, :])))
            items.append((1, q_ref[rb, :], fb_ref[rb, :], v_ref[rb, :], lb_b,
                          (cos_ref[rb, :], sup_ref[rb, :], sdn_ref[rb, :])))
            rows += [(of_scr, rf), (ob_scr, rb)]
        states, outs = _scan_group(carry, items, (tri_f, tri_b), masks)
        for (scr, rws), o in zip(rows, outs):
            scr[rws, :] = o
        return tuple(states)

    lax.fori_loop(0, n_chunks // CHUNK_UNROLL, body, (st_f, st_b))

    o = of_scr[...] + ob_scr[...]
    y = o * lax.rsqrt(jnp.mean(o * o, axis=-1, keepdims=True) + EPS) * gain_ref[...]
    gate = gate_ref[...]
    o_ref[...] = (y * (gate * jax.nn.sigmoid(gate))).astype(o_ref.dtype)


def _state_group(states, items, tris):
    gates = [_gate(f_raw, lb) for _, _, f_raw, lb in items]
    cums = [_split3_dot(tris[it[0]], g) for it, (g, _) in zip(items, gates)]
    terms = []
    for it, (_, key), cum in zip(items, gates, cums):
        total = cum[0:1, :] if it[0] == 1 else cum[CHUNK - 1:CHUNK, :]
        terms.append(((key * jnp.exp(total - cum)).astype(bf16), jnp.exp(total)))
    upds = [lax.dot_general(it[1].astype(bf16), k_dec, (((0,), (0,)), ((), ())), preferred_element_type=f32)
            for it, (k_dec, _) in zip(items, terms)]
    states = list(states)
    for it, (_, decay), upd in zip(items, terms, upds):
        states[it[0]] = states[it[0]] * decay + upd
    return states


def _hgrn(px_hg, pc_hg, lb2, hg_gain):
    b, _, l, _ = px_hg.shape
    lc = pc_hg.shape[2]
    cos_t, sin_up, sin_dn = _rope_tables(l)
    tri, masks = _hgrn_consts()
    blk = lambda off: pl.BlockSpec((None, None, l, HEAD_DIM), lambda h, bi: (bi, off + h, 0, 0))
    cblk = lambda off: pl.BlockSpec((None, None, lc, HEAD_DIM), lambda h, bi: (bi, off + h, 0, 0))
    full = lambda shp: pl.BlockSpec(shp, lambda h, bi: (0,) * len(shp))
    return pl.pallas_call(
        functools.partial(_hgrn_kernel, n_chunks=l // CHUNK, n_cchunks=lc // CHUNK),
        out_shape=jax.ShapeDtypeStruct((b, l, HEADS * HEAD_DIM), bf16),
        grid=(HEADS, b),
        in_specs=[blk(0), blk(HEADS), blk(2 * HEADS), blk(3 * HEADS), blk(4 * HEADS),
                  cblk(0), cblk(HEADS), cblk(2 * HEADS),
                  pl.BlockSpec((2, None, 1, HEAD_DIM), lambda h, bi: (0, h, 0, 0)),
                  full((1, HEAD_DIM)),
                  full((l, HEAD_DIM)), full((l, HEAD_DIM)), full((l, HEAD_DIM)),
                  full((2, CHUNK, CHUNK)), full((2, 4, CHUNK, CHUNK))],
        out_specs=pl.BlockSpec((None, l, HEAD_DIM), lambda h, bi: (bi, 0, h)),
        scratch_shapes=[pltpu.VMEM((l, HEAD_DIM), f32), pltpu.VMEM((l, HEAD_DIM), f32)],
        compiler_params=_cparams(("parallel", "parallel")),
        name="hgrn",
    )(px_hg, px_hg, px_hg, px_hg, px_hg, pc_hg, pc_hg, pc_hg,
      lb2.reshape(2, HEADS, 1, HEAD_DIM), hg_gain.reshape(1, HEAD_DIM),
      cos_t, sin_up, sin_dn, jnp.asarray(tri), jnp.asarray(masks))


def _pack_halves(h):
    k = h.shape[-1] // 2
    lo = lax.bitcast_convert_type(h[:, :k].astype(f32), u32)
    hi = lax.bitcast_convert_type(h[:, k:].astype(f32), u32)
    return (lo >> 16) | (hi & u32(HI_MASK))


def _unpack_halves(u):
    lo = lax.bitcast_convert_type(u << 16, f32)
    hi = lax.bitcast_convert_type(u & u32(HI_MASK), f32)
    return lo, hi


def _outproj_kernel(ana_ref, ahg_ref, x_ref, w0_ref, w1_ref, g1_ref, sh_ref, sc_ref, gain_ref, wr_ref, br_ref,
                    x1_ref, hp_ref, route_ref):
    y = (jnp.dot(ana_ref[...], w0_ref[...], preferred_element_type=f32)
         + jnp.dot(ahg_ref[...], w1_ref[...], preferred_element_type=f32))
    x1 = x_ref[...] + g1_ref[...] * y
    x1_ref[...] = x1
    ms = jnp.mean(x1 * x1, axis=-1, keepdims=True)
    h = x1 * lax.rsqrt(ms + EPS) * gain_ref[...]
    h = h * (1.0 + sc_ref[...]) + sh_ref[...]
    h_hi = h.astype(bf16)
    hp_ref[...] = _pack_halves(h_hi)
    h_lo = (h - h_hi.astype(f32)).astype(bf16)
    t = jnp.dot(h_hi, wr_ref[...], preferred_element_type=f32)
    logits = (t[:, :128] + t[:, 128:] + jnp.dot(h_lo, wr_ref[:, :128], preferred_element_type=f32)) + br_ref[...]
    lane = lax.broadcasted_iota(jnp.int32, logits.shape, 1).astype(f32)
    cur = logits
    vals, idxs = [], []
    for _ in range(TOP_K):
        m = jnp.max(cur, axis=-1, keepdims=True)
        i = jnp.min(jnp.where(cur == m, lane, float(logits.shape[-1])), axis=-1, keepdims=True)
        vals.append(m)
        idxs.append(i)
        cur = jnp.where(lane == i, -jnp.inf, cur)
    es = [jnp.exp(v - vals[0]) for v in vals]
    denom = es[0] + es[1] + es[2] + es[3]
    route = jnp.zeros(logits.shape, f32)
    for k in range(TOP_K):
        route = jnp.where(lane == k, idxs[k], route)
        route = jnp.where(lane == TOP_K + k, es[k] / denom, route)
        route = jnp.where(lane - float(MEMBER_LANE) == idxs[k], 1.0, route)
    route_ref[...] = route


def _outproj(a_na, a_hg, x, w_out_bf, g1, sh2, sc2, gain, w_router, b_router, tm):
    b, l, d = x.shape
    hw = a_na.shape[-1]
    tm = min(tm, l)
    n_e = w_router.shape[-1]
    wr = jnp.zeros((d, 128), f32).at[:, :n_e].set(w_router)
    wr_hi = wr.astype(bf16)
    wr = jnp.concatenate([wr_hi, (wr - wr_hi.astype(f32)).astype(bf16)], axis=1)
    br = jnp.full((1, 128), NEG_BIG, f32).at[0, :n_e].set(b_router)
    row = lambda last: pl.BlockSpec((None, tm, last), lambda bi, mi: (bi, mi, 0))
    vec = pl.BlockSpec((None, 1, d), lambda bi, mi: (bi, 0, 0))
    return pl.pallas_call(
        _outproj_kernel,
        out_shape=(jax.ShapeDtypeStruct((b, l, d), f32),
                   jax.ShapeDtypeStruct((b, l, d // 2), u32),
                   jax.ShapeDtypeStruct((b, l, 128), f32)),
        grid=(b, l // tm),
        in_specs=[row(hw), row(hw), row(d),
                  pl.BlockSpec((hw, d), lambda bi, mi: (0, 0)),
                  pl.BlockSpec((hw, d), lambda bi, mi: (1, 0)),
                  vec, vec, vec,
                  pl.BlockSpec((1, d), lambda bi, mi: (0, 0)),
                  pl.BlockSpec((d, 256), lambda bi, mi: (0, 0)),
                  pl.BlockSpec((1, 128), lambda bi, mi: (0, 0))],
        out_specs=(row(d), row(d // 2), row(128)),
        compiler_params=_cparams(("parallel", "parallel"), 48),
        name="outproj",
    )(a_na, a_hg, x, w_out_bf, w_out_bf, g1, sh2, sc2, gain, wr, br)


CUMSUM_TM = 512


def _cumsum_kernel(route_ref, tri_ref, o_ref, carry):
    @pl.when(pl.program_id(0) == 0)
    def _():
        carry[...] = jnp.zeros_like(carry)

    r = route_ref[...]
    lane = lax.broadcasted_iota(jnp.int32, r.shape, 1)
    member = jnp.where((lane >= MEMBER_LANE) & (lane < MEMBER_LANE + N_EXPERTS), r, 0.0).astype(bf16)
    incl = jnp.dot(tri_ref[...], member, preferred_element_type=f32) + carry[...]
    o_ref[...] = incl
    carry[...] = incl[CUMSUM_TM - 1:CUMSUM_TM, :]


def _token_cumsum(route):
    t = route.shape[0]
    tri = jnp.asarray(np.tril(np.ones((CUMSUM_TM, CUMSUM_TM), np.float32)), bf16)
    return pl.pallas_call(
        _cumsum_kernel,
        out_shape=jax.ShapeDtypeStruct((t, 128), f32),
        grid=(t // CUMSUM_TM,),
        in_specs=[pl.BlockSpec((CUMSUM_TM, 128), lambda i: (i, 0)),
                  pl.BlockSpec((CUMSUM_TM, CUMSUM_TM), lambda i: (0, 0))],
        out_specs=pl.BlockSpec((CUMSUM_TM, 128), lambda i: (i, 0)),
        scratch_shapes=[pltpu.VMEM((1, 128), f32)],
        compiler_params=_cparams(("arbitrary",)),
        name="cumsum",
    )(route, tri)


DISPATCH_TM = 256
SUBLANES = 8


def _dispatch_kernel(pe_ref, dest_ref, hp_ref, xs_ref, zero_scr, sem):
    @pl.when(pl.program_id(0) == 0)
    def _():
        zero_scr[...] = jnp.zeros_like(zero_scr)

        def tail_copy(e):
            start = pl.multiple_of(pe_ref[e] - MOE_TM, MOE_TM)
            return pltpu.make_async_copy(zero_scr, xs_ref.at[pl.ds(start, MOE_TM), :], sem)

        def has_rows(e):
            return pe_ref[e] > jnp.where(e == 0, 0, pe_ref[jnp.maximum(e - 1, 0)])

        def issue_tail(e, c):
            @pl.when(has_rows(e))
            def _():
                tail_copy(e).start()
            return c

        def drain_tail(e, c):
            @pl.when(has_rows(e))
            def _():
                tail_copy(e).wait()
            return c

        lax.fori_loop(0, N_EXPERTS, issue_tail, 0)
        lax.fori_loop(0, N_EXPERTS, drain_tail, 0)

        def dead_copy(m):
            return pltpu.make_async_copy(zero_scr, xs_ref.at[pl.ds(pl.multiple_of(m * MOE_TM, MOE_TM), MOE_TM), :], sem)

        def issue_dead(m, c):
            dead_copy(m).start()
            return c

        def drain_dead(m, c):
            dead_copy(m).wait()
            return c

        n_used = pe_ref[N_EXPERTS - 1] // MOE_TM
        lax.fori_loop(n_used, xs_ref.shape[0] // MOE_TM, issue_dead, 0)
        lax.fori_loop(n_used, xs_ref.shape[0] // MOE_TM, drain_dead, 0)

    def row_copy(i, u, k):
        slot = dest_ref[(i * SUBLANES + u) * TOP_K + k]
        return pltpu.make_async_copy(hp_ref.at[i, pl.ds(u, 1), :], xs_ref.at[pl.ds(slot, 1), :], sem)

    def issue(i, c):
        for u in range(SUBLANES):
            for k in range(TOP_K):
                row_copy(i, u, k).start()
        return c

    def drain(i, c):
        for u in range(SUBLANES):
            for k in range(TOP_K):
                row_copy(i, u, k).wait()
        return c

    lax.fori_loop(0, DISPATCH_TM // SUBLANES, issue, 0)
    lax.fori_loop(0, DISPATCH_TM // SUBLANES, drain, 0)


def _dispatch(pad_end, dest, hp, cap):
    t, kp = hp.shape
    return pl.pallas_call(
        _dispatch_kernel,
        out_shape=jax.ShapeDtypeStruct((cap, kp), hp.dtype),
        grid_spec=pltpu.PrefetchScalarGridSpec(
            num_scalar_prefetch=1,
            grid=(t // DISPATCH_TM,),
            in_specs=[pl.BlockSpec((DISPATCH_TM * TOP_K,), lambda i, pe: (i,), memory_space=pltpu.SMEM),
                      pl.BlockSpec((DISPATCH_TM // SUBLANES, SUBLANES, kp), lambda i, pe: (i, 0, 0))],
            out_specs=pl.BlockSpec(memory_space=pl.ANY),
            scratch_shapes=[pltpu.VMEM((MOE_TM, kp), hp.dtype), pltpu.SemaphoreType.DMA(())]),
        compiler_params=_cparams(("arbitrary",)),
        name="dispatch",
    )(pad_end, dest, hp.reshape(t // SUBLANES, SUBLANES, kp))


def _block_halves(m, half2_ref, nu_ref, out_ref, prepare, half):
    live = m < nu_ref[0]
    second = jnp.logical_and(live, half2_ref[m] == 1)
    zeros = jnp.zeros((MOE_HALF, out_ref.shape[1]), out_ref.dtype)

    @pl.when(live)
    def _():
        prepare()

    @pl.when(second)
    def _():
        half(pl.ds(0, MOE_TM))

    @pl.when(jnp.logical_and(live, jnp.logical_not(second)))
    def _():
        half(pl.ds(0, MOE_HALF))

    @pl.when(jnp.logical_not(live))
    def _():
        out_ref[pl.ds(0, MOE_HALF), :] = zeros

    @pl.when(jnp.logical_not(second))
    def _():
        out_ref[pl.ds(MOE_HALF, MOE_HALF), :] = zeros


def _cast_chunks(w_refs, w_bf):
    rows = w_refs[0].shape[0]
    for i, w_ref in enumerate(w_refs):
        w_bf[pl.ds(i * rows, rows), :] = w_ref[...].astype(bf16)


def _gemm1_kernel(be_ref, first_ref, half2_ref, nu_ref, xs_ref, *refs):
    wg_refs, wl_refs = refs[:W_STREAMS], refs[W_STREAMS:2 * W_STREAMS]
    bg_ref, bl_ref, act_ref, wg_bf, wl_bf = refs[2 * W_STREAMS:]
    m = pl.program_id(1)

    def prepare():
        @pl.when(first_ref[m] == 1)
        def _():
            _cast_chunks(wg_refs, wg_bf)
            _cast_chunks(wl_refs, wl_bf)

    def half(rows):
        lo, hi = _unpack_halves(xs_ref[rows, :])
        lo = lo.astype(bf16)
        hi = hi.astype(bf16)
        k = lo.shape[-1]
        gate = (jnp.dot(lo, wg_bf[:k, :], preferred_element_type=f32)
                + jnp.dot(hi, wg_bf[k:, :], preferred_element_type=f32) + bg_ref[...])
        lin = (jnp.dot(lo, wl_bf[:k, :], preferred_element_type=f32)
               + jnp.dot(hi, wl_bf[k:, :], preferred_element_type=f32) + bl_ref[...])
        x_glu = jnp.minimum(gate, SWIGLU_LIMIT)
        x_lin = jnp.clip(lin, -SWIGLU_LIMIT, SWIGLU_LIMIT)
        act_ref[rows, :] = (x_glu * jax.nn.sigmoid(SWIGLU_ALPHA * x_glu) * (x_lin + 1.0)).astype(act_ref.dtype)

    _block_halves(m, half2_ref, nu_ref, act_ref, prepare, half)


def _gemm1(block_exp, first, half2, n_used, xs, w_gu, b_gu, tn):
    cap, kp = xs.shape
    n_e, d, f2 = w_gu.shape
    ff = f2 // 2
    tn = min(tn, ff)
    nb = cap // MOE_TM
    nj = ff // tn
    live = lambda m, nu: jnp.minimum(m, nu[0] - 1)
    dk = d // W_STREAMS
    wspec = lambda col0, c: pl.BlockSpec((None, dk, tn), lambda j, m, be, fi, h2, nu: (be[m], c, col0 + j))
    return pl.pallas_call(
        _gemm1_kernel,
        out_shape=jax.ShapeDtypeStruct((cap, ff), bf16),
        grid_spec=pltpu.PrefetchScalarGridSpec(
            num_scalar_prefetch=4,
            grid=(nj, nb),
            in_specs=[pl.BlockSpec((MOE_TM, kp), lambda j, m, be, fi, h2, nu: (live(m, nu), 0))]
                     + [wspec(0, c) for c in range(W_STREAMS)] + [wspec(nj, c) for c in range(W_STREAMS)]
                     + [pl.BlockSpec((None, 1, tn), lambda j, m, be, fi, h2, nu: (be[m], 0, j)),
                        pl.BlockSpec((None, 1, tn), lambda j, m, be, fi, h2, nu: (be[m], 0, nj + j))],
            out_specs=pl.BlockSpec((MOE_TM, tn), lambda j, m, be, fi, h2, nu: (m, j)),
            scratch_shapes=[pltpu.VMEM((d, tn), bf16), pltpu.VMEM((d, tn), bf16)]),
        compiler_params=_cparams(("arbitrary", "arbitrary"), 56),
        name="gemm1",
    )(block_exp, first, half2, n_used, xs, *([w_gu] * (2 * W_STREAMS)),
      b_gu.reshape(n_e, 1, f2), b_gu.reshape(n_e, 1, f2))


def _gemm2_kernel(be_ref, first_ref, half2_ref, nu_ref, act_ref, *refs):
    w_refs = refs[:W_STREAMS]
    b_ref, y_ref, w_bf = refs[W_STREAMS:]
    m = pl.program_id(0)

    def prepare():
        @pl.when(first_ref[m] == 1)
        def _():
            _cast_chunks(w_refs, w_bf)

    def half(rows):
        y = jnp.dot(act_ref[rows, :], w_bf[...], preferred_element_type=f32) + b_ref[...]
        y_ref[rows, :] = _pack_halves(y.astype(bf16))

    _block_halves(m, half2_ref, nu_ref, y_ref, prepare, half)


def _gemm2(block_exp, first, half2, n_used, act, w_down, b_down):
    cap, ff = act.shape
    n_e, _, d = w_down.shape
    nb = cap // MOE_TM
    live = lambda m, nu: jnp.minimum(m, nu[0] - 1)
    wspec = lambda c: pl.BlockSpec((None, ff // W_STREAMS, d), lambda m, be, fi, h2, nu: (be[m], c, 0))
    return pl.pallas_call(
        _gemm2_kernel,
        out_shape=jax.ShapeDtypeStruct((cap, d // 2), u32),
        grid_spec=pltpu.PrefetchScalarGridSpec(
            num_scalar_prefetch=4,
            grid=(nb,),
            in_specs=[pl.BlockSpec((MOE_TM, ff), lambda m, be, fi, h2, nu: (live(m, nu), 0))]
                     + [wspec(c) for c in range(W_STREAMS)]
                     + [pl.BlockSpec((None, 1, d), lambda m, be, fi, h2, nu: (be[m], 0, 0))],
            out_specs=pl.BlockSpec((MOE_TM, d // 2), lambda m, be, fi, h2, nu: (m, 0)),
            scratch_shapes=[pltpu.VMEM((ff, d), bf16)]),
        compiler_params=_cparams(("arbitrary",), 56),
        name="gemm2",
    )(block_exp, first, half2, n_used, act, *([w_down] * W_STREAMS), b_down.reshape(n_e, 1, d))


COMBINE_TM = 256


def _combine_kernel(dest_ref, yb_ref, route_ref, x1_ref, g2_ref, gain_ref, o_ref, buf, sem):
    def row_copy(i, u, k):
        slot = dest_ref[(i * SUBLANES + u) * TOP_K + k]
        return pltpu.make_async_copy(yb_ref.at[pl.ds(slot, 1), :], buf.at[k, i, pl.ds(u, 1), :], sem)

    def issue(i, c):
        for u in range(SUBLANES):
            for k in range(TOP_K):
                row_copy(i, u, k).start()
        return c

    def drain(i, c):
        for u in range(SUBLANES):
            for k in range(TOP_K):
                row_copy(i, u, k).wait()
        return c

    lax.fori_loop(0, COMBINE_TM // SUBLANES, issue, 0)
    lax.fori_loop(0, COMBINE_TM // SUBLANES, drain, 0)

    route = route_ref[...]
    acc = None
    for k in range(TOP_K):
        lo, hi = _unpack_halves(buf[k].reshape(COMBINE_TM, buf.shape[-1]))
        term = route[:, TOP_K + k:TOP_K + k + 1] * jnp.concatenate([lo, hi], axis=-1)
        acc = term if acc is None else acc + term
    x2 = x1_ref[...] + g2_ref[...] * acc
    ms = jnp.mean(x2 * x2, axis=-1, keepdims=True)
    o_ref[...] = x2 * lax.rsqrt(ms + EPS) * gain_ref[...]


def _combine(dest, yb, route, x1, g2, gain):
    b, l, d = x1.shape
    tm = min(COMBINE_TM, l)
    assert tm == COMBINE_TM
    per_b = l // tm
    return pl.pallas_call(
        _combine_kernel,
        out_shape=jax.ShapeDtypeStruct((b, l, d), f32),
        grid=(b * per_b,),
        in_specs=[pl.BlockSpec((tm * TOP_K,), lambda i: (i,), memory_space=pltpu.SMEM),
                  pl.BlockSpec(memory_space=pl.ANY),
                  pl.BlockSpec((None, tm, 128), lambda i: (i // per_b, i % per_b, 0)),
                  pl.BlockSpec((None, tm, d), lambda i: (i // per_b, i % per_b, 0)),
                  pl.BlockSpec((None, 1, d), lambda i: (i // per_b, 0, 0)),
                  pl.BlockSpec((1, d), lambda i: (0, 0))],
        out_specs=pl.BlockSpec((None, tm, d), lambda i: (i // per_b, i % per_b, 0)),
        scratch_shapes=[pltpu.VMEM((TOP_K, tm // SUBLANES, SUBLANES, d // 2), u32), pltpu.SemaphoreType.DMA(())],
        compiler_params=_cparams(("arbitrary",), 40),
        name="combine",
    )(dest, yb, route, x1, g2, gain)


def _routing(route, n_blocks):
    experts = slice(MEMBER_LANE, MEMBER_LANE + N_EXPERTS)
    top_idx = route[:, :TOP_K].astype(jnp.int32)
    incl = _token_cumsum(route)[:, experts]
    rank = (incl - route[:, experts]).astype(jnp.int32)
    counts = incl[-1].astype(jnp.int32)
    padded = (counts + MOE_TM - 1) // MOE_TM * MOE_TM
    pad_end = jnp.cumsum(padded).astype(jnp.int32)
    slot0 = (pad_end - padded)[None, :] + rank
    onehot = top_idx[:, :, None] == jnp.arange(N_EXPERTS, dtype=jnp.int32)[None, None, :]
    dest = jnp.sum(jnp.where(onehot, slot0[:, None, :], 0), axis=-1).astype(jnp.int32)
    block_start = jnp.arange(n_blocks, dtype=jnp.int32) * MOE_TM
    block_exp = jnp.minimum(jnp.sum(pad_end[None, :] <= block_start[:, None], axis=1), N_EXPERTS - 1).astype(jnp.int32)
    first = jnp.concatenate([jnp.ones((1,), jnp.int32), (block_exp[1:] != block_exp[:-1]).astype(jnp.int32)])
    n_used = (pad_end[-1:] // MOE_TM).astype(jnp.int32)
    token_end = (pad_end - padded + counts)[block_exp]
    half2 = (token_end > block_start + MOE_HALF).astype(jnp.int32)
    return dest.reshape(-1), pad_end, block_exp, first, half2, n_used


def _moe(hp, route, x1, g2, norm_final, w_gu, b_gu, w_down, b_down):
    b, l, d = x1.shape
    t = b * l
    n_blocks = -(-(t * TOP_K + N_EXPERTS * (MOE_TM - 1)) // MOE_TM)
    dest, pad_end, block_exp, first, half2, n_used = _routing(route.reshape(t, 128), n_blocks)
    xs = _dispatch(pad_end, dest, hp.reshape(t, d // 2), n_blocks * MOE_TM)
    act = _gemm1(block_exp, first, half2, n_used, xs, w_gu, b_gu, 1024)
    yb = _gemm2(block_exp, first, half2, n_used, act, w_down, b_down)
    return _combine(dest, yb, route, x1, g2, norm_final.reshape(1, d))


def kernel(x, c, ctx, c_ctx, w_ada, b_ada, norm_mix, norm_ffn, w_in, lb_table, hg_norm, rpb, w_out, w_router,
           b_router, w_gu, b_gu, w_down, b_down, norm_final):
    b, l, d = x.shape
    assert w_ada.shape[0] == 1, "single-layer block"
    rows = l // GRID_W

    c16 = jnp.zeros((16, d), f32).at[:b].set(c).at[b].set(c_ctx)
    mod = _ada(c16, w_ada[0], b_ada[0])
    sh1, sc1, g1, sh2, sc2, g2 = [mod[:b, i * d:(i + 1) * d].reshape(b, 1, d) for i in range(6)]
    csh = jnp.broadcast_to(mod[b, :d].reshape(1, 1, d), (b, 1, d))
    csc = jnp.broadcast_to(mod[b, d:2 * d].reshape(1, 1, d), (b, 1, d))

    w_in_bf = w_in[0].astype(bf16)
    gain_mix = norm_mix[0].reshape(1, d)
    px_att = _inproj(x, sh1, sc1, gain_mix, w_in_bf, P_QN, 3, bf16, 1024)
    px_hg = _inproj(x, sh1, sc1, gain_mix, w_in_bf, P_QH, 5, f32, 1024)
    pc_att = _inproj(ctx, csh, csc, gain_mix, w_in_bf, P_KN, 2, bf16, 256)
    pc_hg = _inproj(ctx, csh, csc, gain_mix, w_in_bf, P_FF, 3, f32, 256)

    o_na = _natten(px_att, pc_att, _natten_bias_table(rpb[0], rows))

    lower_bounds = jnp.cumsum(jax.nn.softmax(lb_table.astype(f32), axis=0), axis=0)
    o_hg = _hgrn(px_hg, pc_hg, lower_bounds[0].reshape(2, HEADS * HEAD_DIM), hg_norm[0])

    x1, hp, route = _outproj(o_na, o_hg, x, w_out[0].astype(bf16), g1, sh2, sc2, norm_ffn[0].reshape(1, d),
                             w_router[0], b_router[0], 256)
    return _moe(hp, route, x1, g2, norm_final, w_gu[0], b_gu[0], w_down[0], b_down[0])
```

```python
import functools

import numpy as np
import jax
import jax.numpy as jnp
from jax import lax
from jax.experimental import pallas as pl
from jax.experimental.pallas import tpu as pltpu

f32 = jnp.float32
bf16 = jnp.bfloat16
u32 = jnp.uint32

GRID_W = 64
HEADS = 8
HEAD_DIM = 128
WIN_R = 8
WIN_C = 16
ROPE_THETA = 10000.0
N_EXPERTS = 32
TOP_K = 4
SWIGLU_LIMIT = 7.0
SWIGLU_ALPHA = 1.702
EPS = 1e-6
P_QN, P_KN, P_VN, P_QH, P_FF, P_FB, P_IH, P_GH = range(8)

CHUNK = 64
SUB = 16
CHUNK_UNROLL = 8
ROW_UNROLL = 4
MOE_TM = 512
MOE_HALF = 256
W_STREAMS = 1
NEG_BIG = -1e30
MEMBER_LANE = 32
HI_MASK = 0xFFFF0000

_HIGHEST = lax.Precision.HIGHEST


def _cparams(sem, vmem_mb=None):
    kw = dict(dimension_semantics=sem)
    if vmem_mb is not None:
        kw["vmem_limit_bytes"] = vmem_mb * 1024 * 1024
    return pltpu.CompilerParams(**kw)


def _ada_kernel(c_ref, w_ref, b_ref, o_ref):
    c = c_ref[...]
    cond = c * jax.nn.sigmoid(c)
    o_ref[...] = jnp.dot(cond, w_ref[...], precision=_HIGHEST, preferred_element_type=f32) + b_ref[...]


def _ada(c16, w_ada, b_ada):
    d, n = w_ada.shape
    tn = min(1024, n)
    return pl.pallas_call(
        _ada_kernel,
        out_shape=jax.ShapeDtypeStruct((c16.shape[0], n), f32),
        grid=(n // tn,),
        in_specs=[pl.BlockSpec((c16.shape[0], d), lambda j: (0, 0)),
                  pl.BlockSpec((d, tn), lambda j: (0, j)),
                  pl.BlockSpec((1, tn), lambda j: (0, j))],
        out_specs=pl.BlockSpec((c16.shape[0], tn), lambda j: (0, j)),
        compiler_params=_cparams(("parallel",), 40),
        name="ada",
    )(c16, w_ada, b_ada.reshape(1, n))


def _inproj_kernel(x_ref, shift_ref, scale_ref, gain_ref, w_ref, o_ref, h_scr):
    @pl.when(pl.program_id(2) == 0)
    def _():
        x = x_ref[...]
        ms = jnp.mean(x * x, axis=-1, keepdims=True)
        y = x * lax.rsqrt(ms + EPS) * gain_ref[...]
        h_scr[...] = (y * (1.0 + scale_ref[...]) + shift_ref[...]).astype(bf16)

    acc = jnp.dot(h_scr[...], w_ref[...], preferred_element_type=f32)
    for hh in range(HEADS):
        o_ref[hh] = acc[:, hh * HEAD_DIM:(hh + 1) * HEAD_DIM].astype(o_ref.dtype)


def _inproj(x, shift, scale, gain, w_bf, part_lo, n_parts, out_dtype, tm):
    b, l, d = x.shape
    pw = HEADS * HEAD_DIM
    tm = min(tm, l)
    return pl.pallas_call(
        _inproj_kernel,
        out_shape=jax.ShapeDtypeStruct((b, n_parts * HEADS, l, HEAD_DIM), out_dtype),
        grid=(b, l // tm, n_parts),
        in_specs=[pl.BlockSpec((None, tm, d), lambda bi, mi, ni: (bi, mi, 0)),
                  pl.BlockSpec((None, 1, d), lambda bi, mi, ni: (bi, 0, 0)),
                  pl.BlockSpec((None, 1, d), lambda bi, mi, ni: (bi, 0, 0)),
                  pl.BlockSpec((1, d), lambda bi, mi, ni: (0, 0)),
                  pl.BlockSpec((d, pw), lambda bi, mi, ni: (0, part_lo + ni))],
        out_specs=pl.BlockSpec((None, HEADS, tm, HEAD_DIM), lambda bi, mi, ni: (bi, ni, mi, 0)),
        scratch_shapes=[pltpu.VMEM((tm, d), bf16)],
        compiler_params=_cparams(("parallel", "parallel", "arbitrary"), 48),
        name="inproj",
    )(x, shift, scale, gain, w_bf)


def _natten_bias_table(rpb, rows):
    kr = min(WIN_R, rows)
    q = np.arange(GRID_W)
    col_start = np.clip(q - WIN_C // 2, 0, GRID_W - WIN_C)
    kc = np.arange(GRID_W)
    in_win = (kc[None, :] >= col_start[:, None]) & (kc[None, :] < col_start[:, None] + WIN_C)
    d_col = np.clip(kc[None, :] - q[:, None] + WIN_C - 1, 0, 2 * WIN_C - 2)
    n_d0 = 2 * WIN_R - 1 - (kr - 1)
    onehot = (d_col[None] == np.arange(2 * WIN_C - 1)[:, None, None]).astype(np.float32)
    cols = jnp.einsum("hrc,cqk->hrqk", rpb.astype(f32), jnp.asarray(onehot), precision=_HIGHEST)
    cols = jnp.where(in_win[None, None], cols, -jnp.inf)
    t = jnp.stack([cols[:, d0:d0 + kr] for d0 in range(n_d0)], axis=1)
    t = jnp.transpose(t, (0, 1, 3, 2, 4))
    return t.reshape(rpb.shape[0], n_d0, GRID_W, kr * GRID_W)


def _natten_kernel(q_ref, k_ref, v_ref, kc_ref, vc_ref, bias_ref, o_ref, sw0, sc0, sw1, sc1, *, rows, kr):
    scale = HEAD_DIM ** -0.5
    kc = kc_ref[...]
    vc = vc_ref[...]
    nt = (((1,), (1,)), ((), ()))

    n_groups = rows // ROW_UNROLL
    gq = ROW_UNROLL * GRID_W

    def key_start(r):
        return jnp.clip(r - kr // 2, 0, rows - kr)

    def scores(g, sw_scr, sc_scr):
        q_all = q_ref[pl.ds(pl.multiple_of(g * gq, gq), gq), :]
        sc_scr[...] = lax.dot_general(q_all, kc, nt, preferred_element_type=f32) * scale
        for u in range(ROW_UNROLL):
            r = g * ROW_UNROLL + u
            kr0 = key_start(r)
            kw = k_ref[pl.ds(pl.multiple_of(kr0 * GRID_W, GRID_W), kr * GRID_W), :]
            q = q_all[u * GRID_W:(u + 1) * GRID_W, :]
            sw_scr[u] = (lax.dot_general(q, kw, nt, preferred_element_type=f32) * scale
                         + bias_ref[kr0 - r + WIN_R - 1])

    def attend(g, sw_scr, sc_scr):
        s_c = sc_scr[...]
        s_w = [sw_scr[u] for u in range(ROW_UNROLL)]
        m = jnp.maximum(jnp.concatenate([jnp.max(s, axis=-1, keepdims=True) for s in s_w], axis=0),
                        jnp.max(s_c, axis=-1, keepdims=True))
        p_c = jnp.exp(s_c - m)
        p_w = [jnp.exp(s - m[u * GRID_W:(u + 1) * GRID_W, :]) for u, s in enumerate(s_w)]
        denom = (jnp.concatenate([jnp.sum(p, axis=-1, keepdims=True) for p in p_w], axis=0)
                 + jnp.sum(p_c, axis=-1, keepdims=True))
        o_c = jnp.dot(p_c.astype(bf16), vc, preferred_element_type=f32)
        o_w = []
        for u, p in enumerate(p_w):
            k0 = pl.multiple_of(key_start(g * ROW_UNROLL + u) * GRID_W, GRID_W)
            o_w.append(jnp.dot(p.astype(bf16), v_ref[pl.ds(k0, kr * GRID_W), :], preferred_element_type=f32))
        o = (jnp.concatenate(o_w, axis=0) + o_c) / denom
        o_ref[pl.ds(pl.multiple_of(g * gq, gq), gq), :] = o.astype(o_ref.dtype)

    scores(0, sw0, sc0)

    def body(i, carry):
        g = 2 * i
        scores(g + 1, sw1, sc1)
        attend(g, sw0, sc0)
        scores(jnp.minimum(g + 2, n_groups - 1), sw0, sc0)
        attend(g + 1, sw1, sc1)
        return carry

    lax.fori_loop(0, n_groups // 2, body, 0)


def _natten(px_att, pc_att, bias_tab):
    b, _, l, _ = px_att.shape
    lc = pc_att.shape[2]
    rows = l // GRID_W
    kr = min(WIN_R, rows)
    n_d0 = bias_tab.shape[1]
    blk = lambda off: pl.BlockSpec((None, None, l, HEAD_DIM), lambda h, bi: (bi, off + h, 0, 0))
    cblk = lambda off: pl.BlockSpec((None, None, lc, HEAD_DIM), lambda h, bi: (bi, off + h, 0, 0))
    return pl.pallas_call(
        functools.partial(_natten_kernel, rows=rows, kr=kr),
        out_shape=jax.ShapeDtypeStruct((b, l, HEADS * HEAD_DIM), bf16),
        grid=(HEADS, b),
        in_specs=[blk(0), blk(HEADS), blk(2 * HEADS), cblk(0), cblk(HEADS),
                  pl.BlockSpec((None, n_d0, GRID_W, kr * GRID_W), lambda h, bi: (h, 0, 0, 0))],
        out_specs=pl.BlockSpec((None, l, HEAD_DIM), lambda h, bi: (bi, 0, h)),
        scratch_shapes=[pltpu.VMEM((ROW_UNROLL, GRID_W, kr * GRID_W), f32), pltpu.VMEM((ROW_UNROLL * GRID_W, lc), f32),
                        pltpu.VMEM((ROW_UNROLL, GRID_W, kr * GRID_W), f32), pltpu.VMEM((ROW_UNROLL * GRID_W, lc), f32)],
        compiler_params=_cparams(("parallel", "parallel")),
        name="natten",
    )(px_att, px_att, px_att, pc_att, pc_att, bias_tab)


def _hgrn_consts():
    t = np.arange(CHUNK)
    bt, bs = t[:, None] // SUB, t[None, :] // SUB
    tri, masks = [], []
    for sgn in (1, -1):
        before = (t[None, :] <= t[:, None]) if sgn == 1 else (t[None, :] >= t[:, None])
        tri.append(before.astype(np.float32))
        dist = (bt - bs) * sgn
        masks.append(np.stack([dist == 1, dist == 2, dist == 3, (dist == 0) & before]).astype(np.float32))
    return np.stack(tri), np.stack(masks)


def _rope_tables(l):
    t = jnp.arange(l)
    n_freq = HEAD_DIM // 4
    inv_freq = ROPE_THETA ** (-jnp.arange(n_freq, dtype=f32) / n_freq)
    ang_row = (t // GRID_W).astype(f32)[:, None] * inv_freq
    ang_col = (t % GRID_W).astype(f32)[:, None] * inv_freq
    cr, sr, cc, sc = jnp.cos(ang_row), jnp.sin(ang_row), jnp.cos(ang_col), jnp.sin(ang_col)
    z = jnp.zeros_like(sr)
    cos_t = jnp.concatenate([cr, cr, cc, cc], axis=-1)
    sin_up = jnp.concatenate([-sr, z, -sc, z], axis=-1)
    sin_dn = jnp.concatenate([z, sr, z, sc], axis=-1)
    return cos_t, sin_up, sin_dn


def _split3_dot(tri_bf, g):
    g1 = g.astype(bf16)
    r1 = g - g1.astype(f32)
    g2 = r1.astype(bf16)
    g3 = (r1 - g2.astype(f32)).astype(bf16)
    dot = lambda a: jnp.dot(tri_bf, a, preferred_element_type=f32)
    return dot(g1) + dot(g2) + dot(g3)


def _gate(f_raw, lb):
    log_f = jnp.log(lb + (1.0 - lb) * jax.nn.sigmoid(f_raw))
    key = (1.0 - lb) * jax.nn.sigmoid(-f_raw)
    return log_f, key


def _chunk_refs(cum, backward):
    nb = CHUNK // SUB
    if backward:
        ends = [cum[i * SUB:i * SUB + 1, :] for i in range(nb)]
        order = list(range(nb - 1, -1, -1))
    else:
        ends = [cum[i * SUB + SUB - 1:i * SUB + SUB, :] for i in range(nb)]
        order = list(range(nb))
    zero = jnp.zeros_like(ends[0])
    b_rows, g2_rows, g3_rows = [None] * nb, [None] * nb, [None] * nb
    for pos, i in enumerate(order):
        b_i = zero if pos == 0 else ends[order[pos - 1]]
        b_rows[i] = b_i
        g2_rows[i] = b_i - ends[order[pos - 2]] if pos >= 2 else zero
        g3_rows[i] = b_i - ends[order[pos - 3]] if pos >= 3 else zero
    expand = lambda rws: jnp.concatenate([jnp.broadcast_to(r, (SUB, HEAD_DIM)) for r in rws], axis=0)
    total = ends[order[-1]]
    return expand(b_rows), expand(ends), expand(g2_rows), expand(g3_rows), total


def _rope(x, cos_t, sin_up, sin_dn):
    return x * cos_t + pltpu.roll(x, 96, 1) * sin_up + pltpu.roll(x, 32, 1) * sin_dn


def _scan_group(states, items, tris, masks):
    nt = (((1,), (1,)), ((), ()))
    tn = (((0,), (0,)), ((), ()))
    pre = []
    for d, q, f_raw, v, lb, rope in items:
        g, key = _gate(f_raw, lb)
        pre.append((g, _rope(q, *rope), _rope(key, *rope), v.astype(bf16)))
    cums = [_split3_dot(tris[it[0]], p[0]) for it, p in zip(items, pre)]
    ops = []
    for it, (g, qr, kr, v_bf), cum in zip(items, pre, cums):
        b, e, gap2, gap3, total = _chunk_refs(cum, it[0] == 1)
        q_t = qr * jnp.exp(cum - b)
        k_hat = kr * jnp.exp(e - cum)
        k_til = kr * jnp.exp(b - cum)
        lhs = jnp.concatenate([q_t, q_t * jnp.exp(gap2), q_t * jnp.exp(gap3)], axis=0).astype(bf16)
        rhs = jnp.concatenate([k_hat, k_til], axis=0).astype(bf16)
        q_in = (q_t * jnp.exp(b)).astype(bf16)
        k_dec = (k_hat * jnp.exp(total - e)).astype(bf16)
        ops.append((lhs, rhs, q_in, k_dec, jnp.exp(total)))
    scores = [lax.dot_general(o[0], o[1], nt, preferred_element_type=f32) for o in ops]
    upds = [lax.dot_general(p[3], o[3], tn, preferred_element_type=f32) for p, o in zip(pre, ops)]
    intra = []
    for it, p, (g, qr, kr, v_bf) in zip(items, scores, pre):
        m = masks[it[0]]
        att = (jnp.where(m[0] > 0, p[0:CHUNK, 0:CHUNK], 0.0)
               + jnp.where(m[1] > 0, p[CHUNK:2 * CHUNK, 0:CHUNK], 0.0)
               + jnp.where(m[2] > 0, p[2 * CHUNK:3 * CHUNK, 0:CHUNK], 0.0)
               + jnp.where(m[3] > 0, p[0:CHUNK, CHUNK:2 * CHUNK], 0.0))
        intra.append(jnp.dot(att.astype(bf16), v_bf, preferred_element_type=f32))
    states = list(states)
    outs = []
    for it, o, upd, o_in in zip(items, ops, upds, intra):
        st = states[it[0]]
        outs.append(o_in + lax.dot_general(o[2], st.astype(bf16), nt, preferred_element_type=f32))
        states[it[0]] = st * o[4] + upd
    return states, outs


def _hgrn_kernel(q_ref, ff_ref, fb_ref, v_ref, gate_ref, cff_ref, cfb_ref, cv_ref, lb_ref, gain_ref,
                 cos_ref, sup_ref, sdn_ref, tri_ref, mask_ref, o_ref, of_scr, ob_scr, *, n_chunks, n_cchunks):
    lb_f = lb_ref[0]
    lb_b = lb_ref[1]
    tri_f = tri_ref[0].astype(bf16)
    tri_b = tri_ref[1].astype(bf16)
    zero = jnp.zeros((HEAD_DIM, HEAD_DIM), f32)

    ctx_items = []
    for c in range(n_cchunks):
        rf = pl.ds(c * CHUNK, CHUNK)
        rb = pl.ds((n_cchunks - 1 - c) * CHUNK, CHUNK)
        ctx_items.append((0, cv_ref[rf, :], cff_ref[rf, :], lb_f))
        ctx_items.append((1, cv_ref[rb, :], cfb_ref[rb, :], lb_b))
    st_f, st_b = _state_group([zero, zero], ctx_items, (tri_f, tri_b))

    def body(c, carry):
        masks = [[mask_ref[d, i] for i in range(4)] for d in range(2)]
        items, rows = [], []
        for u in range(CHUNK_UNROLL):
            cf = c * CHUNK_UNROLL + u
            rf = pl.ds(pl.multiple_of(cf * CHUNK, CHUNK), CHUNK)
            rb = pl.ds(pl.multiple_of((n_chunks - 1 - cf) * CHUNK, CHUNK), CHUNK)
            items.append((0, q_ref[rf, :], ff_ref[rf, :], v_ref[rf, :], lb_f,
                          (cos_ref[rf, :], sup_ref[rf, :], sdn_ref[rf, :])))
            items.append((1, q_ref[rb, :], fb_ref[rb, :], v_ref[rb, :], lb_b,
                          (cos_ref[rb, :], sup_ref[rb, :], sdn_ref[rb, :])))
            rows += [(of_scr, rf), (ob_scr, rb)]
        states, outs = _scan_group(carry, items, (tri_f, tri_b), masks)
        for (scr, rws), o in zip(rows, outs):
            scr[rws, :] = o
        return tuple(states)

    lax.fori_loop(0, n_chunks // CHUNK_UNROLL, body, (st_f, st_b))

    o = of_scr[...] + ob_scr[...]
    y = o * lax.rsqrt(jnp.mean(o * o, axis=-1, keepdims=True) + EPS) * gain_ref[...]
    gate = gate_ref[...]
    o_ref[...] = (y * (gate * jax.nn.sigmoid(gate))).astype(o_ref.dtype)


def _state_group(states, items, tris):
    gates = [_gate(f_raw, lb) for _, _, f_raw, lb in items]
    cums = [_split3_dot(tris[it[0]], g) for it, (g, _) in zip(items, gates)]
    terms = []
    for it, (_, key), cum in zip(items, gates, cums):
        total = cum[0:1, :] if it[0] == 1 else cum[CHUNK - 1:CHUNK, :]
        terms.append(((key * jnp.exp(total - cum)).astype(bf16), jnp.exp(total)))
    upds = [lax.dot_general(it[1].astype(bf16), k_dec, (((0,), (0,)), ((), ())), preferred_element_type=f32)
            for it, (k_dec, _) in zip(items, terms)]
    states = list(states)
    for it, (_, decay), upd in zip(items, terms, upds):
        states[it[0]] = states[it[0]] * decay + upd
    return states


def _hgrn(px_hg, pc_hg, lb2, hg_gain):
    b, _, l, _ = px_hg.shape
    lc = pc_hg.shape[2]
    cos_t, sin_up, sin_dn = _rope_tables(l)
    tri, masks = _hgrn_consts()
    blk = lambda off: pl.BlockSpec((None, None, l, HEAD_DIM), lambda h, bi: (bi, off + h, 0, 0))
    cblk = lambda off: pl.BlockSpec((None, None, lc, HEAD_DIM), lambda h, bi: (bi, off + h, 0, 0))
    full = lambda shp: pl.BlockSpec(shp, lambda h, bi: (0,) * len(shp))
    return pl.pallas_call(
        functools.partial(_hgrn_kernel, n_chunks=l // CHUNK, n_cchunks=lc // CHUNK),
        out_shape=jax.ShapeDtypeStruct((b, l, HEADS * HEAD_DIM), bf16),
        grid=(HEADS, b),
        in_specs=[blk(0), blk(HEADS), blk(2 * HEADS), blk(3 * HEADS), blk(4 * HEADS),
                  cblk(0), cblk(HEADS), cblk(2 * HEADS),
                  pl.BlockSpec((2, None, 1, HEAD_DIM), lambda h, bi: (0, h, 0, 0)),
                  full((1, HEAD_DIM)),
                  full((l, HEAD_DIM)), full((l, HEAD_DIM)), full((l, HEAD_DIM)),
                  full((2, CHUNK, CHUNK)), full((2, 4, CHUNK, CHUNK))],
        out_specs=pl.BlockSpec((None, l, HEAD_DIM), lambda h, bi: (bi, 0, h)),
        scratch_shapes=[pltpu.VMEM((l, HEAD_DIM), f32), pltpu.VMEM((l, HEAD_DIM), f32)],
        compiler_params=_cparams(("parallel", "parallel")),
        name="hgrn",
    )(px_hg, px_hg, px_hg, px_hg, px_hg, pc_hg, pc_hg, pc_hg,
      lb2.reshape(2, HEADS, 1, HEAD_DIM), hg_gain.reshape(1, HEAD_DIM),
      cos_t, sin_up, sin_dn, jnp.asarray(tri), jnp.asarray(masks))


def _pack_halves(h):
    k = h.shape[-1] // 2
    lo = lax.bitcast_convert_type(h[:, :k].astype(f32), u32)
    hi = lax.bitcast_convert_type(h[:, k:].astype(f32), u32)
    return (lo >> 16) | (hi & u32(HI_MASK))


def _unpack_halves(u):
    lo = lax.bitcast_convert_type(u << 16, f32)
    hi = lax.bitcast_convert_type(u & u32(HI_MASK), f32)
    return lo, hi


def _outproj_kernel(ana_ref, ahg_ref, x_ref, w0_ref, w1_ref, g1_ref, sh_ref, sc_ref, gain_ref, wr_ref, br_ref,
                    x1_ref, hp_ref, route_ref):
    y = (jnp.dot(ana_ref[...], w0_ref[...], preferred_element_type=f32)
         + jnp.dot(ahg_ref[...], w1_ref[...], preferred_element_type=f32))
    x1 = x_ref[...] + g1_ref[...] * y
    x1_ref[...] = x1
    ms = jnp.mean(x1 * x1, axis=-1, keepdims=True)
    h = x1 * lax.rsqrt(ms + EPS) * gain_ref[...]
    h = h * (1.0 + sc_ref[...]) + sh_ref[...]
    h_hi = h.astype(bf16)
    hp_ref[...] = _pack_halves(h_hi)
    h_lo = (h - h_hi.astype(f32)).astype(bf16)
    t = jnp.dot(h_hi, wr_ref[...], preferred_element_type=f32)
    logits = (t[:, :128] + t[:, 128:] + jnp.dot(h_lo, wr_ref[:, :128], preferred_element_type=f32)) + br_ref[...]
    lane = lax.broadcasted_iota(jnp.int32, logits.shape, 1).astype(f32)
    cur = logits
    vals, idxs = [], []
    for _ in range(TOP_K):
        m = jnp.max(cur, axis=-1, keepdims=True)
        i = jnp.min(jnp.where(cur == m, lane, float(logits.shape[-1])), axis=-1, keepdims=True)
        vals.append(m)
        idxs.append(i)
        cur = jnp.where(lane == i, -jnp.inf, cur)
    es = [jnp.exp(v - vals[0]) for v in vals]
    denom = es[0] + es[1] + es[2] + es[3]
    route = jnp.zeros(logits.shape, f32)
    for k in range(TOP_K):
        route = jnp.where(lane == k, idxs[k], route)
        route = jnp.where(lane == TOP_K + k, es[k] / denom, route)
        route = jnp.where(lane - float(MEMBER_LANE) == idxs[k], 1.0, route)
    route_ref[...] = route


def _outproj(a_na, a_hg, x, w_out_bf, g1, sh2, sc2, gain, w_router, b_router, tm):
    b, l, d = x.shape
    hw = a_na.shape[-1]
    tm = min(tm, l)
    n_e = w_router.shape[-1]
    wr = jnp.zeros((d, 128), f32).at[:, :n_e].set(w_router)
    wr_hi = wr.astype(bf16)
    wr = jnp.concatenate([wr_hi, (wr - wr_hi.astype(f32)).astype(bf16)], axis=1)
    br = jnp.full((1, 128), NEG_BIG, f32).at[0, :n_e].set(b_router)
    row = lambda last: pl.BlockSpec((None, tm, last), lambda bi, mi: (bi, mi, 0))
    vec = pl.BlockSpec((None, 1, d), lambda bi, mi: (bi, 0, 0))
    return pl.pallas_call(
        _outproj_kernel,
        out_shape=(jax.ShapeDtypeStruct((b, l, d), f32),
                   jax.ShapeDtypeStruct((b, l, d // 2), u32),
                   jax.ShapeDtypeStruct((b, l, 128), f32)),
        grid=(b, l // tm),
        in_specs=[row(hw), row(hw), row(d),
                  pl.BlockSpec((hw, d), lambda bi, mi: (0, 0)),
                  pl.BlockSpec((hw, d), lambda bi, mi: (1, 0)),
                  vec, vec, vec,
                  pl.BlockSpec((1, d), lambda bi, mi: (0, 0)),
                  pl.BlockSpec((d, 256), lambda bi, mi: (0, 0)),
                  pl.BlockSpec((1, 128), lambda bi, mi: (0, 0))],
        out_specs=(row(d), row(d // 2), row(128)),
        compiler_params=_cparams(("parallel", "parallel"), 48),
        name="outproj",
    )(a_na, a_hg, x, w_out_bf, w_out_bf, g1, sh2, sc2, gain, wr, br)


CUMSUM_TM = 512


def _cumsum_kernel(route_ref, tri_ref, o_ref, carry):
    @pl.when(pl.program_id(0) == 0)
    def _():
        carry[...] = jnp.zeros_like(carry)

    r = route_ref[...]
    lane = lax.broadcasted_iota(jnp.int32, r.shape, 1)
    member = jnp.where((lane >= MEMBER_LANE) & (lane < MEMBER_LANE + N_EXPERTS), r, 0.0).astype(bf16)
    incl = jnp.dot(tri_ref[...], member, preferred_element_type=f32) + carry[...]
    o_ref[...] = incl
    carry[...] = incl[CUMSUM_TM - 1:CUMSUM_TM, :]


def _token_cumsum(route):
    t = route.shape[0]
    tri = jnp.asarray(np.tril(np.ones((CUMSUM_TM, CUMSUM_TM), np.float32)), bf16)
    return pl.pallas_call(
        _cumsum_kernel,
        out_shape=jax.ShapeDtypeStruct((t, 128), f32),
        grid=(t // CUMSUM_TM,),
        in_specs=[pl.BlockSpec((CUMSUM_TM, 128), lambda i: (i, 0)),
                  pl.BlockSpec((CUMSUM_TM, CUMSUM_TM), lambda i: (0, 0))],
        out_specs=pl.BlockSpec((CUMSUM_TM, 128), lambda i: (i, 0)),
        scratch_shapes=[pltpu.VMEM((1, 128), f32)],
        compiler_params=_cparams(("arbitrary",)),
        name="cumsum",
    )(route, tri)


DISPATCH_TM = 256
SUBLANES = 8


def _dispatch_kernel(pe_ref, dest_ref, hp_ref, xs_ref, zero_scr, sem):
    @pl.when(pl.program_id(0) == 0)
    def _():
        zero_scr[...] = jnp.zeros_like(zero_scr)

        def tail_copy(e):
            start = pl.multiple_of(pe_ref[e] - MOE_TM, MOE_TM)
            return pltpu.make_async_copy(zero_scr, xs_ref.at[pl.ds(start, MOE_TM), :], sem)

        def has_rows(e):
            return pe_ref[e] > jnp.where(e == 0, 0, pe_ref[jnp.maximum(e - 1, 0)])

        def issue_tail(e, c):
            @pl.when(has_rows(e))
            def _():
                tail_copy(e).start()
            return c

        def drain_tail(e, c):
            @pl.when(has_rows(e))
            def _():
                tail_copy(e).wait()
            return c

        lax.fori_loop(0, N_EXPERTS, issue_tail, 0)
        lax.fori_loop(0, N_EXPERTS, drain_tail, 0)

        def dead_copy(m):
            return pltpu.make_async_copy(zero_scr, xs_ref.at[pl.ds(pl.multiple_of(m * MOE_TM, MOE_TM), MOE_TM), :], sem)

        def issue_dead(m, c):
            dead_copy(m).start()
            return c

        def drain_dead(m, c):
            dead_copy(m).wait()
            return c

        n_used = pe_ref[N_EXPERTS - 1] // MOE_TM
        lax.fori_loop(n_used, xs_ref.shape[0] // MOE_TM, issue_dead, 0)
        lax.fori_loop(n_used, xs_ref.shape[0] // MOE_TM, drain_dead, 0)

    def row_copy(i, u, k):
        slot = dest_ref[(i * SUBLANES + u) * TOP_K + k]
        return pltpu.make_async_copy(hp_ref.at[i, pl.ds(u, 1), :], xs_ref.at[pl.ds(slot, 1), :], sem)

    def issue(i, c):
        for u in range(SUBLANES):
            for k in range(TOP_K):
                row_copy(i, u, k).start()
        return c

    def drain(i, c):
        for u in range(SUBLANES):
            for k in range(TOP_K):
                row_copy(i, u, k).wait()
        return c

    lax.fori_loop(0, DISPATCH_TM // SUBLANES, issue, 0)
    lax.fori_loop(0, DISPATCH_TM // SUBLANES, drain, 0)


def _dispatch(pad_end, dest, hp, cap):
    t, kp = hp.shape
    return pl.pallas_call(
        _dispatch_kernel,
        out_shape=jax.ShapeDtypeStruct((cap, kp), hp.dtype),
        grid_spec=pltpu.PrefetchScalarGridSpec(
            num_scalar_prefetch=1,
            grid=(t // DISPATCH_TM,),
            in_specs=[pl.BlockSpec((DISPATCH_TM * TOP_K,), lambda i, pe: (i,), memory_space=pltpu.SMEM),
                      pl.BlockSpec((DISPATCH_TM // SUBLANES, SUBLANES, kp), lambda i, pe: (i, 0, 0))],
            out_specs=pl.BlockSpec(memory_space=pl.ANY),
            scratch_shapes=[pltpu.VMEM((MOE_TM, kp), hp.dtype), pltpu.SemaphoreType.DMA(())]),
        compiler_params=_cparams(("arbitrary",)),
        name="dispatch",
    )(pad_end, dest, hp.reshape(t // SUBLANES, SUBLANES, kp))


def _block_halves(m, half2_ref, nu_ref, out_ref, prepare, half):
    live = m < nu_ref[0]
    second = jnp.logical_and(live, half2_ref[m] == 1)
    zeros = jnp.zeros((MOE_HALF, out_ref.shape[1]), out_ref.dtype)

    @pl.when(live)
    def _():
        prepare()

    @pl.when(second)
    def _():
        half(pl.ds(0, MOE_TM))

    @pl.when(jnp.logical_and(live, jnp.logical_not(second)))
    def _():
        half(pl.ds(0, MOE_HALF))

    @pl.when(jnp.logical_not(live))
    def _():
        out_ref[pl.ds(0, MOE_HALF), :] = zeros

    @pl.when(jnp.logical_not(second))
    def _():
        out_ref[pl.ds(MOE_HALF, MOE_HALF), :] = zeros


def _cast_chunks(w_refs, w_bf):
    rows = w_refs[0].shape[0]
    for i, w_ref in enumerate(w_refs):
        w_bf[pl.ds(i * rows, rows), :] = w_ref[...].astype(bf16)


def _gemm1_kernel(be_ref, first_ref, half2_ref, nu_ref, xs_ref, *refs):
    wg_refs, wl_refs = refs[:W_STREAMS], refs[W_STREAMS:2 * W_STREAMS]
    bg_ref, bl_ref, act_ref, wg_bf, wl_bf = refs[2 * W_STREAMS:]
    m = pl.program_id(1)

    def prepare():
        @pl.when(first_ref[m] == 1)
        def _():
            _cast_chunks(wg_refs, wg_bf)
            _cast_chunks(wl_refs, wl_bf)

    def half(rows):
        lo, hi = _unpack_halves(xs_ref[rows, :])
        lo = lo.astype(bf16)
        hi = hi.astype(bf16)
        k = lo.shape[-1]
        gate = (jnp.dot(lo, wg_bf[:k, :], preferred_element_type=f32)
                + jnp.dot(hi, wg_bf[k:, :], preferred_element_type=f32) + bg_ref[...])
        lin = (jnp.dot(lo, wl_bf[:k, :], preferred_element_type=f32)
               + jnp.dot(hi, wl_bf[k:, :], preferred_element_type=f32) + bl_ref[...])
        x_glu = jnp.minimum(gate, SWIGLU_LIMIT)
        x_lin = jnp.clip(lin, -SWIGLU_LIMIT, SWIGLU_LIMIT)
        act_ref[rows, :] = (x_glu * jax.nn.sigmoid(SWIGLU_ALPHA * x_glu) * (x_lin + 1.0)).astype(act_ref.dtype)

    _block_halves(m, half2_ref, nu_ref, act_ref, prepare, half)


def _gemm1(block_exp, first, half2, n_used, xs, w_gu, b_gu, tn):
    cap, kp = xs.shape
    n_e, d, f2 = w_gu.shape
    ff = f2 // 2
    tn = min(tn, ff)
    nb = cap // MOE_TM
    nj = ff // tn
    live = lambda m, nu: jnp.minimum(m, nu[0] - 1)
    dk = d // W_STREAMS
    wspec = lambda col0, c: pl.BlockSpec((None, dk, tn), lambda j, m, be, fi, h2, nu: (be[m], c, col0 + j))
    return pl.pallas_call(
        _gemm1_kernel,
        out_shape=jax.ShapeDtypeStruct((cap, ff), bf16),
        grid_spec=pltpu.PrefetchScalarGridSpec(
            num_scalar_prefetch=4,
            grid=(nj, nb),
            in_specs=[pl.BlockSpec((MOE_TM, kp), lambda j, m, be, fi, h2, nu: (live(m, nu), 0))]
                     + [wspec(0, c) for c in range(W_STREAMS)] + [wspec(nj, c) for c in range(W_STREAMS)]
                     + [pl.BlockSpec((None, 1, tn), lambda j, m, be, fi, h2, nu: (be[m], 0, j)),
                        pl.BlockSpec((None, 1, tn), lambda j, m, be, fi, h2, nu: (be[m], 0, nj + j))],
            out_specs=pl.BlockSpec((MOE_TM, tn), lambda j, m, be, fi, h2, nu: (m, j)),
            scratch_shapes=[pltpu.VMEM((d, tn), bf16), pltpu.VMEM((d, tn), bf16)]),
        compiler_params=_cparams(("arbitrary", "arbitrary"), 56),
        name="gemm1",
    )(block_exp, first, half2, n_used, xs, *([w_gu] * (2 * W_STREAMS)),
      b_gu.reshape(n_e, 1, f2), b_gu.reshape(n_e, 1, f2))


def _gemm2_kernel(be_ref, first_ref, half2_ref, nu_ref, act_ref, *refs):
    w_refs = refs[:W_STREAMS]
    b_ref, y_ref, w_bf = refs[W_STREAMS:]
    m = pl.program_id(0)

    def prepare():
        @pl.when(first_ref[m] == 1)
        def _():
            _cast_chunks(w_refs, w_bf)

    def half(rows):
        y = jnp.dot(act_ref[rows, :], w_bf[...], preferred_element_type=f32) + b_ref[...]
        y_ref[rows, :] = _pack_halves(y.astype(bf16))

    _block_halves(m, half2_ref, nu_ref, y_ref, prepare, half)


def _gemm2(block_exp, first, half2, n_used, act, w_down, b_down):
    cap, ff = act.shape
    n_e, _, d = w_down.shape
    nb = cap // MOE_TM
    live = lambda m, nu: jnp.minimum(m, nu[0] - 1)
    wspec = lambda c: pl.BlockSpec((None, ff // W_STREAMS, d), lambda m, be, fi, h2, nu: (be[m], c, 0))
    return pl.pallas_call(
        _gemm2_kernel,
        out_shape=jax.ShapeDtypeStruct((cap, d // 2), u32),
        grid_spec=pltpu.PrefetchScalarGridSpec(
            num_scalar_prefetch=4,
            grid=(nb,),
            in_specs=[pl.BlockSpec((MOE_TM, ff), lambda m, be, fi, h2, nu: (live(m, nu), 0))]
                     + [wspec(c) for c in range(W_STREAMS)]
                     + [pl.BlockSpec((None, 1, d), lambda m, be, fi, h2, nu: (be[m], 0, 0))],
            out_specs=pl.BlockSpec((MOE_TM, d // 2), lambda m, be, fi, h2, nu: (m, 0)),
            scratch_shapes=[pltpu.VMEM((ff, d), bf16)]),
        compiler_params=_cparams(("arbitrary",), 56),
        name="gemm2",
    )(block_exp, first, half2, n_used, act, *([w_down] * W_STREAMS), b_down.reshape(n_e, 1, d))


COMBINE_TM = 256


def _combine_kernel(dest_ref, dest_next_ref, yb_ref, route_ref, x1_ref, g2_ref, gain_ref, o_ref, buf, sems):
    step = pl.program_id(0)
    cur = lax.rem(step, 2)
    nxt = 1 - cur
    n_groups = COMBINE_TM // SUBLANES

    def row_copy(d_ref, b, g, u, k):
        slot = d_ref[(g * SUBLANES + u) * TOP_K + k]
        return pltpu.make_async_copy(yb_ref.at[pl.ds(slot, 1), :], buf.at[b, k, g, pl.ds(u, 1), :], sems.at[b])

    def each_row(g, fn):
        for u in range(SUBLANES):
            for k in range(TOP_K):
                fn(g, u, k)

    def loop_rows(fn):
        def body(g, c):
            each_row(g, fn)
            return c
        lax.fori_loop(0, n_groups, body, 0)

    @pl.when(step == 0)
    def _():
        loop_rows(lambda g, u, k: row_copy(dest_ref, cur, g, u, k).start())

    loop_rows(lambda g, u, k: row_copy(dest_ref, cur, g, u, k).wait())

    for g in range(n_groups):
        each_row(g, lambda g, u, k: row_copy(dest_next_ref, nxt, g, u, k).start())

    route = route_ref[...]
    acc = None
    for k in range(TOP_K):
        lo, hi = _unpack_halves(buf[cur, k].reshape(COMBINE_TM, buf.shape[-1]))
        term = route[:, TOP_K + k:TOP_K + k + 1] * jnp.concatenate([lo, hi], axis=-1)
        acc = term if acc is None else acc + term
    x2 = x1_ref[...] + g2_ref[...] * acc
    ms = jnp.mean(x2 * x2, axis=-1, keepdims=True)
    o_ref[...] = x2 * lax.rsqrt(ms + EPS) * gain_ref[...]

    @pl.when(step == pl.num_programs(0) - 1)
    def _():
        loop_rows(lambda g, u, k: row_copy(dest_next_ref, nxt, g, u, k).wait())


def _combine(dest, yb, route, x1, g2, gain):
    b, l, d = x1.shape
    tm = min(COMBINE_TM, l)
    assert tm == COMBINE_TM
    per_b = l // tm
    n_tiles = b * per_b
    return pl.pallas_call(
        _combine_kernel,
        out_shape=jax.ShapeDtypeStruct((b, l, d), f32),
        grid=(n_tiles,),
        in_specs=[pl.BlockSpec((tm * TOP_K,), lambda i: (i,), memory_space=pltpu.SMEM),
                  pl.BlockSpec((tm * TOP_K,), lambda i: (jnp.minimum(i + 1, n_tiles - 1),), memory_space=pltpu.SMEM),
                  pl.BlockSpec(memory_space=pl.ANY),
                  pl.BlockSpec((None, tm, 128), lambda i: (i // per_b, i % per_b, 0)),
                  pl.BlockSpec((None, tm, d), lambda i: (i // per_b, i % per_b, 0)),
                  pl.BlockSpec((None, 1, d), lambda i: (i // per_b, 0, 0)),
                  pl.BlockSpec((1, d), lambda i: (0, 0))],
        out_specs=pl.BlockSpec((None, tm, d), lambda i: (i // per_b, i % per_b, 0)),
        scratch_shapes=[pltpu.VMEM((2, TOP_K, tm // SUBLANES, SUBLANES, d // 2), u32),
                        pltpu.SemaphoreType.DMA((2,))],
        compiler_params=_cparams(("arbitrary",), 40),
        name="combine",
    )(dest, dest, yb, route, x1, g2, gain)


def _routing(route, n_blocks):
    experts = slice(MEMBER_LANE, MEMBER_LANE + N_EXPERTS)
    top_idx = route[:, :TOP_K].astype(jnp.int32)
    incl = _token_cumsum(route)[:, experts]
    rank = (incl - route[:, experts]).astype(jnp.int32)
    counts = incl[-1].astype(jnp.int32)
    padded = (counts + MOE_TM - 1) // MOE_TM * MOE_TM
    pad_end = jnp.cumsum(padded).astype(jnp.int32)
    slot0 = (pad_end - padded)[None, :] + rank
    onehot = top_idx[:, :, None] == jnp.arange(N_EXPERTS, dtype=jnp.int32)[None, None, :]
    dest = jnp.sum(jnp.where(onehot, slot0[:, None, :], 0), axis=-1).astype(jnp.int32)
    block_start = jnp.arange(n_blocks, dtype=jnp.int32) * MOE_TM
    block_exp = jnp.minimum(jnp.sum(pad_end[None, :] <= block_start[:, None], axis=1), N_EXPERTS - 1).astype(jnp.int32)
    first = jnp.concatenate([jnp.ones((1,), jnp.int32), (block_exp[1:] != block_exp[:-1]).astype(jnp.int32)])
    n_used = (pad_end[-1:] // MOE_TM).astype(jnp.int32)
    token_end = (pad_end - padded + counts)[block_exp]
    half2 = (token_end > block_start + MOE_HALF).astype(jnp.int32)
    return dest.reshape(-1), pad_end, block_exp, first, half2, n_used


def _moe(hp, route, x1, g2, norm_final, w_gu, b_gu, w_down, b_down):
    b, l, d = x1.shape
    t = b * l
    n_blocks = -(-(t * TOP_K + N_EXPERTS * (MOE_TM - 1)) // MOE_TM)
    dest, pad_end, block_exp, first, half2, n_used = _routing(route.reshape(t, 128), n_blocks)
    xs = _dispatch(pad_end, dest, hp.reshape(t, d // 2), n_blocks * MOE_TM)
    act = _gemm1(block_exp, first, half2, n_used, xs, w_gu, b_gu, 1024)
    yb = _gemm2(block_exp, first, half2, n_used, act, w_down, b_down)
    return _combine(dest, yb, route, x1, g2, norm_final.reshape(1, d))


def kernel(x, c, ctx, c_ctx, w_ada, b_ada, norm_mix, norm_ffn, w_in, lb_table, hg_norm, rpb, w_out, w_router,
           b_router, w_gu, b_gu, w_down, b_down, norm_final):
    b, l, d = x.shape
    assert w_ada.shape[0] == 1, "single-layer block"
    rows = l // GRID_W

    c16 = jnp.zeros((16, d), f32).at[:b].set(c).at[b].set(c_ctx)
    mod = _ada(c16, w_ada[0], b_ada[0])
    sh1, sc1, g1, sh2, sc2, g2 = [mod[:b, i * d:(i + 1) * d].reshape(b, 1, d) for i in range(6)]
    csh = jnp.broadcast_to(mod[b, :d].reshape(1, 1, d), (b, 1, d))
    csc = jnp.broadcast_to(mod[b, d:2 * d].reshape(1, 1, d), (b, 1, d))

    w_in_bf = w_in[0].astype(bf16)
    gain_mix = norm_mix[0].reshape(1, d)
    px_att = _inproj(x, sh1, sc1, gain_mix, w_in_bf, P_QN, 3, bf16, 1024)
    px_hg = _inproj(x, sh1, sc1, gain_mix, w_in_bf, P_QH, 5, f32, 1024)
    pc_att = _inproj(ctx, csh, csc, gain_mix, w_in_bf, P_KN, 2, bf16, 256)
    pc_hg = _inproj(ctx, csh, csc, gain_mix, w_in_bf, P_FF, 3, f32, 256)

    o_na = _natten(px_att, pc_att, _natten_bias_table(rpb[0], rows))

    lower_bounds = jnp.cumsum(jax.nn.softmax(lb_table.astype(f32), axis=0), axis=0)
    o_hg = _hgrn(px_hg, pc_hg, lower_bounds[0].reshape(2, HEADS * HEAD_DIM), hg_norm[0])

    x1, hp, route = _outproj(o_na, o_hg, x, w_out[0].astype(bf16), g1, sh2, sc2, norm_ffn[0].reshape(1, d),
                             w_router[0], b_router[0], 256)
    return _moe(hp, route, x1, g2, norm_final, w_gu[0], b_gu[0], w_down[0], b_down[0])
```

```python
import functools

import numpy as np
import jax
import jax.numpy as jnp
from jax import lax
from jax.experimental import pallas as pl
from jax.experimental.pallas import tpu as pltpu

f32 = jnp.float32
bf16 = jnp.bfloat16
u32 = jnp.uint32

GRID_W = 64
HEADS = 8
HEAD_DIM = 128
WIN_R = 8
WIN_C = 16
ROPE_THETA = 10000.0
N_EXPERTS = 32
TOP_K = 4
SWIGLU_LIMIT = 7.0
SWIGLU_ALPHA = 1.702
EPS = 1e-6
P_QN, P_KN, P_VN, P_QH, P_FF, P_FB, P_IH, P_GH = range(8)

CHUNK = 64
SUB = 16
CHUNK_UNROLL = 8
ROW_UNROLL = 4
MOE_TM = 512
MOE_HALF = 256
W_STREAMS = 1
NEG_BIG = -1e30
MEMBER_LANE = 32
HI_MASK = 0xFFFF0000

_HIGHEST = lax.Precision.HIGHEST


def _cparams(sem, vmem_mb=None):
    kw = dict(dimension_semantics=sem)
    if vmem_mb is not None:
        kw["vmem_limit_bytes"] = vmem_mb * 1024 * 1024
    return pltpu.CompilerParams(**kw)


def _ada_kernel(c_ref, w_ref, b_ref, o_ref):
    c = c_ref[...]
    cond = c * jax.nn.sigmoid(c)
    o_ref[...] = jnp.dot(cond, w_ref[...], precision=_HIGHEST, preferred_element_type=f32) + b_ref[...]


def _ada(c16, w_ada, b_ada):
    d, n = w_ada.shape
    tn = min(1024, n)
    return pl.pallas_call(
        _ada_kernel,
        out_shape=jax.ShapeDtypeStruct((c16.shape[0], n), f32),
        grid=(n // tn,),
        in_specs=[pl.BlockSpec((c16.shape[0], d), lambda j: (0, 0)),
                  pl.BlockSpec((d, tn), lambda j: (0, j)),
                  pl.BlockSpec((1, tn), lambda j: (0, j))],
        out_specs=pl.BlockSpec((c16.shape[0], tn), lambda j: (0, j)),
        compiler_params=_cparams(("parallel",), 40),
        name="ada",
    )(c16, w_ada, b_ada.reshape(1, n))


def _inproj_kernel(x_ref, shift_ref, scale_ref, gain_ref, w_ref, o_ref, h_scr):
    @pl.when(pl.program_id(2) == 0)
    def _():
        x = x_ref[...]
        ms = jnp.mean(x * x, axis=-1, keepdims=True)
        y = x * lax.rsqrt(ms + EPS) * gain_ref[...]
        h_scr[...] = (y * (1.0 + scale_ref[...]) + shift_ref[...]).astype(bf16)

    acc = jnp.dot(h_scr[...], w_ref[...], preferred_element_type=f32)
    for hh in range(HEADS):
        o_ref[hh] = acc[:, hh * HEAD_DIM:(hh + 1) * HEAD_DIM].astype(o_ref.dtype)


def _inproj(x, shift, scale, gain, w_bf, part_lo, n_parts, out_dtype, tm):
    b, l, d = x.shape
    pw = HEADS * HEAD_DIM
    tm = min(tm, l)
    return pl.pallas_call(
        _inproj_kernel,
        out_shape=jax.ShapeDtypeStruct((b, n_parts * HEADS, l, HEAD_DIM), out_dtype),
        grid=(b, l // tm, n_parts),
        in_specs=[pl.BlockSpec((None, tm, d), lambda bi, mi, ni: (bi, mi, 0)),
                  pl.BlockSpec((None, 1, d), lambda bi, mi, ni: (bi, 0, 0)),
                  pl.BlockSpec((None, 1, d), lambda bi, mi, ni: (bi, 0, 0)),
                  pl.BlockSpec((1, d), lambda bi, mi, ni: (0, 0)),
                  pl.BlockSpec((d, pw), lambda bi, mi, ni: (0, part_lo + ni))],
        out_specs=pl.BlockSpec((None, HEADS, tm, HEAD_DIM), lambda bi, mi, ni: (bi, ni, mi, 0)),
        scratch_shapes=[pltpu.VMEM((tm, d), bf16)],
        compiler_params=_cparams(("parallel", "parallel", "arbitrary"), 48),
        name="inproj",
    )(x, shift, scale, gain, w_bf)


def _natten_bias_table(rpb, rows):
    kr = min(WIN_R, rows)
    q = np.arange(GRID_W)
    col_start = np.clip(q - WIN_C // 2, 0, GRID_W - WIN_C)
    kc = np.arange(GRID_W)
    in_win = (kc[None, :] >= col_start[:, None]) & (kc[None, :] < col_start[:, None] + WIN_C)
    d_col = np.clip(kc[None, :] - q[:, None] + WIN_C - 1, 0, 2 * WIN_C - 2)
    n_d0 = 2 * WIN_R - 1 - (kr - 1)
    onehot = (d_col[None] == np.arange(2 * WIN_C - 1)[:, None, None]).astype(np.float32)
    cols = jnp.einsum("hrc,cqk->hrqk", rpb.astype(f32), jnp.asarray(onehot), precision=_HIGHEST)
    cols = jnp.where(in_win[None, None], cols, -jnp.inf)
    t = jnp.stack([cols[:, d0:d0 + kr] for d0 in range(n_d0)], axis=1)
    t = jnp.transpose(t, (0, 1, 3, 2, 4))
    return t.reshape(rpb.shape[0], n_d0, GRID_W, kr * GRID_W)


def _natten_kernel(q_ref, k_ref, v_ref, kc_ref, vc_ref, bias_ref, o_ref, sw0, sc0, sw1, sc1, *, rows, kr):
    scale = HEAD_DIM ** -0.5
    kc = kc_ref[...]
    vc = vc_ref[...]
    nt = (((1,), (1,)), ((), ()))

    n_groups = rows // ROW_UNROLL
    gq = ROW_UNROLL * GRID_W

    def key_start(r):
        return jnp.clip(r - kr // 2, 0, rows - kr)

    def scores(g, sw_scr, sc_scr):
        q_all = q_ref[pl.ds(pl.multiple_of(g * gq, gq), gq), :]
        sc_scr[...] = lax.dot_general(q_all, kc, nt, preferred_element_type=f32) * scale
        for u in range(ROW_UNROLL):
            r = g * ROW_UNROLL + u
            kr0 = key_start(r)
            kw = k_ref[pl.ds(pl.multiple_of(kr0 * GRID_W, GRID_W), kr * GRID_W), :]
            q = q_all[u * GRID_W:(u + 1) * GRID_W, :]
            sw_scr[u] = (lax.dot_general(q, kw, nt, preferred_element_type=f32) * scale
                         + bias_ref[kr0 - r + WIN_R - 1])

    def attend(g, sw_scr, sc_scr):
        s_c = sc_scr[...]
        s_w = [sw_scr[u] for u in range(ROW_UNROLL)]
        m = jnp.maximum(jnp.concatenate([jnp.max(s, axis=-1, keepdims=True) for s in s_w], axis=0),
                        jnp.max(s_c, axis=-1, keepdims=True))
        p_c = jnp.exp(s_c - m)
        p_w = [jnp.exp(s - m[u * GRID_W:(u + 1) * GRID_W, :]) for u, s in enumerate(s_w)]
        denom = (jnp.concatenate([jnp.sum(p, axis=-1, keepdims=True) for p in p_w], axis=0)
                 + jnp.sum(p_c, axis=-1, keepdims=True))
        o_c = jnp.dot(p_c.astype(bf16), vc, preferred_element_type=f32)
        o_w = []
        for u, p in enumerate(p_w):
            k0 = pl.multiple_of(key_start(g * ROW_UNROLL + u) * GRID_W, GRID_W)
            o_w.append(jnp.dot(p.astype(bf16), v_ref[pl.ds(k0, kr * GRID_W), :], preferred_element_type=f32))
        o = (jnp.concatenate(o_w, axis=0) + o_c) / denom
        o_ref[pl.ds(pl.multiple_of(g * gq, gq), gq), :] = o.astype(o_ref.dtype)

    scores(0, sw0, sc0)

    def body(i, carry):
        g = 2 * i
        scores(g + 1, sw1, sc1)
        attend(g, sw0, sc0)
        scores(jnp.minimum(g + 2, n_groups - 1), sw0, sc0)
        attend(g + 1, sw1, sc1)
        return carry

    lax.fori_loop(0, n_groups // 2, body, 0)


def _natten(px_att, pc_att, bias_tab):
    b, _, l, _ = px_att.shape
    lc = pc_att.shape[2]
    rows = l // GRID_W
    kr = min(WIN_R, rows)
    n_d0 = bias_tab.shape[1]
    blk = lambda off: pl.BlockSpec((None, None, l, HEAD_DIM), lambda h, bi: (bi, off + h, 0, 0))
    cblk = lambda off: pl.BlockSpec((None, None, lc, HEAD_DIM), lambda h, bi: (bi, off + h, 0, 0))
    return pl.pallas_call(
        functools.partial(_natten_kernel, rows=rows, kr=kr),
        out_shape=jax.ShapeDtypeStruct((b, l, HEADS * HEAD_DIM), bf16),
        grid=(HEADS, b),
        in_specs=[blk(0), blk(HEADS), blk(2 * HEADS), cblk(0), cblk(HEADS),
                  pl.BlockSpec((None, n_d0, GRID_W, kr * GRID_W), lambda h, bi: (h, 0, 0, 0))],
        out_specs=pl.BlockSpec((None, l, HEAD_DIM), lambda h, bi: (bi, 0, h)),
        scratch_shapes=[pltpu.VMEM((ROW_UNROLL, GRID_W, kr * GRID_W), f32), pltpu.VMEM((ROW_UNROLL * GRID_W, lc), f32),
                        pltpu.VMEM((ROW_UNROLL, GRID_W, kr * GRID_W), f32), pltpu.VMEM((ROW_UNROLL * GRID_W, lc), f32)],
        compiler_params=_cparams(("parallel", "parallel")),
        name="natten",
    )(px_att, px_att, px_att, pc_att, pc_att, bias_tab)


def _hgrn_consts():
    t = np.arange(CHUNK)
    bt, bs = t[:, None] // SUB, t[None, :] // SUB
    tri, masks = [], []
    for sgn in (1, -1):
        before = (t[None, :] <= t[:, None]) if sgn == 1 else (t[None, :] >= t[:, None])
        tri.append(before.astype(np.float32))
        dist = (bt - bs) * sgn
        masks.append(np.stack([dist == 1, dist == 2, dist == 3, (dist == 0) & before]).astype(np.float32))
    return np.stack(tri), np.stack(masks)


def _rope_tables(l):
    t = jnp.arange(l)
    n_freq = HEAD_DIM // 4
    inv_freq = ROPE_THETA ** (-jnp.arange(n_freq, dtype=f32) / n_freq)
    ang_row = (t // GRID_W).astype(f32)[:, None] * inv_freq
    ang_col = (t % GRID_W).astype(f32)[:, None] * inv_freq
    cr, sr, cc, sc = jnp.cos(ang_row), jnp.sin(ang_row), jnp.cos(ang_col), jnp.sin(ang_col)
    z = jnp.zeros_like(sr)
    cos_t = jnp.concatenate([cr, cr, cc, cc], axis=-1)
    sin_up = jnp.concatenate([-sr, z, -sc, z], axis=-1)
    sin_dn = jnp.concatenate([z, sr, z, sc], axis=-1)
    return cos_t, sin_up, sin_dn


def _split3_dot(tri_bf, g):
    g1 = g.astype(bf16)
    r1 = g - g1.astype(f32)
    g2 = r1.astype(bf16)
    g3 = (r1 - g2.astype(f32)).astype(bf16)
    dot = lambda a: jnp.dot(tri_bf, a, preferred_element_type=f32)
    return dot(g1) + dot(g2) + dot(g3)


def _gate(f_raw, lb):
    log_f = jnp.log(lb + (1.0 - lb) * jax.nn.sigmoid(f_raw))
    key = (1.0 - lb) * jax.nn.sigmoid(-f_raw)
    return log_f, key


def _chunk_refs(cum, backward):
    nb = CHUNK // SUB
    if backward:
        ends = [cum[i * SUB:i * SUB + 1, :] for i in range(nb)]
        order = list(range(nb - 1, -1, -1))
    else:
        ends = [cum[i * SUB + SUB - 1:i * SUB + SUB, :] for i in range(nb)]
        order = list(range(nb))
    zero = jnp.zeros_like(ends[0])
    b_rows, g2_rows, g3_rows = [None] * nb, [None] * nb, [None] * nb
    for pos, i in enumerate(order):
        b_i = zero if pos == 0 else ends[order[pos - 1]]
        b_rows[i] = b_i
        g2_rows[i] = b_i - ends[order[pos - 2]] if pos >= 2 else zero
        g3_rows[i] = b_i - ends[order[pos - 3]] if pos >= 3 else zero
    expand = lambda rws: jnp.concatenate([jnp.broadcast_to(r, (SUB, HEAD_DIM)) for r in rws], axis=0)
    total = ends[order[-1]]
    return expand(b_rows), expand(ends), expand(g2_rows), expand(g3_rows), total


def _rope(x, cos_t, sin_up, sin_dn):
    return x * cos_t + pltpu.roll(x, 96, 1) * sin_up + pltpu.roll(x, 32, 1) * sin_dn


def _scan_group(states, items, tris, masks):
    nt = (((1,), (1,)), ((), ()))
    tn = (((0,), (0,)), ((), ()))
    pre = []
    for d, q, f_raw, v, lb, rope in items:
        g, key = _gate(f_raw, lb)
        pre.append((g, _rope(q, *rope), _rope(key, *rope), v.astype(bf16)))
    cums = [_split3_dot(tris[it[0]], p[0]) for it, p in zip(items, pre)]
    ops = []
    for it, (g, qr, kr, v_bf), cum in zip(items, pre, cums):
        b, e, gap2, gap3, total = _chunk_refs(cum, it[0] == 1)
        q_t = qr * jnp.exp(cum - b)
        k_hat = kr * jnp.exp(e - cum)
        k_til = kr * jnp.exp(b - cum)
        lhs = jnp.concatenate([q_t, q_t * jnp.exp(gap2), q_t * jnp.exp(gap3)], axis=0).astype(bf16)
        rhs = jnp.concatenate([k_hat, k_til], axis=0).astype(bf16)
        q_in = (q_t * jnp.exp(b)).astype(bf16)
        k_dec = (k_hat * jnp.exp(total - e)).astype(bf16)
        ops.append((lhs, rhs, q_in, k_dec, jnp.exp(total)))
    scores = [lax.dot_general(o[0], o[1], nt, preferred_element_type=f32) for o in ops]
    upds = [lax.dot_general(p[3], o[3], tn, preferred_element_type=f32) for p, o in zip(pre, ops)]
    intra = []
    for it, p, (g, qr, kr, v_bf) in zip(items, scores, pre):
        m = masks[it[0]]
        att = (jnp.where(m[0] > 0, p[0:CHUNK, 0:CHUNK], 0.0)
               + jnp.where(m[1] > 0, p[CHUNK:2 * CHUNK, 0:CHUNK], 0.0)
               + jnp.where(m[2] > 0, p[2 * CHUNK:3 * CHUNK, 0:CHUNK], 0.0)
               + jnp.where(m[3] > 0, p[0:CHUNK, CHUNK:2 * CHUNK], 0.0))
        intra.append(jnp.dot(att.astype(bf16), v_bf, preferred_element_type=f32))
    states = list(states)
    outs = []
    for it, o, upd, o_in in zip(items, ops, upds, intra):
        st = states[it[0]]
        outs.append(o_in + lax.dot_general(o[2], st.astype(bf16), nt, preferred_element_type=f32))
        states[it[0]] = st * o[4] + upd
    return states, outs


def _hgrn_kernel(q_ref, ff_ref, fb_ref, v_ref, gate_ref, cff_ref, cfb_ref, cv_ref, lb_ref, gain_ref,
                 cos_ref, sup_ref, sdn_ref, tri_ref, mask_ref, o_ref, of_scr, ob_scr, *, n_chunks, n_cchunks):
    lb_f = lb_ref[0]
    lb_b = lb_ref[1]
    tri_f = tri_ref[0].astype(bf16)
    tri_b = tri_ref[1].astype(bf16)
    zero = jnp.zeros((HEAD_DIM, HEAD_DIM), f32)

    ctx_items = []
    for c in range(n_cchunks):
        rf = pl.ds(c * CHUNK, CHUNK)
        rb = pl.ds((n_cchunks - 1 - c) * CHUNK, CHUNK)
        ctx_items.append((0, cv_ref[rf, :], cff_ref[rf, :], lb_f))
        ctx_items.append((1, cv_ref[rb, :], cfb_ref[rb, :], lb_b))
    st_f, st_b = _state_group([zero, zero], ctx_items, (tri_f, tri_b))

    def body(c, carry):
        masks = [[mask_ref[d, i] for i in range(4)] for d in range(2)]
        items, rows = [], []
        for u in range(CHUNK_UNROLL):
            cf = c * CHUNK_UNROLL + u
            rf = pl.ds(pl.multiple_of(cf * CHUNK, CHUNK), CHUNK)
            rb = pl.ds(pl.multiple_of((n_chunks - 1 - cf) * CHUNK, CHUNK), CHUNK)
            items.append((0, q_ref[rf, :], ff_ref[rf, :], v_ref[rf, :], lb_f,
                          (cos_ref[rf, :], sup_ref[rf, :], sdn_ref[rf, :])))
            items.append((1, q_ref[rb, :], fb_ref[rb, :], v_ref[rb, :], lb_b,
                          (cos_ref[rb, :], sup_ref[rb, :], sdn_ref[rb, :])))
            rows += [(of_scr, rf), (ob_scr, rb)]
        states, outs = _scan_group(carry, items, (tri_f, tri_b), masks)
        for (scr, rws), o in zip(rows, outs):
            scr[rws, :] = o
        return tuple(states)

    lax.fori_loop(0, n_chunks // CHUNK_UNROLL, body, (st_f, st_b))

    o = of_scr[...] + ob_scr[...]
    y = o * lax.rsqrt(jnp.mean(o * o, axis=-1, keepdims=True) + EPS) * gain_ref[...]
    gate = gate_ref[...]
    o_ref[...] = (y * (gate * jax.nn.sigmoid(gate))).astype(o_ref.dtype)


def _state_group(states, items, tris):
    gates = [_gate(f_raw, lb) for _, _, f_raw, lb in items]
    cums = [_split3_dot(tris[it[0]], g) for it, (g, _) in zip(items, gates)]
    terms = []
    for it, (_, key), cum in zip(items, gates, cums):
        total = cum[0:1, :] if it[0] == 1 else cum[CHUNK - 1:CHUNK, :]
        terms.append(((key * jnp.exp(total - cum)).astype(bf16), jnp.exp(total)))
    upds = [lax.dot_general(it[1].astype(bf16), k_dec, (((0,), (0,)), ((), ())), preferred_element_type=f32)
            for it, (k_dec, _) in zip(items, terms)]
    states = list(states)
    for it, (_, decay), upd in zip(items, terms, upds):
        states[it[0]] = states[it[0]] * decay + upd
    return states


def _hgrn(px_hg, pc_hg, lb2, hg_gain):
    b, _, l, _ = px_hg.shape
    lc = pc_hg.shape[2]
    assert l % (CHUNK * CHUNK_UNROLL) == 0 and lc % CHUNK == 0
    cos_t, sin_up, sin_dn = _rope_tables(l)
    tri, masks = _hgrn_consts()
    blk = lambda off: pl.BlockSpec((None, None, l, HEAD_DIM), lambda h, bi: (bi, off + h, 0, 0))
    cblk = lambda off: pl.BlockSpec((None, None, lc, HEAD_DIM), lambda h, bi: (bi, off + h, 0, 0))
    full = lambda shp: pl.BlockSpec(shp, lambda h, bi: (0,) * len(shp))
    return pl.pallas_call(
        functools.partial(_hgrn_kernel, n_chunks=l // CHUNK, n_cchunks=lc // CHUNK),
        out_shape=jax.ShapeDtypeStruct((b, l, HEADS * HEAD_DIM), bf16),
        grid=(HEADS, b),
        in_specs=[blk(0), blk(HEADS), blk(2 * HEADS), blk(3 * HEADS), blk(4 * HEADS),
                  cblk(0), cblk(HEADS), cblk(2 * HEADS),
                  pl.BlockSpec((2, None, 1, HEAD_DIM), lambda h, bi: (0, h, 0, 0)),
                  full((1, HEAD_DIM)),
                  full((l, HEAD_DIM)), full((l, HEAD_DIM)), full((l, HEAD_DIM)),
                  full((2, CHUNK, CHUNK)), full((2, 4, CHUNK, CHUNK))],
        out_specs=pl.BlockSpec((None, l, HEAD_DIM), lambda h, bi: (bi, 0, h)),
        scratch_shapes=[pltpu.VMEM((l, HEAD_DIM), f32), pltpu.VMEM((l, HEAD_DIM), f32)],
        compiler_params=_cparams(("parallel", "parallel")),
        name="hgrn",
    )(px_hg, px_hg, px_hg, px_hg, px_hg, pc_hg, pc_hg, pc_hg,
      lb2.reshape(2, HEADS, 1, HEAD_DIM), hg_gain.reshape(1, HEAD_DIM),
      cos_t, sin_up, sin_dn, jnp.asarray(tri), jnp.asarray(masks))


def _pack_halves(h):
    k = h.shape[-1] // 2
    lo = lax.bitcast_convert_type(h[:, :k].astype(f32), u32)
    hi = lax.bitcast_convert_type(h[:, k:].astype(f32), u32)
    return (lo >> 16) | (hi & u32(HI_MASK))


def _unpack_halves(u):
    lo = lax.bitcast_convert_type(u << 16, f32)
    hi = lax.bitcast_convert_type(u & u32(HI_MASK), f32)
    return lo, hi


def _outproj_kernel(ana_ref, ahg_ref, x_ref, w0_ref, w1_ref, g1_ref, sh_ref, sc_ref, gain_ref, wr_ref, br_ref,
                    x1_ref, hp_ref, route_ref):
    y = (jnp.dot(ana_ref[...], w0_ref[...], preferred_element_type=f32)
         + jnp.dot(ahg_ref[...], w1_ref[...], preferred_element_type=f32))
    x1 = x_ref[...] + g1_ref[...] * y
    x1_ref[...] = x1
    ms = jnp.mean(x1 * x1, axis=-1, keepdims=True)
    h = x1 * lax.rsqrt(ms + EPS) * gain_ref[...]
    h = h * (1.0 + sc_ref[...]) + sh_ref[...]
    h_hi = h.astype(bf16)
    hp_ref[...] = _pack_halves(h_hi)
    h_lo = (h - h_hi.astype(f32)).astype(bf16)
    t = jnp.dot(h_hi, wr_ref[...], preferred_element_type=f32)
    logits = (t[:, :128] + t[:, 128:] + jnp.dot(h_lo, wr_ref[:, :128], preferred_element_type=f32)) + br_ref[...]
    lane = lax.broadcasted_iota(jnp.int32, logits.shape, 1).astype(f32)
    cur = logits
    vals, idxs = [], []
    for _ in range(TOP_K):
        m = jnp.max(cur, axis=-1, keepdims=True)
        i = jnp.min(jnp.where(cur == m, lane, float(logits.shape[-1])), axis=-1, keepdims=True)
        vals.append(m)
        idxs.append(i)
        cur = jnp.where(lane == i, -jnp.inf, cur)
    es = [jnp.exp(v - vals[0]) for v in vals]
    denom = es[0] + es[1] + es[2] + es[3]
    route = jnp.zeros(logits.shape, f32)
    for k in range(TOP_K):
        route = jnp.where(lane == k, idxs[k], route)
        route = jnp.where(lane == TOP_K + k, es[k] / denom, route)
        route = jnp.where(lane - float(MEMBER_LANE) == idxs[k], 1.0, route)
    route_ref[...] = route


def _outproj(a_na, a_hg, x, w_out_bf, g1, sh2, sc2, gain, w_router, b_router, tm):
    b, l, d = x.shape
    hw = a_na.shape[-1]
    tm = min(tm, l)
    n_e = w_router.shape[-1]
    wr = jnp.zeros((d, 128), f32).at[:, :n_e].set(w_router)
    wr_hi = wr.astype(bf16)
    wr = jnp.concatenate([wr_hi, (wr - wr_hi.astype(f32)).astype(bf16)], axis=1)
    br = jnp.full((1, 128), NEG_BIG, f32).at[0, :n_e].set(b_router)
    row = lambda last: pl.BlockSpec((None, tm, last), lambda bi, mi: (bi, mi, 0))
    vec = pl.BlockSpec((None, 1, d), lambda bi, mi: (bi, 0, 0))
    return pl.pallas_call(
        _outproj_kernel,
        out_shape=(jax.ShapeDtypeStruct((b, l, d), f32),
                   jax.ShapeDtypeStruct((b, l, d // 2), u32),
                   jax.ShapeDtypeStruct((b, l, 128), f32)),
        grid=(b, l // tm),
        in_specs=[row(hw), row(hw), row(d),
                  pl.BlockSpec((hw, d), lambda bi, mi: (0, 0)),
                  pl.BlockSpec((hw, d), lambda bi, mi: (1, 0)),
                  vec, vec, vec,
                  pl.BlockSpec((1, d), lambda bi, mi: (0, 0)),
                  pl.BlockSpec((d, 256), lambda bi, mi: (0, 0)),
                  pl.BlockSpec((1, 128), lambda bi, mi: (0, 0))],
        out_specs=(row(d), row(d // 2), row(128)),
        compiler_params=_cparams(("parallel", "parallel"), 48),
        name="outproj",
    )(a_na, a_hg, x, w_out_bf, w_out_bf, g1, sh2, sc2, gain, wr, br)


CUMSUM_TM = 512


def _cumsum_kernel(route_ref, tri_ref, o_ref, carry):
    @pl.when(pl.program_id(0) == 0)
    def _():
        carry[...] = jnp.zeros_like(carry)

    r = route_ref[...]
    lane = lax.broadcasted_iota(jnp.int32, r.shape, 1)
    member = jnp.where((lane >= MEMBER_LANE) & (lane < MEMBER_LANE + N_EXPERTS), r, 0.0).astype(bf16)
    incl = jnp.dot(tri_ref[...], member, preferred_element_type=f32) + carry[...]
    o_ref[...] = incl
    carry[...] = incl[CUMSUM_TM - 1:CUMSUM_TM, :]


def _token_cumsum(route):
    t = route.shape[0]
    tri = jnp.asarray(np.tril(np.ones((CUMSUM_TM, CUMSUM_TM), np.float32)), bf16)
    return pl.pallas_call(
        _cumsum_kernel,
        out_shape=jax.ShapeDtypeStruct((t, 128), f32),
        grid=(t // CUMSUM_TM,),
        in_specs=[pl.BlockSpec((CUMSUM_TM, 128), lambda i: (i, 0)),
                  pl.BlockSpec((CUMSUM_TM, CUMSUM_TM), lambda i: (0, 0))],
        out_specs=pl.BlockSpec((CUMSUM_TM, 128), lambda i: (i, 0)),
        scratch_shapes=[pltpu.VMEM((1, 128), f32)],
        compiler_params=_cparams(("arbitrary",)),
        name="cumsum",
    )(route, tri)


DISPATCH_TM = 256
SUBLANES = 8


def _dispatch_kernel(pe_ref, dest_ref, dest_prev_ref, hp_ref, xs_ref, zero_scr, sem, stage, sems):
    @pl.when(pl.program_id(0) == 0)
    def _():
        zero_scr[...] = jnp.zeros_like(zero_scr)

        def tail_copy(e):
            start = pl.multiple_of(pe_ref[e] - MOE_TM, MOE_TM)
            return pltpu.make_async_copy(zero_scr, xs_ref.at[pl.ds(start, MOE_TM), :], sem)

        def has_rows(e):
            return pe_ref[e] > jnp.where(e == 0, 0, pe_ref[jnp.maximum(e - 1, 0)])

        def issue_tail(e, c):
            @pl.when(has_rows(e))
            def _():
                tail_copy(e).start()
            return c

        def drain_tail(e, c):
            @pl.when(has_rows(e))
            def _():
                tail_copy(e).wait()
            return c

        lax.fori_loop(0, N_EXPERTS, issue_tail, 0)
        lax.fori_loop(0, N_EXPERTS, drain_tail, 0)

        def dead_copy(m):
            return pltpu.make_async_copy(zero_scr, xs_ref.at[pl.ds(pl.multiple_of(m * MOE_TM, MOE_TM), MOE_TM), :], sem)

        def issue_dead(m, c):
            dead_copy(m).start()
            return c

        def drain_dead(m, c):
            dead_copy(m).wait()
            return c

        n_used = pe_ref[N_EXPERTS - 1] // MOE_TM
        lax.fori_loop(n_used, xs_ref.shape[0] // MOE_TM, issue_dead, 0)
        lax.fori_loop(n_used, xs_ref.shape[0] // MOE_TM, drain_dead, 0)

    step = pl.program_id(0)
    cur = lax.rem(step, 2)
    prev = 1 - cur

    def row_copy(d_ref, b, g, u, k):
        slot = d_ref[(g * SUBLANES + u) * TOP_K + k]
        return pltpu.make_async_copy(stage.at[b, g, pl.ds(u, 1), :], xs_ref.at[pl.ds(slot, 1), :], sems.at[b])

    def loop_rows(fn):
        def body(g, c):
            for u in range(SUBLANES):
                for k in range(TOP_K):
                    fn(g, u, k)
            return c
        lax.fori_loop(0, DISPATCH_TM // SUBLANES, body, 0)

    stage[cur] = hp_ref[...]
    loop_rows(lambda g, u, k: row_copy(dest_ref, cur, g, u, k).start())

    @pl.when(step > 0)
    def _():
        loop_rows(lambda g, u, k: row_copy(dest_prev_ref, prev, g, u, k).wait())

    @pl.when(step == pl.num_programs(0) - 1)
    def _():
        loop_rows(lambda g, u, k: row_copy(dest_ref, cur, g, u, k).wait())


def _dispatch(pad_end, dest, hp, cap):
    t, kp = hp.shape
    return pl.pallas_call(
        _dispatch_kernel,
        out_shape=jax.ShapeDtypeStruct((cap, kp), hp.dtype),
        grid_spec=pltpu.PrefetchScalarGridSpec(
            num_scalar_prefetch=1,
            grid=(t // DISPATCH_TM,),
            in_specs=[pl.BlockSpec((DISPATCH_TM * TOP_K,), lambda i, pe: (i,), memory_space=pltpu.SMEM),
                      pl.BlockSpec((DISPATCH_TM * TOP_K,), lambda i, pe: (jnp.maximum(i - 1, 0),),
                                   memory_space=pltpu.SMEM),
                      pl.BlockSpec((DISPATCH_TM // SUBLANES, SUBLANES, kp), lambda i, pe: (i, 0, 0))],
            out_specs=pl.BlockSpec(memory_space=pl.ANY),
            scratch_shapes=[pltpu.VMEM((MOE_TM, kp), hp.dtype), pltpu.SemaphoreType.DMA(()),
                            pltpu.VMEM((2, DISPATCH_TM // SUBLANES, SUBLANES, kp), hp.dtype),
                            pltpu.SemaphoreType.DMA((2,))]),
        compiler_params=_cparams(("arbitrary",)),
        name="dispatch",
    )(pad_end, dest, dest, hp.reshape(t // SUBLANES, SUBLANES, kp))


def _block_halves(m, half2_ref, nu_ref, out_ref, prepare, half):
    live = m < nu_ref[0]
    second = jnp.logical_and(live, half2_ref[m] == 1)
    zeros = jnp.zeros((MOE_HALF, out_ref.shape[1]), out_ref.dtype)

    @pl.when(live)
    def _():
        prepare()

    @pl.when(second)
    def _():
        half(pl.ds(0, MOE_TM))

    @pl.when(jnp.logical_and(live, jnp.logical_not(second)))
    def _():
        half(pl.ds(0, MOE_HALF))

    @pl.when(jnp.logical_not(live))
    def _():
        out_ref[pl.ds(0, MOE_HALF), :] = zeros

    @pl.when(jnp.logical_not(second))
    def _():
        out_ref[pl.ds(MOE_HALF, MOE_HALF), :] = zeros


def _cast_chunks(w_refs, w_bf):
    rows = w_refs[0].shape[0]
    for i, w_ref in enumerate(w_refs):
        w_bf[pl.ds(i * rows, rows), :] = w_ref[...].astype(bf16)


def _gemm1_kernel(be_ref, first_ref, half2_ref, nu_ref, xs_ref, *refs):
    wg_refs, wl_refs = refs[:W_STREAMS], refs[W_STREAMS:2 * W_STREAMS]
    bg_ref, bl_ref, act_ref, wg_bf, wl_bf = refs[2 * W_STREAMS:]
    m = pl.program_id(1)

    def prepare():
        @pl.when(first_ref[m] == 1)
        def _():
            _cast_chunks(wg_refs, wg_bf)
            _cast_chunks(wl_refs, wl_bf)

    def half(rows):
        lo, hi = _unpack_halves(xs_ref[rows, :])
        lo = lo.astype(bf16)
        hi = hi.astype(bf16)
        k = lo.shape[-1]
        gate = (jnp.dot(lo, wg_bf[:k, :], preferred_element_type=f32)
                + jnp.dot(hi, wg_bf[k:, :], preferred_element_type=f32) + bg_ref[...])
        lin = (jnp.dot(lo, wl_bf[:k, :], preferred_element_type=f32)
               + jnp.dot(hi, wl_bf[k:, :], preferred_element_type=f32) + bl_ref[...])
        x_glu = jnp.minimum(gate, SWIGLU_LIMIT)
        x_lin = jnp.clip(lin, -SWIGLU_LIMIT, SWIGLU_LIMIT)
        act_ref[rows, :] = (x_glu * jax.nn.sigmoid(SWIGLU_ALPHA * x_glu) * (x_lin + 1.0)).astype(act_ref.dtype)

    _block_halves(m, half2_ref, nu_ref, act_ref, prepare, half)


def _gemm1(block_exp, first, half2, n_used, xs, w_gu, b_gu, tn):
    cap, kp = xs.shape
    n_e, d, f2 = w_gu.shape
    ff = f2 // 2
    tn = min(tn, ff)
    nb = cap // MOE_TM
    nj = ff // tn
    live = lambda m, nu: jnp.minimum(m, nu[0] - 1)
    dk = d // W_STREAMS
    wspec = lambda col0, c: pl.BlockSpec((None, dk, tn), lambda j, m, be, fi, h2, nu: (be[m], c, col0 + j))
    return pl.pallas_call(
        _gemm1_kernel,
        out_shape=jax.ShapeDtypeStruct((cap, ff), bf16),
        grid_spec=pltpu.PrefetchScalarGridSpec(
            num_scalar_prefetch=4,
            grid=(nj, nb),
            in_specs=[pl.BlockSpec((MOE_TM, kp), lambda j, m, be, fi, h2, nu: (live(m, nu), 0))]
                     + [wspec(0, c) for c in range(W_STREAMS)] + [wspec(nj, c) for c in range(W_STREAMS)]
                     + [pl.BlockSpec((None, 1, tn), lambda j, m, be, fi, h2, nu: (be[m], 0, j)),
                        pl.BlockSpec((None, 1, tn), lambda j, m, be, fi, h2, nu: (be[m], 0, nj + j))],
            out_specs=pl.BlockSpec((MOE_TM, tn), lambda j, m, be, fi, h2, nu: (m, j)),
            scratch_shapes=[pltpu.VMEM((d, tn), bf16), pltpu.VMEM((d, tn), bf16)]),
        compiler_params=_cparams(("arbitrary", "arbitrary"), 56),
        name="gemm1",
    )(block_exp, first, half2, n_used, xs, *([w_gu] * (2 * W_STREAMS)),
      b_gu.reshape(n_e, 1, f2), b_gu.reshape(n_e, 1, f2))


def _gemm2_kernel(be_ref, first_ref, half2_ref, nu_ref, act_ref, *refs):
    w_refs = refs[:W_STREAMS]
    b_ref, y_ref, w_bf = refs[W_STREAMS:]
    m = pl.program_id(0)

    def prepare():
        @pl.when(first_ref[m] == 1)
        def _():
            _cast_chunks(w_refs, w_bf)

    def half(rows):
        y = jnp.dot(act_ref[rows, :], w_bf[...], preferred_element_type=f32) + b_ref[...]
        y_ref[rows, :] = _pack_halves(y.astype(bf16))

    _block_halves(m, half2_ref, nu_ref, y_ref, prepare, half)


def _gemm2(block_exp, first, half2, n_used, act, w_down, b_down):
    cap, ff = act.shape
    n_e, _, d = w_down.shape
    nb = cap // MOE_TM
    live = lambda m, nu: jnp.minimum(m, nu[0] - 1)
    wspec = lambda c: pl.BlockSpec((None, ff // W_STREAMS, d), lambda m, be, fi, h2, nu: (be[m], c, 0))
    return pl.pallas_call(
        _gemm2_kernel,
        out_shape=jax.ShapeDtypeStruct((cap, d // 2), u32),
        grid_spec=pltpu.PrefetchScalarGridSpec(
            num_scalar_prefetch=4,
            grid=(nb,),
            in_specs=[pl.BlockSpec((MOE_TM, ff), lambda m, be, fi, h2, nu: (live(m, nu), 0))]
                     + [wspec(c) for c in range(W_STREAMS)]
                     + [pl.BlockSpec((None, 1, d), lambda m, be, fi, h2, nu: (be[m], 0, 0))],
            out_specs=pl.BlockSpec((MOE_TM, d // 2), lambda m, be, fi, h2, nu: (m, 0)),
            scratch_shapes=[pltpu.VMEM((ff, d), bf16)]),
        compiler_params=_cparams(("arbitrary",), 56),
        name="gemm2",
    )(block_exp, first, half2, n_used, act, *([w_down] * W_STREAMS), b_down.reshape(n_e, 1, d))


COMBINE_TM = 256


def _combine_kernel(dest_ref, dest_next_ref, yb_ref, route_ref, x1_ref, g2_ref, gain_ref, o_ref, buf, sems):
    step = pl.program_id(0)
    cur = lax.rem(step, 2)
    nxt = 1 - cur
    n_groups = COMBINE_TM // SUBLANES

    def row_copy(d_ref, b, g, u, k):
        slot = d_ref[(g * SUBLANES + u) * TOP_K + k]
        return pltpu.make_async_copy(yb_ref.at[pl.ds(slot, 1), :], buf.at[b, k, g, pl.ds(u, 1), :], sems.at[b])

    def each_row(g, fn):
        for u in range(SUBLANES):
            for k in range(TOP_K):
                fn(g, u, k)

    def loop_rows(fn):
        def body(g, c):
            each_row(g, fn)
            return c
        lax.fori_loop(0, n_groups, body, 0)

    @pl.when(step == 0)
    def _():
        loop_rows(lambda g, u, k: row_copy(dest_ref, cur, g, u, k).start())

    loop_rows(lambda g, u, k: row_copy(dest_ref, cur, g, u, k).wait())

    for g in range(n_groups):
        each_row(g, lambda g, u, k: row_copy(dest_next_ref, nxt, g, u, k).start())

    route = route_ref[...]
    acc = None
    for k in range(TOP_K):
        lo, hi = _unpack_halves(buf[cur, k].reshape(COMBINE_TM, buf.shape[-1]))
        term = route[:, TOP_K + k:TOP_K + k + 1] * jnp.concatenate([lo, hi], axis=-1)
        acc = term if acc is None else acc + term
    x2 = x1_ref[...] + g2_ref[...] * acc
    ms = jnp.mean(x2 * x2, axis=-1, keepdims=True)
    o_ref[...] = x2 * lax.rsqrt(ms + EPS) * gain_ref[...]

    @pl.when(step == pl.num_programs(0) - 1)
    def _():
        loop_rows(lambda g, u, k: row_copy(dest_next_ref, nxt, g, u, k).wait())


def _combine(dest, yb, route, x1, g2, gain):
    b, l, d = x1.shape
    tm = min(COMBINE_TM, l)
    assert tm == COMBINE_TM
    per_b = l // tm
    n_tiles = b * per_b
    return pl.pallas_call(
        _combine_kernel,
        out_shape=jax.ShapeDtypeStruct((b, l, d), f32),
        grid=(n_tiles,),
        in_specs=[pl.BlockSpec((tm * TOP_K,), lambda i: (i,), memory_space=pltpu.SMEM),
                  pl.BlockSpec((tm * TOP_K,), lambda i: (jnp.minimum(i + 1, n_tiles - 1),), memory_space=pltpu.SMEM),
                  pl.BlockSpec(memory_space=pl.ANY),
                  pl.BlockSpec((None, tm, 128), lambda i: (i // per_b, i % per_b, 0)),
                  pl.BlockSpec((None, tm, d), lambda i: (i // per_b, i % per_b, 0)),
                  pl.BlockSpec((None, 1, d), lambda i: (i // per_b, 0, 0)),
                  pl.BlockSpec((1, d), lambda i: (0, 0))],
        out_specs=pl.BlockSpec((None, tm, d), lambda i: (i // per_b, i % per_b, 0)),
        scratch_shapes=[pltpu.VMEM((2, TOP_K, tm // SUBLANES, SUBLANES, d // 2), u32),
                        pltpu.SemaphoreType.DMA((2,))],
        compiler_params=_cparams(("arbitrary",), 40),
        name="combine",
    )(dest, dest, yb, route, x1, g2, gain)


def _routing(route, n_blocks):
    experts = slice(MEMBER_LANE, MEMBER_LANE + N_EXPERTS)
    top_idx = route[:, :TOP_K].astype(jnp.int32)
    incl = _token_cumsum(route)[:, experts]
    rank = (incl - route[:, experts]).astype(jnp.int32)
    counts = incl[-1].astype(jnp.int32)
    padded = (counts + MOE_TM - 1) // MOE_TM * MOE_TM
    pad_end = jnp.cumsum(padded).astype(jnp.int32)
    slot0 = (pad_end - padded)[None, :] + rank
    onehot = top_idx[:, :, None] == jnp.arange(N_EXPERTS, dtype=jnp.int32)[None, None, :]
    dest = jnp.sum(jnp.where(onehot, slot0[:, None, :], 0), axis=-1).astype(jnp.int32)
    block_start = jnp.arange(n_blocks, dtype=jnp.int32) * MOE_TM
    block_exp = jnp.minimum(jnp.sum(pad_end[None, :] <= block_start[:, None], axis=1), N_EXPERTS - 1).astype(jnp.int32)
    first = jnp.concatenate([jnp.ones((1,), jnp.int32), (block_exp[1:] != block_exp[:-1]).astype(jnp.int32)])
    n_used = (pad_end[-1:] // MOE_TM).astype(jnp.int32)
    token_end = (pad_end - padded + counts)[block_exp]
    half2 = (token_end > block_start + MOE_HALF).astype(jnp.int32)
    return dest.reshape(-1), pad_end, block_exp, first, half2, n_used


def _moe(hp, route, x1, g2, norm_final, w_gu, b_gu, w_down, b_down):
    b, l, d = x1.shape
    t = b * l
    n_blocks = -(-(t * TOP_K + N_EXPERTS * (MOE_TM - 1)) // MOE_TM)
    dest, pad_end, block_exp, first, half2, n_used = _routing(route.reshape(t, 128), n_blocks)
    xs = _dispatch(pad_end, dest, hp.reshape(t, d // 2), n_blocks * MOE_TM)
    act = _gemm1(block_exp, first, half2, n_used, xs, w_gu, b_gu, 1024)
    yb = _gemm2(block_exp, first, half2, n_used, act, w_down, b_down)
    return _combine(dest, yb, route, x1, g2, norm_final.reshape(1, d))


def kernel(x, c, ctx, c_ctx, w_ada, b_ada, norm_mix, norm_ffn, w_in, lb_table, hg_norm, rpb, w_out, w_router,
           b_router, w_gu, b_gu, w_down, b_down, norm_final):
    b, l, d = x.shape
    assert w_ada.shape[0] == 1, "single-layer block"
    rows = l // GRID_W

    c16 = jnp.zeros((16, d), f32).at[:b].set(c).at[b].set(c_ctx)
    mod = _ada(c16, w_ada[0], b_ada[0])
    sh1, sc1, g1, sh2, sc2, g2 = [mod[:b, i * d:(i + 1) * d].reshape(b, 1, d) for i in range(6)]
    csh = jnp.broadcast_to(mod[b, :d].reshape(1, 1, d), (b, 1, d))
    csc = jnp.broadcast_to(mod[b, d:2 * d].reshape(1, 1, d), (b, 1, d))

    w_in_bf = w_in[0].astype(bf16)
    gain_mix = norm_mix[0].reshape(1, d)
    px_att = _inproj(x, sh1, sc1, gain_mix, w_in_bf, P_QN, 3, bf16, 1024)
    px_hg = _inproj(x, sh1, sc1, gain_mix, w_in_bf, P_QH, 5, f32, 1024)
    pc_att = _inproj(ctx, csh, csc, gain_mix, w_in_bf, P_KN, 2, bf16, 256)
    pc_hg = _inproj(ctx, csh, csc, gain_mix, w_in_bf, P_FF, 3, f32, 256)

    o_na = _natten(px_att, pc_att, _natten_bias_table(rpb[0], rows))

    lower_bounds = jnp.cumsum(jax.nn.softmax(lb_table.astype(f32), axis=0), axis=0)
    o_hg = _hgrn(px_hg, pc_hg, lower_bounds[0].reshape(2, HEADS * HEAD_DIM), hg_norm[0])

    x1, hp, route = _outproj(o_na, o_hg, x, w_out[0].astype(bf16), g1, sh2, sc2, norm_ffn[0].reshape(1, d),
                             w_router[0], b_router[0], 256)
    return _moe(hp, route, x1, g2, norm_final, w_gu[0], b_gu[0], w_down[0], b_down[0])
```

```python
import functools

import numpy as np
import jax
import jax.numpy as jnp
from jax import lax
from jax.experimental import pallas as pl
from jax.experimental.pallas import tpu as pltpu

f32 = jnp.float32
bf16 = jnp.bfloat16
u32 = jnp.uint32

GRID_W = 64
HEADS = 8
HEAD_DIM = 128
WIN_R = 8
WIN_C = 16
ROPE_THETA = 10000.0
N_EXPERTS = 32
TOP_K = 4
SWIGLU_LIMIT = 7.0
SWIGLU_ALPHA = 1.702
EPS = 1e-6
P_QN, P_KN, P_VN, P_QH, P_FF, P_FB, P_IH, P_GH = range(8)

CHUNK = 64
SUB = 16
CHUNK_UNROLL = 8
ROW_UNROLL = 8
MOE_TM = 512
MOE_HALF = 256
W_STREAMS = 1
NEG_BIG = -1e30
MEMBER_LANE = 32
HI_MASK = 0xFFFF0000

_HIGHEST = lax.Precision.HIGHEST


def _cparams(sem, vmem_mb=None):
    kw = dict(dimension_semantics=sem)
    if vmem_mb is not None:
        kw["vmem_limit_bytes"] = vmem_mb * 1024 * 1024
    return pltpu.CompilerParams(**kw)


def _ada_kernel(c_ref, w_ref, b_ref, o_ref):
    c = c_ref[...]
    cond = c * jax.nn.sigmoid(c)
    o_ref[...] = jnp.dot(cond, w_ref[...], precision=_HIGHEST, preferred_element_type=f32) + b_ref[...]


def _ada(c16, w_ada, b_ada):
    d, n = w_ada.shape
    tn = min(2048, n)
    return pl.pallas_call(
        _ada_kernel,
        out_shape=jax.ShapeDtypeStruct((c16.shape[0], n), f32),
        grid=(n // tn,),
        in_specs=[pl.BlockSpec((c16.shape[0], d), lambda j: (0, 0)),
                  pl.BlockSpec((d, tn), lambda j: (0, j)),
                  pl.BlockSpec((1, tn), lambda j: (0, j))],
        out_specs=pl.BlockSpec((c16.shape[0], tn), lambda j: (0, j)),
        compiler_params=_cparams(("parallel",), 40),
        name="ada",
    )(c16, w_ada, b_ada.reshape(1, n))


def _inproj_kernel(x_ref, shift_ref, scale_ref, gain_ref, w_ref, o_ref, h_scr):
    @pl.when(pl.program_id(2) == 0)
    def _():
        x = x_ref[...]
        ms = jnp.mean(x * x, axis=-1, keepdims=True)
        y = x * lax.rsqrt(ms + EPS) * gain_ref[...]
        h_scr[...] = (y * (1.0 + scale_ref[...]) + shift_ref[...]).astype(bf16)

    acc = jnp.dot(h_scr[...], w_ref[...], preferred_element_type=f32)
    for hh in range(HEADS):
        o_ref[hh] = acc[:, hh * HEAD_DIM:(hh + 1) * HEAD_DIM].astype(o_ref.dtype)


def _inproj(x, shift, scale, gain, w_bf, part_lo, n_parts, out_dtype, tm):
    b, l, d = x.shape
    pw = HEADS * HEAD_DIM
    tm = min(tm, l)
    return pl.pallas_call(
        _inproj_kernel,
        out_shape=jax.ShapeDtypeStruct((b, n_parts * HEADS, l, HEAD_DIM), out_dtype),
        grid=(b, l // tm, n_parts),
        in_specs=[pl.BlockSpec((None, tm, d), lambda bi, mi, ni: (bi, mi, 0)),
                  pl.BlockSpec((None, 1, d), lambda bi, mi, ni: (bi, 0, 0)),
                  pl.BlockSpec((None, 1, d), lambda bi, mi, ni: (bi, 0, 0)),
                  pl.BlockSpec((1, d), lambda bi, mi, ni: (0, 0)),
                  pl.BlockSpec((d, pw), lambda bi, mi, ni: (0, part_lo + ni))],
        out_specs=pl.BlockSpec((None, HEADS, tm, HEAD_DIM), lambda bi, mi, ni: (bi, ni, mi, 0)),
        scratch_shapes=[pltpu.VMEM((tm, d), bf16)],
        compiler_params=_cparams(("parallel", "parallel", "arbitrary"), 48),
        name="inproj",
    )(x, shift, scale, gain, w_bf)


def _natten_bias_table(rpb, rows):
    kr = min(WIN_R, rows)
    q = np.arange(GRID_W)
    col_start = np.clip(q - WIN_C // 2, 0, GRID_W - WIN_C)
    kc = np.arange(GRID_W)
    in_win = (kc[None, :] >= col_start[:, None]) & (kc[None, :] < col_start[:, None] + WIN_C)
    d_col = np.clip(kc[None, :] - q[:, None] + WIN_C - 1, 0, 2 * WIN_C - 2)
    n_d0 = 2 * WIN_R - 1 - (kr - 1)
    onehot = (d_col[None] == np.arange(2 * WIN_C - 1)[:, None, None]).astype(np.float32)
    cols = jnp.einsum("hrc,cqk->hrqk", rpb.astype(f32), jnp.asarray(onehot), precision=_HIGHEST)
    cols = jnp.where(in_win[None, None], cols, -jnp.inf)
    t = jnp.stack([cols[:, d0:d0 + kr] for d0 in range(n_d0)], axis=1)
    t = jnp.transpose(t, (0, 1, 3, 2, 4))
    return t.reshape(rpb.shape[0], n_d0, GRID_W, kr * GRID_W)


def _natten_kernel(q_ref, k_ref, v_ref, kc_ref, vc_ref, bias_ref, o_ref, sw0, sc0, sw1, sc1, *, rows, kr):
    scale = HEAD_DIM ** -0.5
    kc = kc_ref[...]
    vc = vc_ref[...]
    nt = (((1,), (1,)), ((), ()))

    n_groups = rows // ROW_UNROLL
    gq = ROW_UNROLL * GRID_W

    def key_start(r):
        return jnp.clip(r - kr // 2, 0, rows - kr)

    def scores(g, sw_scr, sc_scr):
        q_all = q_ref[pl.ds(pl.multiple_of(g * gq, gq), gq), :]
        sc_scr[...] = lax.dot_general(q_all, kc, nt, preferred_element_type=f32) * scale
        for u in range(ROW_UNROLL):
            r = g * ROW_UNROLL + u
            kr0 = key_start(r)
            kw = k_ref[pl.ds(pl.multiple_of(kr0 * GRID_W, GRID_W), kr * GRID_W), :]
            q = q_all[u * GRID_W:(u + 1) * GRID_W, :]
            sw_scr[u] = (lax.dot_general(q, kw, nt, preferred_element_type=f32) * scale
                         + bias_ref[kr0 - r + WIN_R - 1])

    def attend(g, sw_scr, sc_scr):
        s_c = sc_scr[...]
        s_w = [sw_scr[u] for u in range(ROW_UNROLL)]
        m = jnp.maximum(jnp.concatenate([jnp.max(s, axis=-1, keepdims=True) for s in s_w], axis=0),
                        jnp.max(s_c, axis=-1, keepdims=True))
        p_c = jnp.exp(s_c - m)
        p_w = [jnp.exp(s - m[u * GRID_W:(u + 1) * GRID_W, :]) for u, s in enumerate(s_w)]
        denom = (jnp.concatenate([jnp.sum(p, axis=-1, keepdims=True) for p in p_w], axis=0)
                 + jnp.sum(p_c, axis=-1, keepdims=True))
        o_c = jnp.dot(p_c.astype(bf16), vc, preferred_element_type=f32)
        o_w = []
        for u, p in enumerate(p_w):
            k0 = pl.multiple_of(key_start(g * ROW_UNROLL + u) * GRID_W, GRID_W)
            o_w.append(jnp.dot(p.astype(bf16), v_ref[pl.ds(k0, kr * GRID_W), :], preferred_element_type=f32))
        o = (jnp.concatenate(o_w, axis=0) + o_c) / denom
        o_ref[pl.ds(pl.multiple_of(g * gq, gq), gq), :] = o.astype(o_ref.dtype)

    scores(0, sw0, sc0)

    def body(i, carry):
        g = 2 * i
        scores(g + 1, sw1, sc1)
        attend(g, sw0, sc0)
        scores(jnp.minimum(g + 2, n_groups - 1), sw0, sc0)
        attend(g + 1, sw1, sc1)
        return carry

    lax.fori_loop(0, n_groups // 2, body, 0)


def _natten(px_att, pc_att, bias_tab):
    b, _, l, _ = px_att.shape
    lc = pc_att.shape[2]
    rows = l // GRID_W
    kr = min(WIN_R, rows)
    n_d0 = bias_tab.shape[1]
    blk = lambda off: pl.BlockSpec((None, None, l, HEAD_DIM), lambda h, bi: (bi, off + h, 0, 0))
    cblk = lambda off: pl.BlockSpec((None, None, lc, HEAD_DIM), lambda h, bi: (bi, off + h, 0, 0))
    return pl.pallas_call(
        functools.partial(_natten_kernel, rows=rows, kr=kr),
        out_shape=jax.ShapeDtypeStruct((b, l, HEADS * HEAD_DIM), bf16),
        grid=(HEADS, b),
        in_specs=[blk(0), blk(HEADS), blk(2 * HEADS), cblk(0), cblk(HEADS),
                  pl.BlockSpec((None, n_d0, GRID_W, kr * GRID_W), lambda h, bi: (h, 0, 0, 0))],
        out_specs=pl.BlockSpec((None, l, HEAD_DIM), lambda h, bi: (bi, 0, h)),
        scratch_shapes=[pltpu.VMEM((ROW_UNROLL, GRID_W, kr * GRID_W), f32), pltpu.VMEM((ROW_UNROLL * GRID_W, lc), f32),
                        pltpu.VMEM((ROW_UNROLL, GRID_W, kr * GRID_W), f32), pltpu.VMEM((ROW_UNROLL * GRID_W, lc), f32)],
        compiler_params=_cparams(("parallel", "parallel")),
        name="natten",
    )(px_att, px_att, px_att, pc_att, pc_att, bias_tab)


def _hgrn_consts():
    t = np.arange(CHUNK)
    bt, bs = t[:, None] // SUB, t[None, :] // SUB
    tri, masks = [], []
    for sgn in (1, -1):
        before = (t[None, :] <= t[:, None]) if sgn == 1 else (t[None, :] >= t[:, None])
        tri.append(before.astype(np.float32))
        dist = (bt - bs) * sgn
        masks.append(np.stack([dist == 1, dist == 2, dist == 3, (dist == 0) & before]).astype(np.float32))
    return np.stack(tri), np.stack(masks)


def _rope_tables(l):
    t = jnp.arange(l)
    n_freq = HEAD_DIM // 4
    inv_freq = ROPE_THETA ** (-jnp.arange(n_freq, dtype=f32) / n_freq)
    ang_row = (t // GRID_W).astype(f32)[:, None] * inv_freq
    ang_col = (t % GRID_W).astype(f32)[:, None] * inv_freq
    cr, sr, cc, sc = jnp.cos(ang_row), jnp.sin(ang_row), jnp.cos(ang_col), jnp.sin(ang_col)
    z = jnp.zeros_like(sr)
    cos_t = jnp.concatenate([cr, cr, cc, cc], axis=-1)
    sin_up = jnp.concatenate([-sr, z, -sc, z], axis=-1)
    sin_dn = jnp.concatenate([z, sr, z, sc], axis=-1)
    return cos_t, sin_up, sin_dn


def _split3_dot(tri_bf, g):
    g1 = g.astype(bf16)
    r1 = g - g1.astype(f32)
    g2 = r1.astype(bf16)
    g3 = (r1 - g2.astype(f32)).astype(bf16)
    dot = lambda a: jnp.dot(tri_bf, a, preferred_element_type=f32)
    return dot(g1) + dot(g2) + dot(g3)


def _gate(f_raw, lb):
    log_f = jnp.log(lb + (1.0 - lb) * jax.nn.sigmoid(f_raw))
    key = (1.0 - lb) * jax.nn.sigmoid(-f_raw)
    return log_f, key


def _chunk_refs(cum, backward):
    nb = CHUNK // SUB
    if backward:
        ends = [cum[i * SUB:i * SUB + 1, :] for i in range(nb)]
        order = list(range(nb - 1, -1, -1))
    else:
        ends = [cum[i * SUB + SUB - 1:i * SUB + SUB, :] for i in range(nb)]
        order = list(range(nb))
    zero = jnp.zeros_like(ends[0])
    b_rows, g2_rows, g3_rows = [None] * nb, [None] * nb, [None] * nb
    for pos, i in enumerate(order):
        b_i = zero if pos == 0 else ends[order[pos - 1]]
        b_rows[i] = b_i
        g2_rows[i] = b_i - ends[order[pos - 2]] if pos >= 2 else zero
        g3_rows[i] = b_i - ends[order[pos - 3]] if pos >= 3 else zero
    expand = lambda rws: jnp.concatenate([jnp.broadcast_to(r, (SUB, HEAD_DIM)) for r in rws], axis=0)
    total = ends[order[-1]]
    return expand(b_rows), expand(ends), expand(g2_rows), expand(g3_rows), total


def _rope(x, cos_t, sin_up, sin_dn):
    return x * cos_t + pltpu.roll(x, 96, 1) * sin_up + pltpu.roll(x, 32, 1) * sin_dn


def _scan_group(states, items, tris, masks):
    nt = (((1,), (1,)), ((), ()))
    tn = (((0,), (0,)), ((), ()))
    pre = []
    for d, q, f_raw, v, lb, rope in items:
        g, key = _gate(f_raw, lb)
        pre.append((g, _rope(q, *rope), _rope(key, *rope), v.astype(bf16)))
    cums = [_split3_dot(tris[it[0]], p[0]) for it, p in zip(items, pre)]
    ops = []
    for it, (g, qr, kr, v_bf), cum in zip(items, pre, cums):
        b, e, gap2, gap3, total = _chunk_refs(cum, it[0] == 1)
        q_t = qr * jnp.exp(cum - b)
        k_hat = kr * jnp.exp(e - cum)
        k_til = kr * jnp.exp(b - cum)
        lhs = jnp.concatenate([q_t, q_t * jnp.exp(gap2), q_t * jnp.exp(gap3)], axis=0).astype(bf16)
        rhs = jnp.concatenate([k_hat, k_til], axis=0).astype(bf16)
        q_in = (q_t * jnp.exp(b)).astype(bf16)
        k_dec = (k_hat * jnp.exp(total - e)).astype(bf16)
        ops.append((lhs, rhs, q_in, k_dec, jnp.exp(total)))
    scores = [lax.dot_general(o[0], o[1], nt, preferred_element_type=f32) for o in ops]
    upds = [lax.dot_general(p[3], o[3], tn, preferred_element_type=f32) for p, o in zip(pre, ops)]
    intra = []
    for it, p, (g, qr, kr, v_bf) in zip(items, scores, pre):
        m = masks[it[0]]
        att = (jnp.where(m[0] > 0, p[0:CHUNK, 0:CHUNK], 0.0)
               + jnp.where(m[1] > 0, p[CHUNK:2 * CHUNK, 0:CHUNK], 0.0)
               + jnp.where(m[2] > 0, p[2 * CHUNK:3 * CHUNK, 0:CHUNK], 0.0)
               + jnp.where(m[3] > 0, p[0:CHUNK, CHUNK:2 * CHUNK], 0.0))
        intra.append(jnp.dot(att.astype(bf16), v_bf, preferred_element_type=f32))
    states = list(states)
    outs = []
    for it, o, upd, o_in in zip(items, ops, upds, intra):
        st = states[it[0]]
        outs.append(o_in + lax.dot_general(o[2], st.astype(bf16), nt, preferred_element_type=f32))
        states[it[0]] = st * o[4] + upd
    return states, outs


def _hgrn_kernel(q_ref, ff_ref, fb_ref, v_ref, gate_ref, cff_ref, cfb_ref, cv_ref, lb_ref, gain_ref,
                 cos_ref, sup_ref, sdn_ref, tri_ref, mask_ref, o_ref, of_scr, ob_scr, *, n_chunks, n_cchunks):
    lb_f = lb_ref[0]
    lb_b = lb_ref[1]
    tri_f = tri_ref[0].astype(bf16)
    tri_b = tri_ref[1].astype(bf16)
    zero = jnp.zeros((HEAD_DIM, HEAD_DIM), f32)

    ctx_items = []
    for c in range(n_cchunks):
        rf = pl.ds(c * CHUNK, CHUNK)
        rb = pl.ds((n_cchunks - 1 - c) * CHUNK, CHUNK)
        ctx_items.append((0, cv_ref[rf, :], cff_ref[rf, :], lb_f))
        ctx_items.append((1, cv_ref[rb, :], cfb_ref[rb, :], lb_b))
    st_f, st_b = _state_group([zero, zero], ctx_items, (tri_f, tri_b))

    def body(c, carry):
        masks = [[mask_ref[d, i] for i in range(4)] for d in range(2)]
        items, rows = [], []
        for u in range(CHUNK_UNROLL):
            cf = c * CHUNK_UNROLL + u
            rf = pl.ds(pl.multiple_of(cf * CHUNK, CHUNK), CHUNK)
            rb = pl.ds(pl.multiple_of((n_chunks - 1 - cf) * CHUNK, CHUNK), CHUNK)
            items.append((0, q_ref[rf, :], ff_ref[rf, :], v_ref[rf, :], lb_f,
                          (cos_ref[rf, :], sup_ref[rf, :], sdn_ref[rf, :])))
            items.append((1, q_ref[rb, :], fb_ref[rb, :], v_ref[rb, :], lb_b,
                          (cos_ref[rb, :], sup_ref[rb, :], sdn_ref[rb, :])))
            rows += [(of_scr, rf), (ob_scr, rb)]
        states, outs = _scan_group(carry, items, (tri_f, tri_b), masks)
        for (scr, rws), o in zip(rows, outs):
            scr[rws, :] = o
        return tuple(states)

    lax.fori_loop(0, n_chunks // CHUNK_UNROLL, body, (st_f, st_b))

    o = of_scr[...] + ob_scr[...]
    y = o * lax.rsqrt(jnp.mean(o * o, axis=-1, keepdims=True) + EPS) * gain_ref[...]
    gate = gate_ref[...]
    o_ref[...] = (y * (gate * jax.nn.sigmoid(gate))).astype(o_ref.dtype)


def _state_group(states, items, tris):
    gates = [_gate(f_raw, lb) for _, _, f_raw, lb in items]
    cums = [_split3_dot(tris[it[0]], g) for it, (g, _) in zip(items, gates)]
    terms = []
    for it, (_, key), cum in zip(items, gates, cums):
        total = cum[0:1, :] if it[0] == 1 else cum[CHUNK - 1:CHUNK, :]
        terms.append(((key * jnp.exp(total - cum)).astype(bf16), jnp.exp(total)))
    upds = [lax.dot_general(it[1].astype(bf16), k_dec, (((0,), (0,)), ((), ())), preferred_element_type=f32)
            for it, (k_dec, _) in zip(items, terms)]
    states = list(states)
    for it, (_, decay), upd in zip(items, terms, upds):
        states[it[0]] = states[it[0]] * decay + upd
    return states


def _hgrn(px_hg, pc_hg, lb2, hg_gain):
    b, _, l, _ = px_hg.shape
    lc = pc_hg.shape[2]
    assert l % (CHUNK * CHUNK_UNROLL) == 0 and lc % CHUNK == 0
    cos_t, sin_up, sin_dn = _rope_tables(l)
    tri, masks = _hgrn_consts()
    blk = lambda off: pl.BlockSpec((None, None, l, HEAD_DIM), lambda h, bi: (bi, off + h, 0, 0))
    cblk = lambda off: pl.BlockSpec((None, None, lc, HEAD_DIM), lambda h, bi: (bi, off + h, 0, 0))
    full = lambda shp: pl.BlockSpec(shp, lambda h, bi: (0,) * len(shp))
    return pl.pallas_call(
        functools.partial(_hgrn_kernel, n_chunks=l // CHUNK, n_cchunks=lc // CHUNK),
        out_shape=jax.ShapeDtypeStruct((b, l, HEADS * HEAD_DIM), bf16),
        grid=(HEADS, b),
        in_specs=[blk(0), blk(HEADS), blk(2 * HEADS), blk(3 * HEADS), blk(4 * HEADS),
                  cblk(0), cblk(HEADS), cblk(2 * HEADS),
                  pl.BlockSpec((2, None, 1, HEAD_DIM), lambda h, bi: (0, h, 0, 0)),
                  full((1, HEAD_DIM)),
                  full((l, HEAD_DIM)), full((l, HEAD_DIM)), full((l, HEAD_DIM)),
                  full((2, CHUNK, CHUNK)), full((2, 4, CHUNK, CHUNK))],
        out_specs=pl.BlockSpec((None, l, HEAD_DIM), lambda h, bi: (bi, 0, h)),
        scratch_shapes=[pltpu.VMEM((l, HEAD_DIM), f32), pltpu.VMEM((l, HEAD_DIM), f32)],
        compiler_params=_cparams(("parallel", "parallel")),
        name="hgrn",
    )(px_hg, px_hg, px_hg, px_hg, px_hg, pc_hg, pc_hg, pc_hg,
      lb2.reshape(2, HEADS, 1, HEAD_DIM), hg_gain.reshape(1, HEAD_DIM),
      cos_t, sin_up, sin_dn, jnp.asarray(tri), jnp.asarray(masks))


def _pack_halves(h):
    k = h.shape[-1] // 2
    lo = lax.bitcast_convert_type(h[:, :k].astype(f32), u32)
    hi = lax.bitcast_convert_type(h[:, k:].astype(f32), u32)
    return (lo >> 16) | (hi & u32(HI_MASK))


def _unpack_halves(u):
    lo = lax.bitcast_convert_type(u << 16, f32)
    hi = lax.bitcast_convert_type(u & u32(HI_MASK), f32)
    return lo, hi


def _outproj_kernel(ana_ref, ahg_ref, x_ref, w0_ref, w1_ref, g1_ref, sh_ref, sc_ref, gain_ref, wr_ref, br_ref,
                    x1_ref, hp_ref, route_ref):
    y = (jnp.dot(ana_ref[...], w0_ref[...], preferred_element_type=f32)
         + jnp.dot(ahg_ref[...], w1_ref[...], preferred_element_type=f32))
    x1 = x_ref[...] + g1_ref[...] * y
    x1_ref[...] = x1
    ms = jnp.mean(x1 * x1, axis=-1, keepdims=True)
    h = x1 * lax.rsqrt(ms + EPS) * gain_ref[...]
    h = h * (1.0 + sc_ref[...]) + sh_ref[...]
    h_hi = h.astype(bf16)
    hp_ref[...] = _pack_halves(h_hi)
    h_lo = (h - h_hi.astype(f32)).astype(bf16)
    t = jnp.dot(h_hi, wr_ref[...], preferred_element_type=f32)
    logits = (t[:, :128] + t[:, 128:] + jnp.dot(h_lo, wr_ref[:, :128], preferred_element_type=f32)) + br_ref[...]
    lane = lax.broadcasted_iota(jnp.int32, logits.shape, 1).astype(f32)
    cur = logits
    vals, idxs = [], []
    for _ in range(TOP_K):
        m = jnp.max(cur, axis=-1, keepdims=True)
        i = jnp.min(jnp.where(cur == m, lane, float(logits.shape[-1])), axis=-1, keepdims=True)
        vals.append(m)
        idxs.append(i)
        cur = jnp.where(lane == i, -jnp.inf, cur)
    es = [jnp.exp(v - vals[0]) for v in vals]
    denom = es[0] + es[1] + es[2] + es[3]
    route = jnp.zeros(logits.shape, f32)
    for k in range(TOP_K):
        route = jnp.where(lane == k, idxs[k], route)
        route = jnp.where(lane == TOP_K + k, es[k] / denom, route)
        route = jnp.where(lane - float(MEMBER_LANE) == idxs[k], 1.0, route)
    route_ref[...] = route


def _outproj(a_na, a_hg, x, w_out_bf, g1, sh2, sc2, gain, w_router, b_router, tm):
    b, l, d = x.shape
    hw = a_na.shape[-1]
    tm = min(tm, l)
    n_e = w_router.shape[-1]
    wr = jnp.zeros((d, 128), f32).at[:, :n_e].set(w_router)
    wr_hi = wr.astype(bf16)
    wr = jnp.concatenate([wr_hi, (wr - wr_hi.astype(f32)).astype(bf16)], axis=1)
    br = jnp.full((1, 128), NEG_BIG, f32).at[0, :n_e].set(b_router)
    row = lambda last: pl.BlockSpec((None, tm, last), lambda bi, mi: (bi, mi, 0))
    vec = pl.BlockSpec((None, 1, d), lambda bi, mi: (bi, 0, 0))
    return pl.pallas_call(
        _outproj_kernel,
        out_shape=(jax.ShapeDtypeStruct((b, l, d), f32),
                   jax.ShapeDtypeStruct((b, l, d // 2), u32),
                   jax.ShapeDtypeStruct((b, l, 128), f32)),
        grid=(b, l // tm),
        in_specs=[row(hw), row(hw), row(d),
                  pl.BlockSpec((hw, d), lambda bi, mi: (0, 0)),
                  pl.BlockSpec((hw, d), lambda bi, mi: (1, 0)),
                  vec, vec, vec,
                  pl.BlockSpec((1, d), lambda bi, mi: (0, 0)),
                  pl.BlockSpec((d, 256), lambda bi, mi: (0, 0)),
                  pl.BlockSpec((1, 128), lambda bi, mi: (0, 0))],
        out_specs=(row(d), row(d // 2), row(128)),
        compiler_params=_cparams(("parallel", "parallel"), 48),
        name="outproj",
    )(a_na, a_hg, x, w_out_bf, w_out_bf, g1, sh2, sc2, gain, wr, br)


CUMSUM_TM = 512


def _cumsum_kernel(route_ref, tri_ref, o_ref, carry):
    @pl.when(pl.program_id(0) == 0)
    def _():
        carry[...] = jnp.zeros_like(carry)

    r = route_ref[...]
    lane = lax.broadcasted_iota(jnp.int32, r.shape, 1)
    member = jnp.where((lane >= MEMBER_LANE) & (lane < MEMBER_LANE + N_EXPERTS), r, 0.0).astype(bf16)
    incl = jnp.dot(tri_ref[...], member, preferred_element_type=f32) + carry[...]
    o_ref[...] = incl
    carry[...] = incl[CUMSUM_TM - 1:CUMSUM_TM, :]


def _token_cumsum(route):
    t = route.shape[0]
    tri = jnp.asarray(np.tril(np.ones((CUMSUM_TM, CUMSUM_TM), np.float32)), bf16)
    return pl.pallas_call(
        _cumsum_kernel,
        out_shape=jax.ShapeDtypeStruct((t, 128), f32),
        grid=(t // CUMSUM_TM,),
        in_specs=[pl.BlockSpec((CUMSUM_TM, 128), lambda i: (i, 0)),
                  pl.BlockSpec((CUMSUM_TM, CUMSUM_TM), lambda i: (0, 0))],
        out_specs=pl.BlockSpec((CUMSUM_TM, 128), lambda i: (i, 0)),
        scratch_shapes=[pltpu.VMEM((1, 128), f32)],
        compiler_params=_cparams(("arbitrary",)),
        name="cumsum",
    )(route, tri)


DISPATCH_TM = 256
SUBLANES = 8


def _dispatch_kernel(pe_ref, dest_ref, dest_prev_ref, hp_ref, xs_ref, zero_scr, sem, stage, sems):
    @pl.when(pl.program_id(0) == 0)
    def _():
        zero_scr[...] = jnp.zeros_like(zero_scr)

        def tail_copy(e):
            start = pl.multiple_of(pe_ref[e] - MOE_TM, MOE_TM)
            return pltpu.make_async_copy(zero_scr, xs_ref.at[pl.ds(start, MOE_TM), :], sem)

        def has_rows(e):
            return pe_ref[e] > jnp.where(e == 0, 0, pe_ref[jnp.maximum(e - 1, 0)])

        def issue_tail(e, c):
            @pl.when(has_rows(e))
            def _():
                tail_copy(e).start()
            return c

        def drain_tail(e, c):
            @pl.when(has_rows(e))
            def _():
                tail_copy(e).wait()
            return c

        lax.fori_loop(0, N_EXPERTS, issue_tail, 0)
        lax.fori_loop(0, N_EXPERTS, drain_tail, 0)

        def dead_copy(m):
            return pltpu.make_async_copy(zero_scr, xs_ref.at[pl.ds(pl.multiple_of(m * MOE_TM, MOE_TM), MOE_TM), :], sem)

        def issue_dead(m, c):
            dead_copy(m).start()
            return c

        def drain_dead(m, c):
            dead_copy(m).wait()
            return c

        n_used = pe_ref[N_EXPERTS - 1] // MOE_TM
        lax.fori_loop(n_used, xs_ref.shape[0] // MOE_TM, issue_dead, 0)
        lax.fori_loop(n_used, xs_ref.shape[0] // MOE_TM, drain_dead, 0)

    step = pl.program_id(0)
    cur = lax.rem(step, 2)
    prev = 1 - cur

    def row_copy(d_ref, b, g, u, k):
        slot = d_ref[(g * SUBLANES + u) * TOP_K + k]
        return pltpu.make_async_copy(stage.at[b, g, pl.ds(u, 1), :], xs_ref.at[pl.ds(slot, 1), :], sems.at[b])

    def loop_rows(fn):
        def body(g, c):
            for u in range(SUBLANES):
                for k in range(TOP_K):
                    fn(g, u, k)
            return c
        lax.fori_loop(0, DISPATCH_TM // SUBLANES, body, 0)

    stage[cur] = hp_ref[...]
    loop_rows(lambda g, u, k: row_copy(dest_ref, cur, g, u, k).start())

    @pl.when(step > 0)
    def _():
        loop_rows(lambda g, u, k: row_copy(dest_prev_ref, prev, g, u, k).wait())

    @pl.when(step == pl.num_programs(0) - 1)
    def _():
        loop_rows(lambda g, u, k: row_copy(dest_ref, cur, g, u, k).wait())


def _dispatch(pad_end, dest, hp, cap):
    t, kp = hp.shape
    return pl.pallas_call(
        _dispatch_kernel,
        out_shape=jax.ShapeDtypeStruct((cap, kp), hp.dtype),
        grid_spec=pltpu.PrefetchScalarGridSpec(
            num_scalar_prefetch=1,
            grid=(t // DISPATCH_TM,),
            in_specs=[pl.BlockSpec((DISPATCH_TM * TOP_K,), lambda i, pe: (i,), memory_space=pltpu.SMEM),
                      pl.BlockSpec((DISPATCH_TM * TOP_K,), lambda i, pe: (jnp.maximum(i - 1, 0),),
                                   memory_space=pltpu.SMEM),
                      pl.BlockSpec((DISPATCH_TM // SUBLANES, SUBLANES, kp), lambda i, pe: (i, 0, 0))],
            out_specs=pl.BlockSpec(memory_space=pl.ANY),
            scratch_shapes=[pltpu.VMEM((MOE_TM, kp), hp.dtype), pltpu.SemaphoreType.DMA(()),
                            pltpu.VMEM((2, DISPATCH_TM // SUBLANES, SUBLANES, kp), hp.dtype),
                            pltpu.SemaphoreType.DMA((2,))]),
        compiler_params=_cparams(("arbitrary",)),
        name="dispatch",
    )(pad_end, dest, dest, hp.reshape(t // SUBLANES, SUBLANES, kp))


def _block_halves(m, half2_ref, nu_ref, out_ref, prepare, half):
    live = m < nu_ref[0]
    second = jnp.logical_and(live, half2_ref[m] == 1)
    zeros = jnp.zeros((MOE_HALF, out_ref.shape[1]), out_ref.dtype)

    @pl.when(live)
    def _():
        prepare()

    @pl.when(second)
    def _():
        half(pl.ds(0, MOE_TM))

    @pl.when(jnp.logical_and(live, jnp.logical_not(second)))
    def _():
        half(pl.ds(0, MOE_HALF))

    @pl.when(jnp.logical_not(live))
    def _():
        out_ref[pl.ds(0, MOE_HALF), :] = zeros

    @pl.when(jnp.logical_not(second))
    def _():
        out_ref[pl.ds(MOE_HALF, MOE_HALF), :] = zeros


def _cast_chunks(w_refs, w_bf):
    rows = w_refs[0].shape[0]
    for i, w_ref in enumerate(w_refs):
        w_bf[pl.ds(i * rows, rows), :] = w_ref[...].astype(bf16)


def _gemm1_kernel(be_ref, first_ref, half2_ref, nu_ref, xs_ref, *refs):
    wg_refs, wl_refs = refs[:W_STREAMS], refs[W_STREAMS:2 * W_STREAMS]
    bg_ref, bl_ref, act_ref, wg_bf, wl_bf = refs[2 * W_STREAMS:]
    m = pl.program_id(1)

    def prepare():
        @pl.when(first_ref[m] == 1)
        def _():
            _cast_chunks(wg_refs, wg_bf)
            _cast_chunks(wl_refs, wl_bf)

    def half(rows):
        lo, hi = _unpack_halves(xs_ref[rows, :])
        lo = lo.astype(bf16)
        hi = hi.astype(bf16)
        k = lo.shape[-1]
        gate = (jnp.dot(lo, wg_bf[:k, :], preferred_element_type=f32)
                + jnp.dot(hi, wg_bf[k:, :], preferred_element_type=f32) + bg_ref[...])
        lin = (jnp.dot(lo, wl_bf[:k, :], preferred_element_type=f32)
               + jnp.dot(hi, wl_bf[k:, :], preferred_element_type=f32) + bl_ref[...])
        x_glu = jnp.minimum(gate, SWIGLU_LIMIT)
        x_lin = jnp.clip(lin, -SWIGLU_LIMIT, SWIGLU_LIMIT)
        act_ref[rows, :] = (x_glu * jax.nn.sigmoid(SWIGLU_ALPHA * x_glu) * (x_lin + 1.0)).astype(act_ref.dtype)

    _block_halves(m, half2_ref, nu_ref, act_ref, prepare, half)


def _gemm1(block_exp, first, half2, n_used, xs, w_gu, b_gu, tn):
    cap, kp = xs.shape
    n_e, d, f2 = w_gu.shape
    ff = f2 // 2
    tn = min(tn, ff)
    nb = cap // MOE_TM
    nj = ff // tn
    live = lambda m, nu: jnp.minimum(m, nu[0] - 1)
    dk = d // W_STREAMS
    wspec = lambda col0, c: pl.BlockSpec((None, dk, tn), lambda j, m, be, fi, h2, nu: (be[m], c, col0 + j))
    return pl.pallas_call(
        _gemm1_kernel,
        out_shape=jax.ShapeDtypeStruct((cap, ff), bf16),
        grid_spec=pltpu.PrefetchScalarGridSpec(
            num_scalar_prefetch=4,
            grid=(nj, nb),
            in_specs=[pl.BlockSpec((MOE_TM, kp), lambda j, m, be, fi, h2, nu: (live(m, nu), 0))]
                     + [wspec(0, c) for c in range(W_STREAMS)] + [wspec(nj, c) for c in range(W_STREAMS)]
                     + [pl.BlockSpec((None, 1, tn), lambda j, m, be, fi, h2, nu: (be[m], 0, j)),
                        pl.BlockSpec((None, 1, tn), lambda j, m, be, fi, h2, nu: (be[m], 0, nj + j))],
            out_specs=pl.BlockSpec((MOE_TM, tn), lambda j, m, be, fi, h2, nu: (m, j)),
            scratch_shapes=[pltpu.VMEM((d, tn), bf16), pltpu.VMEM((d, tn), bf16)]),
        compiler_params=_cparams(("arbitrary", "arbitrary"), 56),
        name="gemm1",
    )(block_exp, first, half2, n_used, xs, *([w_gu] * (2 * W_STREAMS)),
      b_gu.reshape(n_e, 1, f2), b_gu.reshape(n_e, 1, f2))


def _gemm2_kernel(be_ref, first_ref, half2_ref, nu_ref, act_ref, *refs):
    w_refs = refs[:W_STREAMS]
    b_ref, y_ref, w_bf = refs[W_STREAMS:]
    m = pl.program_id(0)

    def prepare():
        @pl.when(first_ref[m] == 1)
        def _():
            _cast_chunks(w_refs, w_bf)

    def half(rows):
        y = jnp.dot(act_ref[rows, :], w_bf[...], preferred_element_type=f32) + b_ref[...]
        y_ref[rows, :] = _pack_halves(y.astype(bf16))

    _block_halves(m, half2_ref, nu_ref, y_ref, prepare, half)


def _gemm2(block_exp, first, half2, n_used, act, w_down, b_down):
    cap, ff = act.shape
    n_e, _, d = w_down.shape
    nb = cap // MOE_TM
    live = lambda m, nu: jnp.minimum(m, nu[0] - 1)
    wspec = lambda c: pl.BlockSpec((None, ff // W_STREAMS, d), lambda m, be, fi, h2, nu: (be[m], c, 0))
    return pl.pallas_call(
        _gemm2_kernel,
        out_shape=jax.ShapeDtypeStruct((cap, d // 2), u32),
        grid_spec=pltpu.PrefetchScalarGridSpec(
            num_scalar_prefetch=4,
            grid=(nb,),
            in_specs=[pl.BlockSpec((MOE_TM, ff), lambda m, be, fi, h2, nu: (live(m, nu), 0))]
                     + [wspec(c) for c in range(W_STREAMS)]
                     + [pl.BlockSpec((None, 1, d), lambda m, be, fi, h2, nu: (be[m], 0, 0))],
            out_specs=pl.BlockSpec((MOE_TM, d // 2), lambda m, be, fi, h2, nu: (m, 0)),
            scratch_shapes=[pltpu.VMEM((ff, d), bf16)]),
        compiler_params=_cparams(("arbitrary",), 56),
        name="gemm2",
    )(block_exp, first, half2, n_used, act, *([w_down] * W_STREAMS), b_down.reshape(n_e, 1, d))


COMBINE_TM = 256


def _combine_kernel(dest_ref, dest_next_ref, yb_ref, route_ref, x1_ref, g2_ref, gain_ref, o_ref, buf, sems):
    step = pl.program_id(0)
    cur = lax.rem(step, 2)
    nxt = 1 - cur
    n_groups = COMBINE_TM // SUBLANES

    def row_copy(d_ref, b, g, u, k):
        slot = d_ref[(g * SUBLANES + u) * TOP_K + k]
        return pltpu.make_async_copy(yb_ref.at[pl.ds(slot, 1), :], buf.at[b, k, g, pl.ds(u, 1), :], sems.at[b])

    def each_row(g, fn):
        for u in range(SUBLANES):
            for k in range(TOP_K):
                fn(g, u, k)

    def loop_rows(fn):
        def body(g, c):
            each_row(g, fn)
            return c
        lax.fori_loop(0, n_groups, body, 0)

    @pl.when(step == 0)
    def _():
        loop_rows(lambda g, u, k: row_copy(dest_ref, cur, g, u, k).start())

    loop_rows(lambda g, u, k: row_copy(dest_ref, cur, g, u, k).wait())

    for g in range(n_groups):
        each_row(g, lambda g, u, k: row_copy(dest_next_ref, nxt, g, u, k).start())

    route = route_ref[...]
    acc = None
    for k in range(TOP_K):
        lo, hi = _unpack_halves(buf[cur, k].reshape(COMBINE_TM, buf.shape[-1]))
        term = route[:, TOP_K + k:TOP_K + k + 1] * jnp.concatenate([lo, hi], axis=-1)
        acc = term if acc is None else acc + term
    x2 = x1_ref[...] + g2_ref[...] * acc
    ms = jnp.mean(x2 * x2, axis=-1, keepdims=True)
    o_ref[...] = x2 * lax.rsqrt(ms + EPS) * gain_ref[...]

    @pl.when(step == pl.num_programs(0) - 1)
    def _():
        loop_rows(lambda g, u, k: row_copy(dest_next_ref, nxt, g, u, k).wait())


def _combine(dest, yb, route, x1, g2, gain):
    b, l, d = x1.shape
    tm = min(COMBINE_TM, l)
    assert tm == COMBINE_TM
    per_b = l // tm
    n_tiles = b * per_b
    return pl.pallas_call(
        _combine_kernel,
        out_shape=jax.ShapeDtypeStruct((b, l, d), f32),
        grid=(n_tiles,),
        in_specs=[pl.BlockSpec((tm * TOP_K,), lambda i: (i,), memory_space=pltpu.SMEM),
                  pl.BlockSpec((tm * TOP_K,), lambda i: (jnp.minimum(i + 1, n_tiles - 1),), memory_space=pltpu.SMEM),
                  pl.BlockSpec(memory_space=pl.ANY),
                  pl.BlockSpec((None, tm, 128), lambda i: (i // per_b, i % per_b, 0)),
                  pl.BlockSpec((None, tm, d), lambda i: (i // per_b, i % per_b, 0)),
                  pl.BlockSpec((None, 1, d), lambda i: (i // per_b, 0, 0)),
                  pl.BlockSpec((1, d), lambda i: (0, 0))],
        out_specs=pl.BlockSpec((None, tm, d), lambda i: (i // per_b, i % per_b, 0)),
        scratch_shapes=[pltpu.VMEM((2, TOP_K, tm // SUBLANES, SUBLANES, d // 2), u32),
                        pltpu.SemaphoreType.DMA((2,))],
        compiler_params=_cparams(("arbitrary",), 40),
        name="combine",
    )(dest, dest, yb, route, x1, g2, gain)


def _routing(route, n_blocks):
    experts = slice(MEMBER_LANE, MEMBER_LANE + N_EXPERTS)
    top_idx = route[:, :TOP_K].astype(jnp.int32)
    incl = _token_cumsum(route)[:, experts]
    rank = (incl - route[:, experts]).astype(jnp.int32)
    counts = incl[-1].astype(jnp.int32)
    padded = (counts + MOE_TM - 1) // MOE_TM * MOE_TM
    pad_end = jnp.cumsum(padded).astype(jnp.int32)
    slot0 = (pad_end - padded)[None, :] + rank
    onehot = top_idx[:, :, None] == jnp.arange(N_EXPERTS, dtype=jnp.int32)[None, None, :]
    dest = jnp.sum(jnp.where(onehot, slot0[:, None, :], 0), axis=-1).astype(jnp.int32)
    block_start = jnp.arange(n_blocks, dtype=jnp.int32) * MOE_TM
    block_exp = jnp.minimum(jnp.sum(pad_end[None, :] <= block_start[:, None], axis=1), N_EXPERTS - 1).astype(jnp.int32)
    first = jnp.concatenate([jnp.ones((1,), jnp.int32), (block_exp[1:] != block_exp[:-1]).astype(jnp.int32)])
    n_used = (pad_end[-1:] // MOE_TM).astype(jnp.int32)
    token_end = (pad_end - padded + counts)[block_exp]
    half2 = (token_end > block_start + MOE_HALF).astype(jnp.int32)
    return dest.reshape(-1), pad_end, block_exp, first, half2, n_used


def _moe(hp, route, x1, g2, norm_final, w_gu, b_gu, w_down, b_down):
    b, l, d = x1.shape
    t = b * l
    n_blocks = -(-(t * TOP_K + N_EXPERTS * (MOE_TM - 1)) // MOE_TM)
    dest, pad_end, block_exp, first, half2, n_used = _routing(route.reshape(t, 128), n_blocks)
    xs = _dispatch(pad_end, dest, hp.reshape(t, d // 2), n_blocks * MOE_TM)
    act = _gemm1(block_exp, first, half2, n_used, xs, w_gu, b_gu, 1024)
    yb = _gemm2(block_exp, first, half2, n_used, act, w_down, b_down)
    return _combine(dest, yb, route, x1, g2, norm_final.reshape(1, d))


def kernel(x, c, ctx, c_ctx, w_ada, b_ada, norm_mix, norm_ffn, w_in, lb_table, hg_norm, rpb, w_out, w_router,
           b_router, w_gu, b_gu, w_down, b_down, norm_final):
    b, l, d = x.shape
    assert w_ada.shape[0] == 1, "single-layer block"
    rows = l // GRID_W

    c16 = jnp.zeros((16, d), f32).at[:b].set(c).at[b].set(c_ctx)
    mod = _ada(c16, w_ada[0], b_ada[0])
    sh1, sc1, g1, sh2, sc2, g2 = [mod[:b, i * d:(i + 1) * d].reshape(b, 1, d) for i in range(6)]
    csh = jnp.broadcast_to(mod[b, :d].reshape(1, 1, d), (b, 1, d))
    csc = jnp.broadcast_to(mod[b, d:2 * d].reshape(1, 1, d), (b, 1, d))

    w_in_bf = w_in[0].astype(bf16)
    gain_mix = norm_mix[0].reshape(1, d)
    px_att = _inproj(x, sh1, sc1, gain_mix, w_in_bf, P_QN, 3, bf16, 1024)
    px_hg = _inproj(x, sh1, sc1, gain_mix, w_in_bf, P_QH, 5, f32, 1024)
    pc_att = _inproj(ctx, csh, csc, gain_mix, w_in_bf, P_KN, 2, bf16, 256)
    pc_hg = _inproj(ctx, csh, csc, gain_mix, w_in_bf, P_FF, 3, f32, 256)

    o_na = _natten(px_att, pc_att, _natten_bias_table(rpb[0], rows))

    lower_bounds = jnp.cumsum(jax.nn.softmax(lb_table.astype(f32), axis=0), axis=0)
    o_hg = _hgrn(px_hg, pc_hg, lower_bounds[0].reshape(2, HEADS * HEAD_DIM), hg_norm[0])

    x1, hp, route = _outproj(o_na, o_hg, x, w_out[0].astype(bf16), g1, sh2, sc2, norm_ffn[0].reshape(1, d),
                             w_router[0], b_router[0], 256)
    return _moe(hp, route, x1, g2, norm_final, w_gu[0], b_gu[0], w_down[0], b_down[0])
```

```python
import functools

import numpy as np
import jax
import jax.numpy as jnp
from jax import lax
from jax.experimental import pallas as pl
from jax.experimental.pallas import tpu as pltpu

f32 = jnp.float32
bf16 = jnp.bfloat16
u32 = jnp.uint32

GRID_W = 64
HEADS = 8
HEAD_DIM = 128
WIN_R = 8
WIN_C = 16
ROPE_THETA = 10000.0
N_EXPERTS = 32
TOP_K = 4
SWIGLU_LIMIT = 7.0
SWIGLU_ALPHA = 1.702
EPS = 1e-6
P_QN, P_KN, P_VN, P_QH, P_FF, P_FB, P_IH, P_GH = range(8)

CHUNK = 64
SUB = 16
CHUNK_UNROLL = 8
ROW_UNROLL = 8
MOE_TM = 512
MOE_HALF = 256
W_STREAMS = 1
NEG_BIG = -1e30
MEMBER_LANE = 32
HI_MASK = 0xFFFF0000

_HIGHEST = lax.Precision.HIGHEST


def _cparams(sem, vmem_mb=None):
    kw = dict(dimension_semantics=sem)
    if vmem_mb is not None:
        kw["vmem_limit_bytes"] = vmem_mb * 1024 * 1024
    return pltpu.CompilerParams(**kw)


def _ada_kernel(c_ref, w_ref, b_ref, o_ref):
    c = c_ref[...]
    cond = c * jax.nn.sigmoid(c)
    o_ref[...] = jnp.dot(cond, w_ref[...], precision=_HIGHEST, preferred_element_type=f32) + b_ref[...]


def _ada(c16, w_ada, b_ada):
    d, n = w_ada.shape
    tn = min(2048, n)
    return pl.pallas_call(
        _ada_kernel,
        out_shape=jax.ShapeDtypeStruct((c16.shape[0], n), f32),
        grid=(n // tn,),
        in_specs=[pl.BlockSpec((c16.shape[0], d), lambda j: (0, 0)),
                  pl.BlockSpec((d, tn), lambda j: (0, j)),
                  pl.BlockSpec((1, tn), lambda j: (0, j))],
        out_specs=pl.BlockSpec((c16.shape[0], tn), lambda j: (0, j)),
        compiler_params=_cparams(("parallel",), 40),
        name="ada",
    )(c16, w_ada, b_ada.reshape(1, n))


def _inproj_kernel(x_ref, shift_ref, scale_ref, gain_ref, w_ref, o_ref, h_scr):
    @pl.when(pl.program_id(2) == 0)
    def _():
        x = x_ref[...]
        ms = jnp.mean(x * x, axis=-1, keepdims=True)
        y = x * lax.rsqrt(ms + EPS) * gain_ref[...]
        h_scr[...] = (y * (1.0 + scale_ref[...]) + shift_ref[...]).astype(bf16)

    acc = jnp.dot(h_scr[...], w_ref[...], preferred_element_type=f32)
    for hh in range(HEADS):
        o_ref[hh] = acc[:, hh * HEAD_DIM:(hh + 1) * HEAD_DIM].astype(o_ref.dtype)


def _inproj(x, shift, scale, gain, w_bf, part_lo, n_parts, out_dtype, tm):
    b, l, d = x.shape
    pw = HEADS * HEAD_DIM
    tm = min(tm, l)
    return pl.pallas_call(
        _inproj_kernel,
        out_shape=jax.ShapeDtypeStruct((b, n_parts * HEADS, l, HEAD_DIM), out_dtype),
        grid=(b, l // tm, n_parts),
        in_specs=[pl.BlockSpec((None, tm, d), lambda bi, mi, ni: (bi, mi, 0)),
                  pl.BlockSpec((None, 1, d), lambda bi, mi, ni: (bi, 0, 0)),
                  pl.BlockSpec((None, 1, d), lambda bi, mi, ni: (bi, 0, 0)),
                  pl.BlockSpec((1, d), lambda bi, mi, ni: (0, 0)),
                  pl.BlockSpec((d, pw), lambda bi, mi, ni: (0, part_lo + ni))],
        out_specs=pl.BlockSpec((None, HEADS, tm, HEAD_DIM), lambda bi, mi, ni: (bi, ni, mi, 0)),
        scratch_shapes=[pltpu.VMEM((tm, d), bf16)],
        compiler_params=_cparams(("parallel", "parallel", "arbitrary"), 48),
        name="inproj",
    )(x, shift, scale, gain, w_bf)


def _natten_bias_table(rpb, rows):
    kr = min(WIN_R, rows)
    q = np.arange(GRID_W)
    col_start = np.clip(q - WIN_C // 2, 0, GRID_W - WIN_C)
    kc = np.arange(GRID_W)
    in_win = (kc[None, :] >= col_start[:, None]) & (kc[None, :] < col_start[:, None] + WIN_C)
    d_col = np.clip(kc[None, :] - q[:, None] + WIN_C - 1, 0, 2 * WIN_C - 2)
    n_d0 = 2 * WIN_R - 1 - (kr - 1)
    onehot = (d_col[None] == np.arange(2 * WIN_C - 1)[:, None, None]).astype(np.float32)
    cols = jnp.einsum("hrc,cqk->hrqk", rpb.astype(f32), jnp.asarray(onehot), precision=_HIGHEST)
    cols = jnp.where(in_win[None, None], cols, -jnp.inf)
    t = jnp.stack([cols[:, d0:d0 + kr] for d0 in range(n_d0)], axis=1)
    t = jnp.transpose(t, (0, 1, 3, 2, 4))
    return t.reshape(rpb.shape[0], n_d0, GRID_W, kr * GRID_W)


def _natten_kernel(q_ref, k_ref, v_ref, kc_ref, vc_ref, bias_ref, o_ref, sw0, sc0, sw1, sc1, *, rows, kr):
    scale = HEAD_DIM ** -0.5
    kc = kc_ref[...]
    vc = vc_ref[...]
    nt = (((1,), (1,)), ((), ()))

    n_groups = rows // ROW_UNROLL
    gq = ROW_UNROLL * GRID_W

    def key_start(r):
        return jnp.clip(r - kr // 2, 0, rows - kr)

    def scores(g, sw_scr, sc_scr):
        q_all = q_ref[pl.ds(pl.multiple_of(g * gq, gq), gq), :]
        sc_scr[...] = lax.dot_general(q_all, kc, nt, preferred_element_type=f32) * scale
        for u in range(ROW_UNROLL):
            r = g * ROW_UNROLL + u
            kr0 = key_start(r)
            kw = k_ref[pl.ds(pl.multiple_of(kr0 * GRID_W, GRID_W), kr * GRID_W), :]
            q = q_all[u * GRID_W:(u + 1) * GRID_W, :]
            sw_scr[u] = (lax.dot_general(q, kw, nt, preferred_element_type=f32) * scale
                         + bias_ref[kr0 - r + WIN_R - 1])

    def attend(g, sw_scr, sc_scr):
        s_c = sc_scr[...]
        s_w = [sw_scr[u] for u in range(ROW_UNROLL)]
        m = jnp.maximum(jnp.concatenate([jnp.max(s, axis=-1, keepdims=True) for s in s_w], axis=0),
                        jnp.max(s_c, axis=-1, keepdims=True))
        p_c = jnp.exp(s_c - m)
        p_w = [jnp.exp(s - m[u * GRID_W:(u + 1) * GRID_W, :]) for u, s in enumerate(s_w)]
        denom = (jnp.concatenate([jnp.sum(p, axis=-1, keepdims=True) for p in p_w], axis=0)
                 + jnp.sum(p_c, axis=-1, keepdims=True))
        o_c = jnp.dot(p_c.astype(bf16), vc, preferred_element_type=f32)
        o_w = []
        for u, p in enumerate(p_w):
            k0 = pl.multiple_of(key_start(g * ROW_UNROLL + u) * GRID_W, GRID_W)
            o_w.append(jnp.dot(p.astype(bf16), v_ref[pl.ds(k0, kr * GRID_W), :], preferred_element_type=f32))
        o = (jnp.concatenate(o_w, axis=0) + o_c) / denom
        o_ref[pl.ds(pl.multiple_of(g * gq, gq), gq), :] = o.astype(o_ref.dtype)

    scores(0, sw0, sc0)

    def body(i, carry):
        g = 2 * i
        scores(g + 1, sw1, sc1)
        attend(g, sw0, sc0)
        scores(jnp.minimum(g + 2, n_groups - 1), sw0, sc0)
        attend(g + 1, sw1, sc1)
        return carry

    lax.fori_loop(0, n_groups // 2, body, 0)


def _natten(px_att, pc_att, bias_tab):
    b, _, l, _ = px_att.shape
    lc = pc_att.shape[2]
    rows = l // GRID_W
    kr = min(WIN_R, rows)
    n_d0 = bias_tab.shape[1]
    blk = lambda off: pl.BlockSpec((None, None, l, HEAD_DIM), lambda h, bi: (bi, off + h, 0, 0))
    cblk = lambda off: pl.BlockSpec((None, None, lc, HEAD_DIM), lambda h, bi: (bi, off + h, 0, 0))
    return pl.pallas_call(
        functools.partial(_natten_kernel, rows=rows, kr=kr),
        out_shape=jax.ShapeDtypeStruct((b, l, HEADS * HEAD_DIM), bf16),
        grid=(HEADS, b),
        in_specs=[blk(0), blk(HEADS), blk(2 * HEADS), cblk(0), cblk(HEADS),
                  pl.BlockSpec((None, n_d0, GRID_W, kr * GRID_W), lambda h, bi: (h, 0, 0, 0))],
        out_specs=pl.BlockSpec((None, l, HEAD_DIM), lambda h, bi: (bi, 0, h)),
        scratch_shapes=[pltpu.VMEM((ROW_UNROLL, GRID_W, kr * GRID_W), f32), pltpu.VMEM((ROW_UNROLL * GRID_W, lc), f32),
                        pltpu.VMEM((ROW_UNROLL, GRID_W, kr * GRID_W), f32), pltpu.VMEM((ROW_UNROLL * GRID_W, lc), f32)],
        compiler_params=_cparams(("parallel", "parallel")),
        name="natten",
    )(px_att, px_att, px_att, pc_att, pc_att, bias_tab)


def _hgrn_consts():
    t = np.arange(CHUNK)
    bt, bs = t[:, None] // SUB, t[None, :] // SUB
    tri, masks = [], []
    for sgn in (1, -1):
        before = (t[None, :] <= t[:, None]) if sgn == 1 else (t[None, :] >= t[:, None])
        tri.append(before.astype(np.float32))
        dist = (bt - bs) * sgn
        masks.append(np.stack([dist == 1, dist == 2, dist == 3, (dist == 0) & before]).astype(np.float32))
    return np.stack(tri), np.stack(masks)


def _rope_tables(l):
    t = jnp.arange(l)
    n_freq = HEAD_DIM // 4
    inv_freq = ROPE_THETA ** (-jnp.arange(n_freq, dtype=f32) / n_freq)
    ang_row = (t // GRID_W).astype(f32)[:, None] * inv_freq
    ang_col = (t % GRID_W).astype(f32)[:, None] * inv_freq
    cr, sr, cc, sc = jnp.cos(ang_row), jnp.sin(ang_row), jnp.cos(ang_col), jnp.sin(ang_col)
    z = jnp.zeros_like(sr)
    cos_t = jnp.concatenate([cr, cr, cc, cc], axis=-1)
    sin_up = jnp.concatenate([-sr, z, -sc, z], axis=-1)
    sin_dn = jnp.concatenate([z, sr, z, sc], axis=-1)
    return cos_t, sin_up, sin_dn


def _split3_dot(tri_bf, g):
    g1 = g.astype(bf16)
    r1 = g - g1.astype(f32)
    g2 = r1.astype(bf16)
    g3 = (r1 - g2.astype(f32)).astype(bf16)
    dot = lambda a: jnp.dot(tri_bf, a, preferred_element_type=f32)
    return dot(g1) + dot(g2) + dot(g3)


def _gate(f_raw, lb):
    log_f = jnp.log(lb + (1.0 - lb) * jax.nn.sigmoid(f_raw))
    key = (1.0 - lb) * jax.nn.sigmoid(-f_raw)
    return log_f, key


def _chunk_refs(cum, backward):
    nb = CHUNK // SUB
    if backward:
        ends = [cum[i * SUB:i * SUB + 1, :] for i in range(nb)]
        order = list(range(nb - 1, -1, -1))
    else:
        ends = [cum[i * SUB + SUB - 1:i * SUB + SUB, :] for i in range(nb)]
        order = list(range(nb))
    zero = jnp.zeros_like(ends[0])
    b_rows, g2_rows, g3_rows = [None] * nb, [None] * nb, [None] * nb
    for pos, i in enumerate(order):
        b_i = zero if pos == 0 else ends[order[pos - 1]]
        b_rows[i] = b_i
        g2_rows[i] = b_i - ends[order[pos - 2]] if pos >= 2 else zero
        g3_rows[i] = b_i - ends[order[pos - 3]] if pos >= 3 else zero
    expand = lambda rws: jnp.concatenate([jnp.broadcast_to(r, (SUB, HEAD_DIM)) for r in rws], axis=0)
    total = ends[order[-1]]
    return expand(b_rows), expand(ends), expand(g2_rows), expand(g3_rows), total


def _rope(x, cos_t, sin_up, sin_dn):
    return x * cos_t + pltpu.roll(x, 96, 1) * sin_up + pltpu.roll(x, 32, 1) * sin_dn


def _scan_group(states, items, tris, masks):
    nt = (((1,), (1,)), ((), ()))
    tn = (((0,), (0,)), ((), ()))
    pre = []
    for d, q, f_raw, v, lb, rope in items:
        g, key = _gate(f_raw, lb)
        pre.append((g, _rope(q, *rope), _rope(key, *rope), v.astype(bf16)))
    cums = [_split3_dot(tris[it[0]], p[0]) for it, p in zip(items, pre)]
    ops = []
    for it, (g, qr, kr, v_bf), cum in zip(items, pre, cums):
        b, e, gap2, gap3, total = _chunk_refs(cum, it[0] == 1)
        q_t = qr * jnp.exp(cum - b)
        k_hat = kr * jnp.exp(e - cum)
        k_til = kr * jnp.exp(b - cum)
        lhs = jnp.concatenate([q_t, q_t * jnp.exp(gap2), q_t * jnp.exp(gap3)], axis=0).astype(bf16)
        rhs = jnp.concatenate([k_hat, k_til], axis=0).astype(bf16)
        q_in = (q_t * jnp.exp(b)).astype(bf16)
        k_dec = (k_hat * jnp.exp(total - e)).astype(bf16)
        ops.append((lhs, rhs, q_in, k_dec, jnp.exp(total)))
    scores = [lax.dot_general(o[0], o[1], nt, preferred_element_type=f32) for o in ops]
    upds = [lax.dot_general(p[3], o[3], tn, preferred_element_type=f32) for p, o in zip(pre, ops)]
    intra = []
    for it, p, (g, qr, kr, v_bf) in zip(items, scores, pre):
        m = masks[it[0]]
        att = (jnp.where(m[0] > 0, p[0:CHUNK, 0:CHUNK], 0.0)
               + jnp.where(m[1] > 0, p[CHUNK:2 * CHUNK, 0:CHUNK], 0.0)
               + jnp.where(m[2] > 0, p[2 * CHUNK:3 * CHUNK, 0:CHUNK], 0.0)
               + jnp.where(m[3] > 0, p[0:CHUNK, CHUNK:2 * CHUNK], 0.0))
        intra.append(jnp.dot(att.astype(bf16), v_bf, preferred_element_type=f32))
    states = list(states)
    outs = []
    for it, o, upd, o_in in zip(items, ops, upds, intra):
        st = states[it[0]]
        outs.append(o_in + lax.dot_general(o[2], st.astype(bf16), nt, preferred_element_type=f32))
        states[it[0]] = st * o[4] + upd
    return states, outs


def _hgrn_kernel(q_ref, ff_ref, fb_ref, v_ref, gate_ref, cff_ref, cfb_ref, cv_ref, lb_ref, gain_ref,
                 cos_ref, sup_ref, sdn_ref, tri_ref, mask_ref, o_ref, of_scr, ob_scr, *, n_chunks, n_cchunks):
    lb_f = lb_ref[0]
    lb_b = lb_ref[1]
    tri_f = tri_ref[0].astype(bf16)
    tri_b = tri_ref[1].astype(bf16)
    zero = jnp.zeros((HEAD_DIM, HEAD_DIM), f32)

    ctx_items = []
    for c in range(n_cchunks):
        rf = pl.ds(c * CHUNK, CHUNK)
        rb = pl.ds((n_cchunks - 1 - c) * CHUNK, CHUNK)
        ctx_items.append((0, cv_ref[rf, :], cff_ref[rf, :], lb_f))
        ctx_items.append((1, cv_ref[rb, :], cfb_ref[rb, :], lb_b))
    st_f, st_b = _state_group([zero, zero], ctx_items, (tri_f, tri_b))

    def body(c, carry):
        masks = [[mask_ref[d, i] for i in range(4)] for d in range(2)]
        items, rows = [], []
        for u in range(CHUNK_UNROLL):
            cf = c * CHUNK_UNROLL + u
            rf = pl.ds(pl.multiple_of(cf * CHUNK, CHUNK), CHUNK)
            rb = pl.ds(pl.multiple_of((n_chunks - 1 - cf) * CHUNK, CHUNK), CHUNK)
            items.append((0, q_ref[rf, :], ff_ref[rf, :], v_ref[rf, :], lb_f,
                          (cos_ref[rf, :], sup_ref[rf, :], sdn_ref[rf, :])))
            items.append((1, q_ref[rb, :], fb_ref[rb, :], v_ref[rb, :], lb_b,
                          (cos_ref[rb, :], sup_ref[rb, :], sdn_ref[rb, :])))
            rows += [(of_scr, rf), (ob_scr, rb)]
        states, outs = _scan_group(carry, items, (tri_f, tri_b), masks)
        for (scr, rws), o in zip(rows, outs):
            scr[rws, :] = o
        return tuple(states)

    lax.fori_loop(0, n_chunks // CHUNK_UNROLL, body, (st_f, st_b))

    o = of_scr[...] + ob_scr[...]
    y = o * lax.rsqrt(jnp.mean(o * o, axis=-1, keepdims=True) + EPS) * gain_ref[...]
    gate = gate_ref[...]
    o_ref[...] = (y * (gate * jax.nn.sigmoid(gate))).astype(o_ref.dtype)


def _state_group(states, items, tris):
    gates = [_gate(f_raw, lb) for _, _, f_raw, lb in items]
    cums = [_split3_dot(tris[it[0]], g) for it, (g, _) in zip(items, gates)]
    terms = []
    for it, (_, key), cum in zip(items, gates, cums):
        total = cum[0:1, :] if it[0] == 1 else cum[CHUNK - 1:CHUNK, :]
        terms.append(((key * jnp.exp(total - cum)).astype(bf16), jnp.exp(total)))
    upds = [lax.dot_general(it[1].astype(bf16), k_dec, (((0,), (0,)), ((), ())), preferred_element_type=f32)
            for it, (k_dec, _) in zip(items, terms)]
    states = list(states)
    for it, (_, decay), upd in zip(items, terms, upds):
        states[it[0]] = states[it[0]] * decay + upd
    return states


def _hgrn(px_hg, pc_hg, lb2, hg_gain):
    b, _, l, _ = px_hg.shape
    lc = pc_hg.shape[2]
    assert l % (CHUNK * CHUNK_UNROLL) == 0 and lc % CHUNK == 0
    cos_t, sin_up, sin_dn = _rope_tables(l)
    tri, masks = _hgrn_consts()
    blk = lambda off: pl.BlockSpec((None, None, l, HEAD_DIM), lambda h, bi: (bi, off + h, 0, 0))
    cblk = lambda off: pl.BlockSpec((None, None, lc, HEAD_DIM), lambda h, bi: (bi, off + h, 0, 0))
    full = lambda shp: pl.BlockSpec(shp, lambda h, bi: (0,) * len(shp))
    return pl.pallas_call(
        functools.partial(_hgrn_kernel, n_chunks=l // CHUNK, n_cchunks=lc // CHUNK),
        out_shape=jax.ShapeDtypeStruct((b, l, HEADS * HEAD_DIM), bf16),
        grid=(HEADS, b),
        in_specs=[blk(0), blk(HEADS), blk(2 * HEADS), blk(3 * HEADS), blk(4 * HEADS),
                  cblk(0), cblk(HEADS), cblk(2 * HEADS),
                  pl.BlockSpec((2, None, 1, HEAD_DIM), lambda h, bi: (0, h, 0, 0)),
                  full((1, HEAD_DIM)),
                  full((l, HEAD_DIM)), full((l, HEAD_DIM)), full((l, HEAD_DIM)),
                  full((2, CHUNK, CHUNK)), full((2, 4, CHUNK, CHUNK))],
        out_specs=pl.BlockSpec((None, l, HEAD_DIM), lambda h, bi: (bi, 0, h)),
        scratch_shapes=[pltpu.VMEM((l, HEAD_DIM), f32), pltpu.VMEM((l, HEAD_DIM), f32)],
        compiler_params=_cparams(("parallel", "parallel")),
        name="hgrn",
    )(px_hg, px_hg, px_hg, px_hg, px_hg, pc_hg, pc_hg, pc_hg,
      lb2.reshape(2, HEADS, 1, HEAD_DIM), hg_gain.reshape(1, HEAD_DIM),
      cos_t, sin_up, sin_dn, jnp.asarray(tri), jnp.asarray(masks))


def _pack_halves(h):
    k = h.shape[-1] // 2
    lo = lax.bitcast_convert_type(h[:, :k].astype(f32), u32)
    hi = lax.bitcast_convert_type(h[:, k:].astype(f32), u32)
    return (lo >> 16) | (hi & u32(HI_MASK))


def _unpack_halves(u):
    lo = lax.bitcast_convert_type(u << 16, f32)
    hi = lax.bitcast_convert_type(u & u32(HI_MASK), f32)
    return lo, hi


def _outproj_kernel(ana_ref, ahg_ref, x_ref, w0_ref, w1_ref, g1_ref, sh_ref, sc_ref, gain_ref, wr_ref, br_ref,
                    x1_ref, hp_ref, route_ref):
    y = (jnp.dot(ana_ref[...], w0_ref[...], preferred_element_type=f32)
         + jnp.dot(ahg_ref[...], w1_ref[...], preferred_element_type=f32))
    x1 = x_ref[...] + g1_ref[...] * y
    x1_ref[...] = x1
    ms = jnp.mean(x1 * x1, axis=-1, keepdims=True)
    h = x1 * lax.rsqrt(ms + EPS) * gain_ref[...]
    h = h * (1.0 + sc_ref[...]) + sh_ref[...]
    h_hi = h.astype(bf16)
    hp_ref[...] = _pack_halves(h_hi)
    h_lo = (h - h_hi.astype(f32)).astype(bf16)
    t = jnp.dot(h_hi, wr_ref[...], preferred_element_type=f32)
    logits = (t[:, :128] + t[:, 128:] + jnp.dot(h_lo, wr_ref[:, :128], preferred_element_type=f32)) + br_ref[...]
    lane = lax.broadcasted_iota(jnp.int32, logits.shape, 1).astype(f32)
    cur = logits
    vals, idxs = [], []
    for _ in range(TOP_K):
        m = jnp.max(cur, axis=-1, keepdims=True)
        i = jnp.min(jnp.where(cur == m, lane, float(logits.shape[-1])), axis=-1, keepdims=True)
        vals.append(m)
        idxs.append(i)
        cur = jnp.where(lane == i, -jnp.inf, cur)
    es = [jnp.exp(v - vals[0]) for v in vals]
    denom = es[0] + es[1] + es[2] + es[3]
    route = jnp.zeros(logits.shape, f32)
    for k in range(TOP_K):
        route = jnp.where(lane == k, idxs[k], route)
        route = jnp.where(lane == TOP_K + k, es[k] / denom, route)
        route = jnp.where(lane - float(MEMBER_LANE) == idxs[k], 1.0, route)
    route_ref[...] = route


def _outproj(a_na, a_hg, x, w_out_bf, g1, sh2, sc2, gain, w_router, b_router, tm):
    b, l, d = x.shape
    hw = a_na.shape[-1]
    tm = min(tm, l)
    n_e = w_router.shape[-1]
    wr = jnp.zeros((d, 128), f32).at[:, :n_e].set(w_router)
    wr_hi = wr.astype(bf16)
    wr = jnp.concatenate([wr_hi, (wr - wr_hi.astype(f32)).astype(bf16)], axis=1)
    br = jnp.full((1, 128), NEG_BIG, f32).at[0, :n_e].set(b_router)
    row = lambda last: pl.BlockSpec((None, tm, last), lambda bi, mi: (bi, mi, 0))
    vec = pl.BlockSpec((None, 1, d), lambda bi, mi: (bi, 0, 0))
    return pl.pallas_call(
        _outproj_kernel,
        out_shape=(jax.ShapeDtypeStruct((b, l, d), f32),
                   jax.ShapeDtypeStruct((b, l, d // 2), u32),
                   jax.ShapeDtypeStruct((b, l, 128), f32)),
        grid=(b, l // tm),
        in_specs=[row(hw), row(hw), row(d),
                  pl.BlockSpec((hw, d), lambda bi, mi: (0, 0)),
                  pl.BlockSpec((hw, d), lambda bi, mi: (1, 0)),
                  vec, vec, vec,
                  pl.BlockSpec((1, d), lambda bi, mi: (0, 0)),
                  pl.BlockSpec((d, 256), lambda bi, mi: (0, 0)),
                  pl.BlockSpec((1, 128), lambda bi, mi: (0, 0))],
        out_specs=(row(d), row(d // 2), row(128)),
        compiler_params=_cparams(("parallel", "parallel"), 48),
        name="outproj",
    )(a_na, a_hg, x, w_out_bf, w_out_bf, g1, sh2, sc2, gain, wr, br)


CUMSUM_TM = 512


def _cumsum_kernel(route_ref, tri_ref, o_ref, carry):
    @pl.when(pl.program_id(0) == 0)
    def _():
        carry[...] = jnp.zeros_like(carry)

    r = route_ref[...]
    lane = lax.broadcasted_iota(jnp.int32, r.shape, 1)
    member = jnp.where((lane >= MEMBER_LANE) & (lane < MEMBER_LANE + N_EXPERTS), r, 0.0).astype(bf16)
    incl = jnp.dot(tri_ref[...], member, preferred_element_type=f32) + carry[...]
    o_ref[...] = incl
    carry[...] = incl[CUMSUM_TM - 1:CUMSUM_TM, :]


def _token_cumsum(route):
    t = route.shape[0]
    tri = jnp.asarray(np.tril(np.ones((CUMSUM_TM, CUMSUM_TM), np.float32)), bf16)
    return pl.pallas_call(
        _cumsum_kernel,
        out_shape=jax.ShapeDtypeStruct((t, 128), f32),
        grid=(t // CUMSUM_TM,),
        in_specs=[pl.BlockSpec((CUMSUM_TM, 128), lambda i: (i, 0)),
                  pl.BlockSpec((CUMSUM_TM, CUMSUM_TM), lambda i: (0, 0))],
        out_specs=pl.BlockSpec((CUMSUM_TM, 128), lambda i: (i, 0)),
        scratch_shapes=[pltpu.VMEM((1, 128), f32)],
        compiler_params=_cparams(("arbitrary",)),
        name="cumsum",
    )(route, tri)


DISPATCH_TM = 256
SUBLANES = 8


def _dispatch_kernel(pe_ref, dest_ref, dest_prev_ref, hp_ref, xs_ref, zero_scr, sem, stage, sems):
    @pl.when(pl.program_id(0) == 0)
    def _():
        zero_scr[...] = jnp.zeros_like(zero_scr)

        def tail_copy(e):
            start = pl.multiple_of(pe_ref[e] - MOE_TM, MOE_TM)
            return pltpu.make_async_copy(zero_scr, xs_ref.at[pl.ds(start, MOE_TM), :], sem)

        def has_rows(e):
            return pe_ref[e] > jnp.where(e == 0, 0, pe_ref[jnp.maximum(e - 1, 0)])

        def issue_tail(e, c):
            @pl.when(has_rows(e))
            def _():
                tail_copy(e).start()
            return c

        def drain_tail(e, c):
            @pl.when(has_rows(e))
            def _():
                tail_copy(e).wait()
            return c

        lax.fori_loop(0, N_EXPERTS, issue_tail, 0)
        lax.fori_loop(0, N_EXPERTS, drain_tail, 0)

        def dead_copy(m):
            return pltpu.make_async_copy(zero_scr, xs_ref.at[pl.ds(pl.multiple_of(m * MOE_TM, MOE_TM), MOE_TM), :], sem)

        def issue_dead(m, c):
            dead_copy(m).start()
            return c

        def drain_dead(m, c):
            dead_copy(m).wait()
            return c

        n_used = pe_ref[N_EXPERTS - 1] // MOE_TM
        lax.fori_loop(n_used, xs_ref.shape[0] // MOE_TM, issue_dead, 0)
        lax.fori_loop(n_used, xs_ref.shape[0] // MOE_TM, drain_dead, 0)

    step = pl.program_id(0)
    cur = lax.rem(step, 2)
    prev = 1 - cur

    def row_copy(d_ref, b, g, u, k):
        slot = d_ref[(g * SUBLANES + u) * TOP_K + k]
        return pltpu.make_async_copy(stage.at[b, g, pl.ds(u, 1), :], xs_ref.at[pl.ds(slot, 1), :], sems.at[b])

    def loop_rows(fn):
        def body(g, c):
            for u in range(SUBLANES):
                for k in range(TOP_K):
                    fn(g, u, k)
            return c
        lax.fori_loop(0, DISPATCH_TM // SUBLANES, body, 0)

    stage[cur] = hp_ref[...]
    loop_rows(lambda g, u, k: row_copy(dest_ref, cur, g, u, k).start(priority=k % 2))

    @pl.when(step > 0)
    def _():
        loop_rows(lambda g, u, k: row_copy(dest_prev_ref, prev, g, u, k).wait())

    @pl.when(step == pl.num_programs(0) - 1)
    def _():
        loop_rows(lambda g, u, k: row_copy(dest_ref, cur, g, u, k).wait())


def _dispatch(pad_end, dest, hp, cap):
    t, kp = hp.shape
    return pl.pallas_call(
        _dispatch_kernel,
        out_shape=jax.ShapeDtypeStruct((cap, kp), hp.dtype),
        grid_spec=pltpu.PrefetchScalarGridSpec(
            num_scalar_prefetch=1,
            grid=(t // DISPATCH_TM,),
            in_specs=[pl.BlockSpec((DISPATCH_TM * TOP_K,), lambda i, pe: (i,), memory_space=pltpu.SMEM),
                      pl.BlockSpec((DISPATCH_TM * TOP_K,), lambda i, pe: (jnp.maximum(i - 1, 0),),
                                   memory_space=pltpu.SMEM),
                      pl.BlockSpec((DISPATCH_TM // SUBLANES, SUBLANES, kp), lambda i, pe: (i, 0, 0))],
            out_specs=pl.BlockSpec(memory_space=pl.ANY),
            scratch_shapes=[pltpu.VMEM((MOE_TM, kp), hp.dtype), pltpu.SemaphoreType.DMA(()),
                            pltpu.VMEM((2, DISPATCH_TM // SUBLANES, SUBLANES, kp), hp.dtype),
                            pltpu.SemaphoreType.DMA((2,))]),
        compiler_params=_cparams(("arbitrary",)),
        name="dispatch",
    )(pad_end, dest, dest, hp.reshape(t // SUBLANES, SUBLANES, kp))


def _block_halves(m, half2_ref, nu_ref, out_ref, prepare, half):
    live = m < nu_ref[0]
    second = jnp.logical_and(live, half2_ref[m] == 1)
    zeros = jnp.zeros((MOE_HALF, out_ref.shape[1]), out_ref.dtype)

    @pl.when(live)
    def _():
        prepare()

    @pl.when(second)
    def _():
        half(pl.ds(0, MOE_TM))

    @pl.when(jnp.logical_and(live, jnp.logical_not(second)))
    def _():
        half(pl.ds(0, MOE_HALF))

    @pl.when(jnp.logical_not(live))
    def _():
        out_ref[pl.ds(0, MOE_HALF), :] = zeros

    @pl.when(jnp.logical_not(second))
    def _():
        out_ref[pl.ds(MOE_HALF, MOE_HALF), :] = zeros


def _cast_chunks(w_refs, w_bf):
    rows = w_refs[0].shape[0]
    for i, w_ref in enumerate(w_refs):
        w_bf[pl.ds(i * rows, rows), :] = w_ref[...].astype(bf16)


def _gemm1_kernel(be_ref, first_ref, half2_ref, nu_ref, xs_ref, *refs):
    wg_refs, wl_refs = refs[:W_STREAMS], refs[W_STREAMS:2 * W_STREAMS]
    bg_ref, bl_ref, act_ref, wg_bf, wl_bf = refs[2 * W_STREAMS:]
    m = pl.program_id(1)

    def prepare():
        @pl.when(first_ref[m] == 1)
        def _():
            _cast_chunks(wg_refs, wg_bf)
            _cast_chunks(wl_refs, wl_bf)

    def half(rows):
        lo, hi = _unpack_halves(xs_ref[rows, :])
        lo = lo.astype(bf16)
        hi = hi.astype(bf16)
        k = lo.shape[-1]
        gate = (jnp.dot(lo, wg_bf[:k, :], preferred_element_type=f32)
                + jnp.dot(hi, wg_bf[k:, :], preferred_element_type=f32) + bg_ref[...])
        lin = (jnp.dot(lo, wl_bf[:k, :], preferred_element_type=f32)
               + jnp.dot(hi, wl_bf[k:, :], preferred_element_type=f32) + bl_ref[...])
        x_glu = jnp.minimum(gate, SWIGLU_LIMIT)
        x_lin = jnp.clip(lin, -SWIGLU_LIMIT, SWIGLU_LIMIT)
        act_ref[rows, :] = (x_glu * jax.nn.sigmoid(SWIGLU_ALPHA * x_glu) * (x_lin + 1.0)).astype(act_ref.dtype)

    _block_halves(m, half2_ref, nu_ref, act_ref, prepare, half)


def _gemm1(block_exp, first, half2, n_used, xs, w_gu, b_gu, tn):
    cap, kp = xs.shape
    n_e, d, f2 = w_gu.shape
    ff = f2 // 2
    tn = min(tn, ff)
    nb = cap // MOE_TM
    nj = ff // tn
    live = lambda m, nu: jnp.minimum(m, nu[0] - 1)
    dk = d // W_STREAMS
    wspec = lambda col0, c: pl.BlockSpec((None, dk, tn), lambda j, m, be, fi, h2, nu: (be[m], c, col0 + j))
    return pl.pallas_call(
        _gemm1_kernel,
        out_shape=jax.ShapeDtypeStruct((cap, ff), bf16),
        grid_spec=pltpu.PrefetchScalarGridSpec(
            num_scalar_prefetch=4,
            grid=(nj, nb),
            in_specs=[pl.BlockSpec((MOE_TM, kp), lambda j, m, be, fi, h2, nu: (live(m, nu), 0))]
                     + [wspec(0, c) for c in range(W_STREAMS)] + [wspec(nj, c) for c in range(W_STREAMS)]
                     + [pl.BlockSpec((None, 1, tn), lambda j, m, be, fi, h2, nu: (be[m], 0, j)),
                        pl.BlockSpec((None, 1, tn), lambda j, m, be, fi, h2, nu: (be[m], 0, nj + j))],
            out_specs=pl.BlockSpec((MOE_TM, tn), lambda j, m, be, fi, h2, nu: (m, j)),
            scratch_shapes=[pltpu.VMEM((d, tn), bf16), pltpu.VMEM((d, tn), bf16)]),
        compiler_params=_cparams(("arbitrary", "arbitrary"), 56),
        name="gemm1",
    )(block_exp, first, half2, n_used, xs, *([w_gu] * (2 * W_STREAMS)),
      b_gu.reshape(n_e, 1, f2), b_gu.reshape(n_e, 1, f2))


def _gemm2_kernel(be_ref, first_ref, half2_ref, nu_ref, act_ref, *refs):
    w_refs = refs[:W_STREAMS]
    b_ref, y_ref, w_bf = refs[W_STREAMS:]
    m = pl.program_id(0)

    def prepare():
        @pl.when(first_ref[m] == 1)
        def _():
            _cast_chunks(w_refs, w_bf)

    def half(rows):
        y = jnp.dot(act_ref[rows, :], w_bf[...], preferred_element_type=f32) + b_ref[...]
        y_ref[rows, :] = _pack_halves(y.astype(bf16))

    _block_halves(m, half2_ref, nu_ref, y_ref, prepare, half)


def _gemm2(block_exp, first, half2, n_used, act, w_down, b_down):
    cap, ff = act.shape
    n_e, _, d = w_down.shape
    nb = cap // MOE_TM
    live = lambda m, nu: jnp.minimum(m, nu[0] - 1)
    wspec = lambda c: pl.BlockSpec((None, ff // W_STREAMS, d), lambda m, be, fi, h2, nu: (be[m], c, 0))
    return pl.pallas_call(
        _gemm2_kernel,
        out_shape=jax.ShapeDtypeStruct((cap, d // 2), u32),
        grid_spec=pltpu.PrefetchScalarGridSpec(
            num_scalar_prefetch=4,
            grid=(nb,),
            in_specs=[pl.BlockSpec((MOE_TM, ff), lambda m, be, fi, h2, nu: (live(m, nu), 0))]
                     + [wspec(c) for c in range(W_STREAMS)]
                     + [pl.BlockSpec((None, 1, d), lambda m, be, fi, h2, nu: (be[m], 0, 0))],
            out_specs=pl.BlockSpec((MOE_TM, d // 2), lambda m, be, fi, h2, nu: (m, 0)),
            scratch_shapes=[pltpu.VMEM((ff, d), bf16)]),
        compiler_params=_cparams(("arbitrary",), 56),
        name="gemm2",
    )(block_exp, first, half2, n_used, act, *([w_down] * W_STREAMS), b_down.reshape(n_e, 1, d))


COMBINE_TM = 256


def _combine_kernel(dest_ref, dest_next_ref, yb_ref, route_ref, x1_ref, g2_ref, gain_ref, o_ref, buf, sems):
    step = pl.program_id(0)
    cur = lax.rem(step, 2)
    nxt = 1 - cur
    n_groups = COMBINE_TM // SUBLANES

    def row_copy(d_ref, b, g, u, k):
        slot = d_ref[(g * SUBLANES + u) * TOP_K + k]
        return pltpu.make_async_copy(yb_ref.at[pl.ds(slot, 1), :], buf.at[b, k, g, pl.ds(u, 1), :], sems.at[b])

    def each_row(g, fn):
        for u in range(SUBLANES):
            for k in range(TOP_K):
                fn(g, u, k)

    def loop_rows(fn):
        def body(g, c):
            each_row(g, fn)
            return c
        lax.fori_loop(0, n_groups, body, 0)

    @pl.when(step == 0)
    def _():
        loop_rows(lambda g, u, k: row_copy(dest_ref, cur, g, u, k).start(priority=k % 2))

    loop_rows(lambda g, u, k: row_copy(dest_ref, cur, g, u, k).wait())

    for g in range(n_groups):
        each_row(g, lambda g, u, k: row_copy(dest_next_ref, nxt, g, u, k).start(priority=k % 2))

    route = route_ref[...]
    acc = None
    for k in range(TOP_K):
        lo, hi = _unpack_halves(buf[cur, k].reshape(COMBINE_TM, buf.shape[-1]))
        term = route[:, TOP_K + k:TOP_K + k + 1] * jnp.concatenate([lo, hi], axis=-1)
        acc = term if acc is None else acc + term
    x2 = x1_ref[...] + g2_ref[...] * acc
    ms = jnp.mean(x2 * x2, axis=-1, keepdims=True)
    o_ref[...] = x2 * lax.rsqrt(ms + EPS) * gain_ref[...]

    @pl.when(step == pl.num_programs(0) - 1)
    def _():
        loop_rows(lambda g, u, k: row_copy(dest_next_ref, nxt, g, u, k).wait())


def _combine(dest, yb, route, x1, g2, gain):
    b, l, d = x1.shape
    tm = min(COMBINE_TM, l)
    assert tm == COMBINE_TM
    per_b = l // tm
    n_tiles = b * per_b
    return pl.pallas_call(
        _combine_kernel,
        out_shape=jax.ShapeDtypeStruct((b, l, d), f32),
        grid=(n_tiles,),
        in_specs=[pl.BlockSpec((tm * TOP_K,), lambda i: (i,), memory_space=pltpu.SMEM),
                  pl.BlockSpec((tm * TOP_K,), lambda i: (jnp.minimum(i + 1, n_tiles - 1),), memory_space=pltpu.SMEM),
                  pl.BlockSpec(memory_space=pl.ANY),
                  pl.BlockSpec((None, tm, 128), lambda i: (i // per_b, i % per_b, 0)),
                  pl.BlockSpec((None, tm, d), lambda i: (i // per_b, i % per_b, 0)),
                  pl.BlockSpec((None, 1, d), lambda i: (i // per_b, 0, 0)),
                  pl.BlockSpec((1, d), lambda i: (0, 0))],
        out_specs=pl.BlockSpec((None, tm, d), lambda i: (i // per_b, i % per_b, 0)),
        scratch_shapes=[pltpu.VMEM((2, TOP_K, tm // SUBLANES, SUBLANES, d // 2), u32),
                        pltpu.SemaphoreType.DMA((2,))],
        compiler_params=_cparams(("arbitrary",), 40),
        name="combine",
    )(dest, dest, yb, route, x1, g2, gain)


def _routing(route, n_blocks):
    experts = slice(MEMBER_LANE, MEMBER_LANE + N_EXPERTS)
    top_idx = route[:, :TOP_K].astype(jnp.int32)
    incl = _token_cumsum(route)[:, experts]
    rank = (incl - route[:, experts]).astype(jnp.int32)
    counts = incl[-1].astype(jnp.int32)
    padded = (counts + MOE_TM - 1) // MOE_TM * MOE_TM
    pad_end = jnp.cumsum(padded).astype(jnp.int32)
    slot0 = (pad_end - padded)[None, :] + rank
    onehot = top_idx[:, :, None] == jnp.arange(N_EXPERTS, dtype=jnp.int32)[None, None, :]
    dest = jnp.sum(jnp.where(onehot, slot0[:, None, :], 0), axis=-1).astype(jnp.int32)
    block_start = jnp.arange(n_blocks, dtype=jnp.int32) * MOE_TM
    block_exp = jnp.minimum(jnp.sum(pad_end[None, :] <= block_start[:, None], axis=1), N_EXPERTS - 1).astype(jnp.int32)
    first = jnp.concatenate([jnp.ones((1,), jnp.int32), (block_exp[1:] != block_exp[:-1]).astype(jnp.int32)])
    n_used = (pad_end[-1:] // MOE_TM).astype(jnp.int32)
    token_end = (pad_end - padded + counts)[block_exp]
    half2 = (token_end > block_start + MOE_HALF).astype(jnp.int32)
    return dest.reshape(-1), pad_end, block_exp, first, half2, n_used


def _moe(hp, route, x1, g2, norm_final, w_gu, b_gu, w_down, b_down):
    b, l, d = x1.shape
    t = b * l
    n_blocks = -(-(t * TOP_K + N_EXPERTS * (MOE_TM - 1)) // MOE_TM)
    dest, pad_end, block_exp, first, half2, n_used = _routing(route.reshape(t, 128), n_blocks)
    xs = _dispatch(pad_end, dest, hp.reshape(t, d // 2), n_blocks * MOE_TM)
    act = _gemm1(block_exp, first, half2, n_used, xs, w_gu, b_gu, 1024)
    yb = _gemm2(block_exp, first, half2, n_used, act, w_down, b_down)
    return _combine(dest, yb, route, x1, g2, norm_final.reshape(1, d))


def kernel(x, c, ctx, c_ctx, w_ada, b_ada, norm_mix, norm_ffn, w_in, lb_table, hg_norm, rpb, w_out, w_router,
           b_router, w_gu, b_gu, w_down, b_down, norm_final):
    b, l, d = x.shape
    assert w_ada.shape[0] == 1, "single-layer block"
    rows = l // GRID_W

    c16 = jnp.zeros((16, d), f32).at[:b].set(c).at[b].set(c_ctx)
    mod = _ada(c16, w_ada[0], b_ada[0])
    sh1, sc1, g1, sh2, sc2, g2 = [mod[:b, i * d:(i + 1) * d].reshape(b, 1, d) for i in range(6)]
    csh = jnp.broadcast_to(mod[b, :d].reshape(1, 1, d), (b, 1, d))
    csc = jnp.broadcast_to(mod[b, d:2 * d].reshape(1, 1, d), (b, 1, d))

    w_in_bf = w_in[0].astype(bf16)
    gain_mix = norm_mix[0].reshape(1, d)
    px_att = _inproj(x, sh1, sc1, gain_mix, w_in_bf, P_QN, 3, bf16, 1024)
    px_hg = _inproj(x, sh1, sc1, gain_mix, w_in_bf, P_QH, 5, f32, 1024)
    pc_att = _inproj(ctx, csh, csc, gain_mix, w_in_bf, P_KN, 2, bf16, 256)
    pc_hg = _inproj(ctx, csh, csc, gain_mix, w_in_bf, P_FF, 3, f32, 256)

    o_na = _natten(px_att, pc_att, _natten_bias_table(rpb[0], rows))

    lower_bounds = jnp.cumsum(jax.nn.softmax(lb_table.astype(f32), axis=0), axis=0)
    o_hg = _hgrn(px_hg, pc_hg, lower_bounds[0].reshape(2, HEADS * HEAD_DIM), hg_norm[0])

    x1, hp, route = _outproj(o_na, o_hg, x, w_out[0].astype(bf16), g1, sh2, sc2, norm_ffn[0].reshape(1, d),
                             w_router[0], b_router[0], 256)
    return _moe(hp, route, x1, g2, norm_final, w_gu[0], b_gu[0], w_down[0], b_down[0])
```
